```python
import math
import jax
import jax.numpy as jnp
from jax import lax
import numpy as np

D_MODEL = 1024
BATCH = 4
SEQ = 8192
DEPTH = 4

CTX_LEN = 256
GRID_W = 64
HEAD_DIM = 64
ROPE_THETA = 10000.0
BLOCK_Q = 128
EPS = 1e-6
A_HEADS = 8
A_KV_HEADS = 2
B_HEADS = 4
B_V_DIM = 2 * HEAD_DIM
C_HEADS = 8
NA_KH = 8
NA_KW = 16
N_BRANCH = 3
IN_SIZES = (A_HEADS * HEAD_DIM, A_KV_HEADS * HEAD_DIM, A_KV_HEADS * HEAD_DIM,
            2 * B_HEADS * HEAD_DIM, 2 * B_HEADS * HEAD_DIM, B_HEADS * B_V_DIM,
            C_HEADS * HEAD_DIM, C_HEADS * HEAD_DIM, C_HEADS * HEAD_DIM,
            N_BRANCH * D_MODEL)
D_IN = sum(IN_SIZES)
D_FF = 2816
N_EXPERTS = 8
TOP_K = 2
D_FF_EXPERT = 3584

kernel_name = 'hybrid_gated_dit_block'


def rmsnorm(x, g):
    xf = x.astype(jnp.float32)
    y = xf * lax.rsqrt(jnp.mean(xf * xf, axis=-1, keepdims=True) + EPS)
    return (y * g.astype(jnp.float32)).astype(x.dtype)


def modulate(h, shift, scale):
    return h * (1.0 + scale) + shift


def heads(t, n):
    return t.reshape(t.shape[:-1] + (n, t.shape[-1] // n))


def split_in(p):
    out, start = [], 0
    for size in IN_SIZES:
        out.append(p[..., start:start + size])
        start += size
    return out


def axial_rope_tables(n_tokens):
    t = jnp.arange(n_tokens, dtype=jnp.int32)
    row = (t // GRID_W).astype(jnp.float32)
    col = (t % GRID_W).astype(jnp.float32)
    n_pairs = HEAD_DIM // 4
    inv = ROPE_THETA ** (-jnp.arange(n_pairs, dtype=jnp.float32) / n_pairs)
    ang = jnp.concatenate([row[:, None] * inv, col[:, None] * inv], axis=-1)
    return jnp.cos(ang), jnp.sin(ang)


def apply_rope(x, cos, sin):
    xf = x.astype(jnp.float32).reshape(x.shape[:-1] + (HEAD_DIM // 2, 2))
    x1, x2 = xf[..., 0], xf[..., 1]
    cs = cos[None, :, None, :]
    sn = sin[None, :, None, :]
    out = jnp.stack([x1 * cs - x2 * sn, x1 * sn + x2 * cs], axis=-1)
    return out.reshape(x.shape).astype(x.dtype)


def to_blocks(t):
    b, l = t.shape[:2]
    return jnp.moveaxis(t.reshape((b, l // BLOCK_Q, BLOCK_Q) + t.shape[2:]), 1, 0)


def from_blocks(o):
    nb, b, bq = o.shape[:3]
    return jnp.moveaxis(o, 0, 1).reshape((b, nb * bq) + o.shape[3:])


def gqa_attend(q, k, v):
    s = jnp.einsum('bqhgd,bnhd->bhgqn', q, k).astype(jnp.float32) * (q.shape[-1] ** -0.5)
    p = jax.nn.softmax(s, axis=-1).astype(v.dtype)
    return jnp.einsum('bhgqn,bnhd->bqhgd', p, v)


def diff_attend(q1, q2, k1, k2, v, lam):
    scale = q1.shape[-1] ** -0.5
    s1 = jnp.einsum('bqhd,bnhd->bhqn', q1, k1).astype(jnp.float32) * scale
    s2 = jnp.einsum('bqhd,bnhd->bhqn', q2, k2).astype(jnp.float32) * scale
    p = jax.nn.softmax(s1, axis=-1) - lam * jax.nn.softmax(s2, axis=-1)
    return jnp.einsum('bhqn,bnhd->bqhd', p.astype(v.dtype), v)


def neighbourhood_attend(q, k, v, kc, vc, rpb):
    b, l, h, d = q.shape
    rows = l // GRID_W
    kh = min(NA_KH, rows)
    kg = k.reshape(b, rows, GRID_W, h, d)
    vg = v.reshape(b, rows, GRID_W, h, d)
    qg_rows = jnp.moveaxis(q.reshape(b, rows, GRID_W, h, d), 1, 0)
    col = jnp.arange(GRID_W, dtype=jnp.int32)
    cs = jnp.clip(col - NA_KW // 2, 0, GRID_W - NA_KW)
    col_idx = cs[:, None] + jnp.arange(NA_KW, dtype=jnp.int32)[None, :]
    dc = col_idx - col[:, None] + (NA_KW - 1)
    scale = d ** -0.5
    n_nb = kh * NA_KW

    def row_block(args):
        r, q_row = args
        rs = jnp.clip(r - kh // 2, 0, rows - kh)
        k_band = lax.dynamic_slice_in_dim(kg, rs, kh, axis=1)
        v_band = lax.dynamic_slice_in_dim(vg, rs, kh, axis=1)
        k_nb = k_band[:, :, col_idx]
        v_nb = v_band[:, :, col_idx]
        dr = rs + jnp.arange(kh, dtype=jnp.int32) - r + (NA_KH - 1)
        bias = jnp.transpose(rpb[:, dr[:, None, None], dc[None, :, :]], (0, 2, 1, 3))
        s_nb = jnp.einsum('bqhd,biqjhd->bhqij', q_row, k_nb).astype(jnp.float32) * scale
        s_nb = (s_nb + bias[None].astype(jnp.float32)).reshape(b, h, GRID_W, n_nb)
        s_c = jnp.einsum('bqhd,bnhd->bhqn', q_row, kc).astype(jnp.float32) * scale
        p = jax.nn.softmax(jnp.concatenate([s_nb, s_c], axis=-1), axis=-1)
        p_nb = p[..., :n_nb].reshape(b, h, GRID_W, kh, NA_KW).astype(v.dtype)
        p_c = p[..., n_nb:].astype(v.dtype)
        return (jnp.einsum('bhqij,biqjhd->bqhd', p_nb, v_nb)
                + jnp.einsum('bhqn,bnhd->bqhd', p_c, vc))

    out = lax.map(row_block, (jnp.arange(rows, dtype=jnp.int32), qg_rows))
    return jnp.moveaxis(out, 0, 1).reshape(b, l, h, d)


def qkv_a(parts, qn, kn, cos, sin):
    q = rmsnorm(heads(parts[0], A_HEADS), qn)
    k = rmsnorm(heads(parts[1], A_KV_HEADS), kn)
    v = heads(parts[2], A_KV_HEADS)
    if cos is not None:
        q = apply_rope(q, cos, sin)
        k = apply_rope(k, cos, sin)
    q = q.reshape(q.shape[:2] + (A_KV_HEADS, A_HEADS // A_KV_HEADS, HEAD_DIM))
    return q, k, v


def qkv_b(parts, cos, sin):
    q = parts[3].reshape(parts[3].shape[:2] + (2, B_HEADS, HEAD_DIM))
    k = parts[4].reshape(parts[4].shape[:2] + (2, B_HEADS, HEAD_DIM))
    q1, q2, k1, k2 = q[:, :, 0], q[:, :, 1], k[:, :, 0], k[:, :, 1]
    if cos is not None:
        q1, q2 = apply_rope(q1, cos, sin), apply_rope(q2, cos, sin)
        k1, k2 = apply_rope(k1, cos, sin), apply_rope(k2, cos, sin)
    return q1, q2, k1, k2, heads(parts[5], B_HEADS)


def merge_branches(o_a, o_b, o_c, gate_logits, w_br_a, w_br_b, w_br_c, w_out):
    flat = lambda t: t.reshape(t.shape[:2] + (-1,))
    g = jax.nn.sigmoid(gate_logits.astype(jnp.float32)).astype(o_a.dtype)
    g_a, g_b, g_c = jnp.split(g, N_BRANCH, axis=-1)
    m = g_a * (flat(o_a) @ w_br_a) + g_b * (flat(o_b) @ w_br_b) + g_c * (flat(o_c) @ w_br_c)
    return m @ w_out


def swiglu(h, w1, w3, w2):
    return (jax.nn.silu(h @ w1) * (h @ w3)) @ w2


def moe_ffn(h, router, w1, w3, w2):
    logits = (h @ router).astype(jnp.float32)
    top_v, top_i = lax.top_k(logits, TOP_K)
    top_w = jax.nn.softmax(top_v, axis=-1)
    combine = jnp.sum(jax.nn.one_hot(top_i, N_EXPERTS, dtype=jnp.float32) * top_w[..., None], axis=-2)
    combine = combine.astype(h.dtype)
    out = jnp.zeros_like(h)
    for e in range(N_EXPERTS):
        out = out + combine[..., e:e + 1] * swiglu(h, w1[e], w3[e], w2[e])
    return out


def channel_mixer(h, layer, ffn_w1, ffn_w3, ffn_w2, router, moe_w1, moe_w3, moe_w2):
    i = layer // 2
    if layer % 2 == 0:
        return swiglu(h, ffn_w1[i], ffn_w3[i], ffn_w2[i])
    return moe_ffn(h, router[i], moe_w1[i], moe_w3[i], moe_w2[i])


def setup_inputs(seed: int = 0) -> dict:
    key = jax.random.key(seed)
    ks = iter(jax.random.split(key, 32))
    n_dense = (DEPTH + 1) // 2
    n_moe = DEPTH // 2

    def nrm(shape, scale):
        return jax.random.normal(next(ks), shape, jnp.float32) * scale

    d = D_MODEL
    return {
        'x': nrm((BATCH, SEQ, d), 1.0),
        'c': nrm((BATCH, d), 1.0),
        'ctx': nrm((BATCH, CTX_LEN, d), 1.0),
        'c_ctx': nrm((d,), 1.0),
        'w_mod': nrm((DEPTH, d, 6 * d), d ** -0.5),
        'b_mod': nrm((DEPTH, 6 * d), 0.02),
        'g_mix': 1.0 + nrm((DEPTH, d), 0.05),
        'g_ffn': 1.0 + nrm((DEPTH, d), 0.05),
        'w_in': nrm((DEPTH, d, D_IN), d ** -0.5),
        'qn_a': 1.0 + nrm((DEPTH, HEAD_DIM), 0.05),
        'kn_a': 1.0 + nrm((DEPTH, HEAD_DIM), 0.05),
        'lam_q1': nrm((DEPTH, HEAD_DIM), 0.1),
        'lam_k1': nrm((DEPTH, HEAD_DIM), 0.1),
        'lam_q2': nrm((DEPTH, HEAD_DIM), 0.1),
        'lam_k2': nrm((DEPTH, HEAD_DIM), 0.1),
        'subln_b': 1.0 + nrm((DEPTH, B_V_DIM), 0.05),
        'rpb_c': nrm((DEPTH, C_HEADS, 2 * NA_KH - 1, 2 * NA_KW - 1), 0.1),
        'w_br_a': nrm((DEPTH, A_HEADS * HEAD_DIM, d), (A_HEADS * HEAD_DIM) ** -0.5),
        'w_br_b': nrm((DEPTH, B_HEADS * B_V_DIM, d), (B_HEADS * B_V_DIM) ** -0.5),
        'w_br_c': nrm((DEPTH, C_HEADS * HEAD_DIM, d), (C_HEADS * HEAD_DIM) ** -0.5),
        'w_out': nrm((DEPTH, d, d), d ** -0.5),
        'ffn_w1': nrm((n_dense, d, D_FF), d ** -0.5),
        'ffn_w3': nrm((n_dense, d, D_FF), d ** -0.5),
        'ffn_w2': nrm((n_dense, D_FF, d), D_FF ** -0.5),
        'router': nrm((n_moe, d, N_EXPERTS), d ** -0.5),
        'moe_w1': nrm((n_moe, N_EXPERTS, d, D_FF_EXPERT), d ** -0.5),
        'moe_w3': nrm((n_moe, N_EXPERTS, d, D_FF_EXPERT), d ** -0.5),
        'moe_w2': nrm((n_moe, N_EXPERTS, D_FF_EXPERT, d), D_FF_EXPERT ** -0.5),
        'g_final': 1.0 + nrm((d,), 0.05),
    }


def reference(x, c, ctx, c_ctx, w_mod, b_mod, g_mix, g_ffn, w_in, qn_a, kn_a,
              lam_q1, lam_k1, lam_q2, lam_k2, subln_b, rpb_c, w_br_a, w_br_b, w_br_c,
              w_out, ffn_w1, ffn_w3, ffn_w2, router, moe_w1, moe_w3, moe_w2, g_final):
    n_lat = x.shape[1]
    cos, sin = axial_rope_tables(n_lat)
    silu_c = jax.nn.silu(c)
    silu_cc = jax.nn.silu(c_ctx)
    xl, xc = x, ctx
    for layer in range(DEPTH):
        update_ctx = layer < DEPTH - 1
        mod_l = jnp.split((silu_c @ w_mod[layer] + b_mod[layer])[:, None, :], 6, axis=-1)
        mod_c = jnp.split(silu_cc @ w_mod[layer] + b_mod[layer], 6, axis=-1)
        h_l = modulate(rmsnorm(xl, g_mix[layer]), mod_l[0], mod_l[1])
        h_c = modulate(rmsnorm(xc, g_mix[layer]), mod_c[0], mod_c[1])
        pl = split_in(h_l @ w_in[layer])
        pc = split_in(h_c @ w_in[layer])

        qa_l, ka_l, va_l = qkv_a(pl, qn_a[layer], kn_a[layer], cos, sin)
        qa_c, ka_c, va_c = qkv_a(pc, qn_a[layer], kn_a[layer], None, None)
        ka_all = jnp.concatenate([ka_c, ka_l], axis=1)
        va_all = jnp.concatenate([va_c, va_l], axis=1)
        o_a_l = from_blocks(lax.map(lambda qb: gqa_attend(qb, ka_all, va_all), to_blocks(qa_l)))

        lam_init = 0.8 - 0.6 * math.exp(-0.3 * layer)
        lam = (jnp.exp(jnp.sum((lam_q1[layer] * lam_k1[layer]).astype(jnp.float32)))
               - jnp.exp(jnp.sum((lam_q2[layer] * lam_k2[layer]).astype(jnp.float32))) + lam_init)
        q1_l, q2_l, k1_l, k2_l, vb_l = qkv_b(pl, cos, sin)
        q1_c, q2_c, k1_c, k2_c, vb_c = qkv_b(pc, None, None)
        k1_all = jnp.concatenate([k1_c, k1_l], axis=1)
        k2_all = jnp.concatenate([k2_c, k2_l], axis=1)
        vb_all = jnp.concatenate([vb_c, vb_l], axis=1)
        o_b_l = from_blocks(lax.map(
            lambda qs: diff_attend(qs[0], qs[1], k1_all, k2_all, vb_all, lam),
            (to_blocks(q1_l), to_blocks(q2_l))))
        o_b_l = rmsnorm(o_b_l, subln_b[layer]) * (1.0 - lam_init)

        qc_l, kc_l, vc_l = [heads(t, C_HEADS) for t in pl[6:9]]
        qc_c, kc_c, vc_c = [heads(t, C_HEADS) for t in pc[6:9]]
        o_c_l = neighbourhood_attend(qc_l, kc_l, vc_l, kc_c, vc_c, rpb_c[layer])

        xl = xl + mod_l[2] * merge_branches(o_a_l, o_b_l, o_c_l, pl[9], w_br_a[layer],
                                            w_br_b[layer], w_br_c[layer], w_out[layer])
        h2 = modulate(rmsnorm(xl, g_ffn[layer]), mod_l[3], mod_l[4])
        xl = xl + mod_l[5] * channel_mixer(h2, layer, ffn_w1, ffn_w3, ffn_w2,
                                           router, moe_w1, moe_w3, moe_w2)

        if update_ctx:
            o_a_c = gqa_attend(qa_c, ka_c, va_c)
            o_b_c = rmsnorm(diff_attend(q1_c, q2_c, k1_c, k2_c, vb_c, lam), subln_b[layer]) * (1.0 - lam_init)
            o_c_c = gqa_attend(qc_c[:, :, :, None, :], kc_c, vc_c)
            xc = xc + mod_c[2] * merge_branches(o_a_c, o_b_c, o_c_c, pc[9], w_br_a[layer],
                                                w_br_b[layer], w_br_c[layer], w_out[layer])
            h2c = modulate(rmsnorm(xc, g_ffn[layer]), mod_c[3], mod_c[4])
            xc = xc + mod_c[5] * channel_mixer(h2c, layer, ffn_w1, ffn_w3, ffn_w2,
                                               router, moe_w1, moe_w3, moe_w2)
    return rmsnorm(xl, g_final)
```

```python
import functools
import math

import numpy as np
import jax
import jax.numpy as jnp
from jax import lax
from jax.experimental import pallas as pl
from jax.experimental.pallas import tpu as pltpu

F32 = jnp.float32
BF16 = jnp.bfloat16
HIGHEST = lax.Precision.HIGHEST

D_MODEL = 1024
HEAD_DIM = 64
HALF = HEAD_DIM // 2
GRID_W = 64
ROPE_THETA = 10000.0
EPS = 1e-6
N_HEADS = 8
A_GROUP = 4
B_HEADS = 4
B_V_DIM = 128
NA_KH = 8
NA_KW = 16
N_EXPERTS = 8
LANES = 128
CHUNK = 256
ROW_TILE = 768
Q_TILE = 512
KV_TILE = 768
NBR_ROWS = 8
BAND_ROWS = 16
NEG = -1e30
SM_SCALE = HEAD_DIM ** -0.5
VMEM_LIMIT = 56 * 1024 * 1024

_OFF = {}
_o = 0
for _name, _size in (("aq", 512), ("ak", 128), ("av", 128), ("bq", 512), ("bk", 512),
                     ("bv", 512), ("cq", 512), ("ck", 512), ("cv", 512), ("gate", 3072)):
    _OFF[_name] = (_o, _o + _size)
    _o += _size
D_IN = _o
N_COLS = 4608
T_COLS = 2304


def _params(sem):
    return pltpu.CompilerParams(dimension_semantics=sem, vmem_limit_bytes=VMEM_LIMIT)


def _sigmoid(x):
    return 1.0 / (1.0 + jnp.exp(-x))


def _dot(a, b, **kw):
    return jnp.dot(a, b, preferred_element_type=F32, **kw)


def _dot_nt(a, b):
    return lax.dot_general(a, b, (((1,), (1,)), ((), ())), preferred_element_type=F32)


def _norm_mod(x, g, shift, scale):
    ms = jnp.mean(x * x, axis=-1, keepdims=True)
    y = x * lax.rsqrt(ms + EPS) * g
    return y * (1.0 + scale) + shift


def _mod_kernel(c_ref, w_ref, b_ref, o_ref):
    c = c_ref[...]
    s = c * _sigmoid(c)
    o_ref[0] = _dot(s, w_ref[0], precision=HIGHEST) + b_ref[0]


def _mod_vectors(c_rows, w_mod, b_mod):
    depth, d, n = w_mod.shape
    tn = 1536
    return pl.pallas_call(
        _mod_kernel,
        grid=(depth, n // tn),
        in_specs=[pl.BlockSpec((8, d), lambda l, j: (0, 0)),
                  pl.BlockSpec((1, d, tn), lambda l, j: (l, 0, j)),
                  pl.BlockSpec((1, 1, tn), lambda l, j: (l, 0, j))],
        out_specs=pl.BlockSpec((1, 8, tn), lambda l, j: (l, 0, j)),
        out_shape=jax.ShapeDtypeStruct((depth, 8, n), F32),
        compiler_params=_params(("arbitrary", "arbitrary")),
        name="mod_vectors",
    )(c_rows, w_mod, b_mod.reshape(depth, 1, n))


def _inproj_kernel(x_ref, g_ref, mod_ref, w_ref, o_ref, h_ref, *, transposed):
    @pl.when(pl.program_id(2) == 0)
    def _():
        for c in range(ROW_TILE // CHUNK):
            rows = slice(c * CHUNK, (c + 1) * CHUNK)
            h = _norm_mod(x_ref[0, rows, :], g_ref[...], mod_ref[0, c, 0:1, :], mod_ref[0, c, 1:2, :])
            h_ref[rows, :] = h.astype(BF16)

    if transposed:
        o_ref[0] = _dot_nt(w_ref[...], h_ref[...]).astype(BF16)
    else:
        o_ref[0] = _dot(h_ref[...], w_ref[...]).astype(BF16)


def _inproj(x_all, g, modc, w, transposed):
    b, n_tok, d = x_all.shape
    nt = n_tok // ROW_TILE
    if transposed:
        cols, tn = w.shape[0], 768
        w_spec = pl.BlockSpec((tn, d), lambda bi, i, j: (j, 0))
        o_spec = pl.BlockSpec((1, tn, ROW_TILE), lambda bi, i, j: (bi, j, i))
        o_shape = (b, cols, n_tok)
    else:
        cols, tn = w.shape[1], 2304
        w_spec = pl.BlockSpec((d, tn), lambda bi, i, j: (0, j))
        o_spec = pl.BlockSpec((1, ROW_TILE, tn), lambda bi, i, j: (bi, i, j))
        o_shape = (b, n_tok, cols)
    return pl.pallas_call(
        functools.partial(_inproj_kernel, transposed=transposed),
        grid=(b, nt, cols // tn),
        in_specs=[pl.BlockSpec((1, ROW_TILE, d), lambda bi, i, j: (bi, i, 0)),
                  pl.BlockSpec((1, d), lambda bi, i, j: (0, 0)),
                  pl.BlockSpec((1, ROW_TILE // CHUNK, 6, d), lambda bi, i, j: (bi, i, 0, 0)),
                  w_spec],
        out_specs=o_spec,
        out_shape=jax.ShapeDtypeStruct(o_shape, BF16),
        scratch_shapes=[pltpu.VMEM((ROW_TILE, d), BF16)],
        compiler_params=_params(("arbitrary", "arbitrary", "arbitrary")),
        name="inproj_t" if transposed else "inproj_n",
    )(x_all, g, modc, w)


def _rope_t(x, c, s):
    x1, x2 = x[:HALF], x[HALF:]
    return jnp.concatenate([x1 * c - x2 * s, x1 * s + x2 * c], axis=0)


def _rmsnorm_t(x, g):
    ms = jnp.mean(x * x, axis=0, keepdims=True)
    return x * lax.rsqrt(ms + EPS) * g


def _kprep_kernel(kb_ref, ka_ref, cos_ref, sin_ref, kn_ref, ob_ref, oa_ref):
    c, s = cos_ref[...], sin_ref[...]
    kn = kn_ref[...]
    outs = []
    for h in range(N_HEADS):
        x = kb_ref[0, HEAD_DIM * h:HEAD_DIM * (h + 1), :].astype(F32)
        outs.append(_rope_t(x, c, s))
    ob_ref[0] = jnp.concatenate(outs, axis=0).T.astype(BF16)
    outs = []
    for h in range(N_HEADS // A_GROUP):
        x = ka_ref[0, HEAD_DIM * h:HEAD_DIM * (h + 1), :].astype(F32)
        outs.append(_rope_t(_rmsnorm_t(x, kn), c, s))
    oa_ref[0] = jnp.concatenate(outs, axis=0).T.astype(BF16)


def _kprep(proj_t, cos_t, sin_t, kn):
    b, _, n_tok = proj_t.shape
    nt = n_tok // ROW_TILE
    return pl.pallas_call(
        _kprep_kernel,
        grid=(b, nt),
        in_specs=[pl.BlockSpec((1, 512, ROW_TILE), lambda bi, i: (bi, 3, i)),
                  pl.BlockSpec((1, 128, ROW_TILE), lambda bi, i: (bi, 17, i)),
                  pl.BlockSpec((HALF, ROW_TILE), lambda bi, i: (0, i)),
                  pl.BlockSpec((HALF, ROW_TILE), lambda bi, i: (0, i)),
                  pl.BlockSpec((HEAD_DIM, 1), lambda bi, i: (0, 0))],
        out_specs=[pl.BlockSpec((1, ROW_TILE, 512), lambda bi, i: (bi, i, 0)),
                   pl.BlockSpec((1, ROW_TILE, 128), lambda bi, i: (bi, i, 0))],
        out_shape=[jax.ShapeDtypeStruct((b, n_tok, 512), BF16),
                   jax.ShapeDtypeStruct((b, n_tok, 128), BF16)],
        compiler_params=_params(("arbitrary", "arbitrary")),
        name="kprep",
    )(proj_t, proj_t, cos_t, sin_t, kn)


def _attn_kernel(*refs, mode, nkv, lam_init, aliased):
    if mode == "A":
        q_ref, k_ref, v_ref, cos_ref, sin_ref, qn_ref = refs[:6]
        rest = refs[6:]
    else:
        q_ref, k_ref, v_ref, cos_ref, sin_ref, lamv_ref, sub_ref = refs[:7]
        rest = refs[7:]
    if aliased:
        rest = rest[1:]
    o_ref, qz_ref, m_ref, l_ref, acc_ref = rest
    kv = pl.program_id(2)

    @pl.when(kv == 0)
    def _init():
        c, s = cos_ref[...], sin_ref[...]
        for h in range(N_HEADS):
            x = q_ref[0, HEAD_DIM * h:HEAD_DIM * (h + 1), :].astype(F32)
            if mode == "A":
                x = _rmsnorm_t(x, qn_ref[...])
            xb = (_rope_t(x, c, s) * SM_SCALE).astype(BF16)
            z = jnp.zeros_like(xb)
            half = (h // A_GROUP) if mode == "A" else (h % 2)
            qz_ref[h] = jnp.concatenate([xb, z] if half == 0 else [z, xb], axis=0)
        m_ref[...] = jnp.full(m_ref.shape, NEG, F32)
        l_ref[...] = jnp.zeros(l_ref.shape, F32)
        acc_ref[...] = jnp.zeros(acc_ref.shape, F32)

    for h in range(N_HEADS):
        if mode == "A":
            g = h // A_GROUP
            kblk = k_ref[0]
            vt = v_ref[0, HEAD_DIM * g:HEAD_DIM * (g + 1), :]
        else:
            kblk = k_ref[0, :, LANES * (h // 2):LANES * (h // 2 + 1)]
            hh = h % B_HEADS
            vt = v_ref[0, B_V_DIM * hh:B_V_DIM * (hh + 1), :]
        s = _dot(kblk, qz_ref[h])
        m_prev = m_ref[h:h + 1, :]
        m_new = jnp.maximum(m_prev, jnp.max(s, axis=0, keepdims=True))
        alpha = jnp.exp(m_prev - m_new)
        p = jnp.exp(s - m_new)
        l_ref[h:h + 1, :] = alpha * l_ref[h:h + 1, :] + jnp.sum(p, axis=0, keepdims=True)
        acc_ref[h] = alpha * acc_ref[h] + _dot(vt, p.astype(BF16))
        m_ref[h:h + 1, :] = m_new

    @pl.when(kv == nkv - 1)
    def _fin():
        outs = []
        if mode == "A":
            for h in range(N_HEADS):
                outs.append(acc_ref[h] * (1.0 / l_ref[h:h + 1, :]))
        else:
            lv = lamv_ref[...]
            lam = (jnp.exp(jnp.sum(lv[0:1] * lv[1:2], axis=1, keepdims=True))
                   - jnp.exp(jnp.sum(lv[2:3] * lv[3:4], axis=1, keepdims=True)) + lam_init)
            for hh in range(B_HEADS):
                o = (acc_ref[hh] * (1.0 / l_ref[hh:hh + 1, :])
                     - lam * (acc_ref[hh + B_HEADS] * (1.0 / l_ref[hh + B_HEADS:hh + B_HEADS + 1, :])))
                outs.append(_rmsnorm_t(o, sub_ref[...]) * (1.0 - lam_init))
        o_ref[0] = jnp.concatenate(outs, axis=0).T.astype(BF16)


def _attention(mode, proj_t, k, cos_t, sin_t, aux, lam_init, n_lat, ctx_into=None):
    b, _, n_tok = proj_t.shape
    is_ctx = ctx_into is not None
    if is_ctx:
        tq = tk = n_tok - n_lat
        nq, nkv = 1, 1
        q_off = k_off = n_lat // tq
    else:
        tq, tk = Q_TILE, KV_TILE
        nq, nkv = n_lat // tq, n_tok // tk
        q_off = k_off = 0
    q_blk = 0 if mode == "A" else 1
    if mode == "A":
        kw, vw, v_blk, dv = 128, 128, 16, HEAD_DIM
    else:
        kw, vw, v_blk, dv = 512, 512, 2, B_V_DIM
    in_specs = [pl.BlockSpec((1, 512, tq), lambda bi, i, j: (bi, q_blk, i + q_off)),
                pl.BlockSpec((1, tk, kw), lambda bi, i, j: (bi, j + k_off, 0)),
                pl.BlockSpec((1, vw, tk), lambda bi, i, j: (bi, v_blk, j + k_off)),
                pl.BlockSpec((HALF, tq), lambda bi, i, j: (0, i + q_off)),
                pl.BlockSpec((HALF, tq), lambda bi, i, j: (0, i + q_off))]
    args = [proj_t, k, proj_t, cos_t, sin_t]
    for a in aux:
        in_specs.append(pl.BlockSpec(a.shape, lambda bi, i, j: (0, 0)))
        args.append(a)
    aliases = {}
    if is_ctx:
        in_specs.append(pl.BlockSpec(memory_space=pl.ANY))
        args.append(ctx_into)
        aliases = {len(args) - 1: 0}
    return pl.pallas_call(
        functools.partial(_attn_kernel, mode=mode, nkv=nkv, lam_init=lam_init, aliased=is_ctx),
        grid=(b, nq, nkv),
        in_specs=in_specs,
        out_specs=pl.BlockSpec((1, tq, 512), lambda bi, i, j: (bi, i + q_off, 0)),
        out_shape=jax.ShapeDtypeStruct((b, n_tok, 512), BF16),
        scratch_shapes=[pltpu.VMEM((N_HEADS, LANES, tq), BF16),
                        pltpu.VMEM((N_HEADS, tq), F32),
                        pltpu.VMEM((N_HEADS, tq), F32),
                        pltpu.VMEM((N_HEADS, dv, tq), F32)],
        input_output_aliases=aliases,
        compiler_params=_params(("arbitrary", "arbitrary", "arbitrary")),
        name=f"attn_{mode.lower()}_{'ctx' if is_ctx else 'lat'}",
    )(*args)


def _nbr_tables(rows):
    nblk = rows // NBR_ROWS
    combos, index = {}, np.zeros((3, NBR_ROWS, BAND_ROWS // 2), np.int32)
    for v, blk in enumerate((0, 1, nblk - 1)):
        bs = min(max(NBR_ROWS * blk - NA_KH // 2, 0), rows - BAND_ROWS)
        for qr in range(NBR_ROWS):
            r = NBR_ROWS * blk + qr
            rs = min(max(r - NA_KH // 2, 0), rows - NA_KH)
            for kp in range(BAND_ROWS // 2):
                codes = []
                for kr in (bs + 2 * kp, bs + 2 * kp + 1):
                    codes.append(kr - r + NA_KH - 1 if rs <= kr < rs + NA_KH else 15)
                index[v, qr, kp] = combos.setdefault(tuple(codes), len(combos))
    lo = np.array([c[0] for c in combos], np.int32)
    hi = np.array([c[1] for c in combos], np.int32)
    return index.reshape(-1), lo, hi


def _nbr_bias_table(rpb, lo, hi):
    nh = rpb.shape[0]
    qc = np.arange(GRID_W)[:, None]
    kc = np.arange(GRID_W)[None, :]
    cs = np.clip(qc - NA_KW // 2, 0, GRID_W - NA_KW)
    inside = (kc >= cs) & (kc < cs + NA_KW)
    dc = np.clip(kc - qc + NA_KW - 1, 0, 2 * NA_KW - 2)
    onehot = (dc.reshape(-1)[None, :] == np.arange(2 * NA_KW - 1)[:, None]).astype(np.float32)
    toep = jnp.einsum("hrc,cq->hrq", rpb, jnp.asarray(onehot), precision=HIGHEST)
    toep = toep.reshape(nh, 2 * NA_KH - 1, GRID_W, GRID_W)
    toep = jnp.where(jnp.asarray(inside)[None, None], toep, NEG)
    toep = jnp.concatenate([toep, jnp.full((nh, 1, GRID_W, GRID_W), NEG, F32)], axis=1)
    return jnp.concatenate([toep[:, lo], toep[:, hi]], axis=-1)


def _pair_heads(q, lane):
    zero = jnp.zeros_like(q)
    return jnp.where(lane < HEAD_DIM, q, zero), jnp.where(lane >= HEAD_DIM, q, zero)


def _nbr_kernel(idx_ref, q_ref, k0, k1, k2, k3, v0, v1, v2, v3, kc_ref, vc_ref, tp_ref,
                o_ref, s_ref, *, nblk):
    blk = pl.program_id(1)
    variant = jnp.where(blk == 0, 0, jnp.where(blk == nblk - 1, 2, 1))
    lane = lax.broadcasted_iota(jnp.int32, (1, LANES), 1)
    outs = []
    for p in range(N_HEADS // 2):
        cols = slice(LANES * p, LANES * (p + 1))
        kband = jnp.concatenate([k0[0, :, cols], k1[0, :, cols], k2[0, :, cols], k3[0, :, cols]], axis=0)
        vband = jnp.concatenate([v0[0, :, cols], v1[0, :, cols], v2[0, :, cols], v3[0, :, cols]], axis=0)
        kctx, vctx = kc_ref[0, :, cols], vc_ref[0, :, cols]
        qp = q_ref[0, :, cols] * jnp.asarray(SM_SCALE, BF16)
        res = []
        for e, qm in enumerate(_pair_heads(qp, lane)):
            h = 2 * p + e
            s_raw = _dot_nt(qm, kband)
            sc = _dot_nt(qm, kctx)
            for qr in range(NBR_ROWS):
                rws = slice(GRID_W * qr, GRID_W * (qr + 1))
                for kp in range(BAND_ROWS // 2):
                    u = idx_ref[variant * (NBR_ROWS * BAND_ROWS // 2) + qr * (BAND_ROWS // 2) + kp]
                    cl = slice(LANES * kp, LANES * (kp + 1))
                    s_ref[rws, cl] = s_raw[rws, cl] + tp_ref[h, u]
            s = s_ref[...]
            m = jnp.maximum(jnp.max(s, axis=-1, keepdims=True), jnp.max(sc, axis=-1, keepdims=True))
            pn = jnp.exp(s - m)
            pc = jnp.exp(sc - m)
            l = jnp.sum(pn, axis=-1, keepdims=True) + jnp.sum(pc, axis=-1, keepdims=True)
            o = _dot(pn.astype(BF16), vband) + _dot(pc.astype(BF16), vctx)
            res.append(o * (1.0 / l))
        outs.append(jnp.where(lane < HEAD_DIM, res[0], res[1]))
    o_ref[0] = jnp.concatenate(outs, axis=1).astype(BF16)


def _nbr_attention(proj_n, idx, tp, n_lat):
    b, n_tok, _ = proj_n.shape
    rows = n_lat // GRID_W
    nblk = rows // NBR_ROWS
    tq = NBR_ROWS * GRID_W
    blk_rows = CHUNK // GRID_W
    n_band = BAND_ROWS // blk_rows
    max_start = rows // blk_rows - n_band
    ctx_blk = n_lat // CHUNK
    q_col, k_col, v_col = 6, 7, 8

    def band_spec(j, col):
        def imap(bi, i, idx_ref):
            start = jnp.clip(2 * i - 1, 0, max_start)
            return (bi, start + j, col)
        return pl.BlockSpec((1, CHUNK, 512), imap)

    in_specs = [pl.BlockSpec((1, tq, 512), lambda bi, i, idx_ref: (bi, i, q_col))]
    in_specs += [band_spec(j, k_col) for j in range(n_band)]
    in_specs += [band_spec(j, v_col) for j in range(n_band)]
    in_specs += [pl.BlockSpec((1, CHUNK, 512), lambda bi, i, idx_ref: (bi, ctx_blk, k_col)),
                 pl.BlockSpec((1, CHUNK, 512), lambda bi, i, idx_ref: (bi, ctx_blk, v_col)),
                 pl.BlockSpec(tp.shape, lambda bi, i, idx_ref: (0, 0, 0, 0))]
    grid_spec = pltpu.PrefetchScalarGridSpec(
        num_scalar_prefetch=1, grid=(b, nblk), in_specs=in_specs,
        out_specs=pl.BlockSpec((1, tq, 512), lambda bi, i, idx_ref: (bi, i, 0)),
        scratch_shapes=[pltpu.VMEM((tq, BAND_ROWS * GRID_W), F32)])
    return pl.pallas_call(
        functools.partial(_nbr_kernel, nblk=nblk),
        grid_spec=grid_spec,
        out_shape=jax.ShapeDtypeStruct((b, n_tok, 512), BF16),
        compiler_params=_params(("arbitrary", "arbitrary")),
        name="nbr_attn",
    )(idx, *([proj_n] * (1 + 2 * n_band + 2)), tp)


def _ctx_c_kernel(q_ref, k_ref, v_ref, prev_ref, o_ref):
    del prev_ref
    lane = lax.broadcasted_iota(jnp.int32, (1, LANES), 1)
    outs = []
    for p in range(N_HEADS // 2):
        cols = slice(LANES * p, LANES * (p + 1))
        kp_, vp = k_ref[0, :, cols], v_ref[0, :, cols]
        qp = q_ref[0, :, cols] * jnp.asarray(SM_SCALE, BF16)
        res = []
        for qm in _pair_heads(qp, lane):
            s = _dot_nt(qm, kp_)
            m = jnp.max(s, axis=-1, keepdims=True)
            pr = jnp.exp(s - m)
            l = jnp.sum(pr, axis=-1, keepdims=True)
            res.append(_dot(pr.astype(BF16), vp) * (1.0 / l))
        outs.append(jnp.where(lane < HEAD_DIM, res[0], res[1]))
    o_ref[0] = jnp.concatenate(outs, axis=1).astype(BF16)


def _ctx_c_attention(proj_n, o_c, n_lat):
    b, n_tok, _ = proj_n.shape
    n_ctx = n_tok - n_lat
    blk = n_lat // n_ctx
    return pl.pallas_call(
        _ctx_c_kernel,
        grid=(b,),
        in_specs=[pl.BlockSpec((1, n_ctx, 512), lambda bi: (bi, blk, 6)),
                  pl.BlockSpec((1, n_ctx, 512), lambda bi: (bi, blk, 7)),
                  pl.BlockSpec((1, n_ctx, 512), lambda bi: (bi, blk, 8)),
                  pl.BlockSpec(memory_space=pl.ANY)],
        out_specs=pl.BlockSpec((1, n_ctx, 512), lambda bi: (bi, blk, 0)),
        out_shape=jax.ShapeDtypeStruct(o_c.shape, BF16),
        input_output_aliases={3: 0},
        compiler_params=_params(("arbitrary",)),
        name="ctx_c_attn",
    )(proj_n, proj_n, proj_n, o_c)


def _merge_kernel(*refs, with_router):
    (oa_ref, ob_ref, oc_ref, gate_ref, x_ref, wa_ref, wb_ref, wc_ref, wo_ref,
     mod_ref, g_ref) = refs[:11]
    if with_router:
        r_ref, xo_ref, h_ref, comb_ref = refs[11:]
    else:
        xo_ref, h_ref = refs[11:]
    d = D_MODEL
    gate = lambda k: _sigmoid(gate_ref[0, :, k * d:(k + 1) * d].astype(F32))
    m = (gate(0) * _dot(oa_ref[0], wa_ref[...])
         + gate(1) * _dot(ob_ref[0], wb_ref[...])
         + gate(2) * _dot(oc_ref[0], wc_ref[...]))
    y = _dot(m.astype(BF16), wo_ref[...])
    for c in range(ROW_TILE // CHUNK):
        rows = slice(c * CHUNK, (c + 1) * CHUNK)
        xn = x_ref[0, rows, :] + mod_ref[0, c, 2:3, :] * y[rows, :]
        xo_ref[0, rows, :] = xn
        h = _norm_mod(xn, g_ref[...], mod_ref[0, c, 3:4, :], mod_ref[0, c, 4:5, :])
        h_ref[0, rows, :] = h.astype(BF16)
        if with_router:
            lane = lax.broadcasted_iota(jnp.int32, (CHUNK, LANES), 1)
            logits = jnp.where(lane < N_EXPERTS, _dot(h, r_ref[...], precision=HIGHEST), NEG)
            m1 = jnp.max(logits, axis=-1, keepdims=True)
            i1 = jnp.min(jnp.where(logits == m1, lane, LANES), axis=-1, keepdims=True)
            rest = jnp.where(lane == i1, NEG, logits)
            m2 = jnp.max(rest, axis=-1, keepdims=True)
            i2 = jnp.min(jnp.where(rest == m2, lane, LANES), axis=-1, keepdims=True)
            e = jnp.exp(m2 - m1)
            w1 = 1.0 / (1.0 + e)
            comb_ref[0, rows, :] = (jnp.where(lane == i1, w1, 0.0)
                                    + jnp.where(lane == i2, e * w1, 0.0))


def _merge(o_a, o_b, o_c, proj_n, x_all, wa, wb, wc, wo, modc, g_ffn, router):
    b, n_tok, d = x_all.shape
    nt = n_tok // ROW_TILE
    with_router = router is not None
    tile = lambda w: pl.BlockSpec((1, ROW_TILE, w), lambda bi, i: (bi, i, 0))
    full = lambda a: pl.BlockSpec(a.shape, lambda bi, i: (0,) * a.ndim)
    in_specs = [tile(512), tile(512), tile(512), tile(3 * d), tile(d),
                full(wa), full(wb), full(wc), full(wo),
                pl.BlockSpec((1, ROW_TILE // CHUNK, 6, d), lambda bi, i: (bi, i, 0, 0)),
                full(g_ffn)]
    args = [o_a, o_b, o_c, proj_n, x_all, wa, wb, wc, wo, modc, g_ffn]
    out_specs = [tile(d), tile(d)]
    out_shape = [jax.ShapeDtypeStruct((b, n_tok, d), F32), jax.ShapeDtypeStruct((b, n_tok, d), BF16)]
    if with_router:
        in_specs.append(full(router))
        args.append(router)
        out_specs.append(tile(LANES))
        out_shape.append(jax.ShapeDtypeStruct((b, n_tok, LANES), F32))
    return pl.pallas_call(
        functools.partial(_merge_kernel, with_router=with_router),
        grid=(b, nt),
        in_specs=in_specs,
        out_specs=out_specs,
        out_shape=out_shape,
        compiler_params=_params(("arbitrary", "arbitrary")),
        name="merge_router" if with_router else "merge",
    )(*args)


def _ffn_kernel(*refs, n_f, moe):
    if moe:
        h_ref, x_ref, w1_ref, w3_ref, w2_ref, mod_ref, comb_ref, o_ref, acc_ref = refs
        first = (pl.program_id(2) == 0) & (pl.program_id(3) == 0)
        last = (pl.program_id(2) == N_EXPERTS - 1) & (pl.program_id(3) == n_f - 1)
        w1, w3, w2 = w1_ref[0, 0], w3_ref[0, 0], w2_ref[0, 0]
    else:
        h_ref, x_ref, w1_ref, w3_ref, w2_ref, mod_ref, o_ref, acc_ref = refs
        first = pl.program_id(2) == 0
        last = pl.program_id(2) == n_f - 1
        w1, w3, w2 = w1_ref[0], w3_ref[0], w2_ref[0]

    @pl.when(first)
    def _():
        acc_ref[...] = jnp.zeros(acc_ref.shape, F32)

    h = h_ref[0]
    a = _dot(h, w1)
    t = (a * _sigmoid(a)) * _dot(h, w3)
    y = _dot(t.astype(BF16), w2)
    if moe:
        lane = lax.broadcasted_iota(jnp.int32, (1, LANES), 1)
        ce = jnp.sum(jnp.where(lane == pl.program_id(2), comb_ref[0], 0.0), axis=-1, keepdims=True)
        y = ce * y
    acc_ref[...] += y

    @pl.when(last)
    def _():
        for c in range(ROW_TILE // CHUNK):
            rows = slice(c * CHUNK, (c + 1) * CHUNK)
            o_ref[0, rows, :] = x_ref[0, rows, :] + mod_ref[0, c, 5:6, :] * acc_ref[rows, :]


def _ff_tile(f, cap):
    return max(t for t in range(LANES, min(f, cap) + 1, LANES) if f % t == 0)


def _ffn(h2, x_all, w1, w3, w2, li, modc, comb=None):
    b, n_tok, d = x_all.shape
    nt = n_tok // ROW_TILE
    moe = comb is not None
    f = w1.shape[-1]
    tf = _ff_tile(f, 896 if moe else 1408)
    n_f = f // tf
    if moe:
        grid = (b, nt, N_EXPERTS, n_f)
        tile = lambda w: pl.BlockSpec((1, ROW_TILE, w), lambda bi, i, e, j: (bi, i, 0))
        w13 = pl.BlockSpec((1, 1, d, tf), lambda bi, i, e, j: (li, e, 0, j))
        w2s = pl.BlockSpec((1, 1, tf, d), lambda bi, i, e, j: (li, e, j, 0))
        mods = pl.BlockSpec((1, ROW_TILE // CHUNK, 6, d), lambda bi, i, e, j: (bi, i, 0, 0))
        sem = ("arbitrary",) * 4
    else:
        grid = (b, nt, n_f)
        tile = lambda w: pl.BlockSpec((1, ROW_TILE, w), lambda bi, i, j: (bi, i, 0))
        w13 = pl.BlockSpec((1, d, tf), lambda bi, i, j: (li, 0, j))
        w2s = pl.BlockSpec((1, tf, d), lambda bi, i, j: (li, j, 0))
        mods = pl.BlockSpec((1, ROW_TILE // CHUNK, 6, d), lambda bi, i, j: (bi, i, 0, 0))
        sem = ("arbitrary",) * 3
    in_specs = [tile(d), tile(d), w13, w13, w2s, mods]
    args = [h2, x_all, w1, w3, w2, modc]
    if moe:
        in_specs.append(tile(LANES))
        args.append(comb)
    return pl.pallas_call(
        functools.partial(_ffn_kernel, n_f=n_f, moe=moe),
        grid=grid,
        in_specs=in_specs,
        out_specs=tile(d),
        out_shape=jax.ShapeDtypeStruct((b, n_tok, d), F32),
        scratch_shapes=[pltpu.VMEM((ROW_TILE, d), F32)],
        compiler_params=_params(sem),
        name="moe_ffn" if moe else "ffn",
    )(*args)


def _final_kernel(x_ref, g_ref, o_ref):
    x = x_ref[0]
    ms = jnp.mean(x * x, axis=-1, keepdims=True)
    o_ref[0] = x * lax.rsqrt(ms + EPS) * g_ref[...]


def _final_norm(x_all, g, n_lat):
    b, _, d = x_all.shape
    tm = 1024
    return pl.pallas_call(
        _final_kernel,
        grid=(b, n_lat // tm),
        in_specs=[pl.BlockSpec((1, tm, d), lambda bi, i: (bi, i, 0)),
                  pl.BlockSpec((1, d), lambda bi, i: (0, 0))],
        out_specs=pl.BlockSpec((1, tm, d), lambda bi, i: (bi, i, 0)),
        out_shape=jax.ShapeDtypeStruct((b, n_lat, d), F32),
        compiler_params=_params(("arbitrary", "arbitrary")),
        name="final_norm",
    )(x_all, g)


def _deinterleave(w, n_heads):
    lead = w.shape[:-1]
    w = w.reshape(lead + (n_heads, HALF, 2))
    return jnp.swapaxes(w, -1, -2).reshape(lead + (n_heads * HEAD_DIM,))


def _rope_tables(n_lat, n_tok):
    t = np.arange(n_lat)
    inv = ROPE_THETA ** (-np.arange(HEAD_DIM // 4, dtype=np.float64) / (HEAD_DIM // 4))
    ang = np.concatenate([(t // GRID_W)[:, None] * inv, (t % GRID_W)[:, None] * inv], axis=-1)
    ang = np.concatenate([ang, np.zeros((n_tok - n_lat, HALF))], axis=0)
    return jnp.asarray(np.cos(ang).T, F32), jnp.asarray(np.sin(ang).T, F32)


def kernel(x, c, ctx, c_ctx, w_mod, b_mod, g_mix, g_ffn, w_in, qn_a, kn_a, lam_q1, lam_k1, lam_q2,
           lam_k2, subln_b, rpb_c, w_br_a, w_br_b, w_br_c, w_out, ffn_w1, ffn_w3, ffn_w2, router,
           moe_w1, moe_w3, moe_w2, g_final):
    b, n_lat, d = x.shape
    n_ctx = ctx.shape[1]
    n_tok = n_lat + n_ctx
    depth = w_in.shape[0]
    assert d == D_MODEL and n_ctx == CHUNK and b + 1 <= 8
    assert n_tok % ROW_TILE == 0 and n_tok % KV_TILE == 0 and n_lat % Q_TILE == 0
    assert n_lat % (NBR_ROWS * GRID_W) == 0 and n_lat // GRID_W >= 24
    n_chunks = n_tok // CHUNK

    x_all = jnp.concatenate([x, ctx], axis=1)
    cos_t, sin_t = _rope_tables(n_lat, n_tok)

    c_rows = jnp.concatenate([c, c_ctx[None], jnp.zeros((7 - b, d), F32)], axis=0)
    mods = _mod_vectors(c_rows, w_mod, b_mod).reshape(depth, 8, 6, d)
    mod_lat = jnp.broadcast_to(mods[:, :b, None], (depth, b, n_chunks - 1, 6, d))
    mod_ctx = jnp.broadcast_to(mods[:, b:b + 1, None], (depth, b, 1, 6, d))
    mod_chunks = jnp.concatenate([mod_lat, mod_ctx], axis=2)

    sl = lambda name: w_in[:, :, _OFF[name][0]:_OFF[name][1]]
    w_n = jnp.concatenate([sl("gate"), sl("cq"), sl("ck"), sl("cv")], axis=-1).astype(BF16)
    w_t = jnp.concatenate([_deinterleave(sl("aq"), 8), _deinterleave(sl("bq"), 8), sl("bv"),
                           _deinterleave(sl("bk"), 8), sl("av"), _deinterleave(sl("ak"), 2)], axis=-1)
    w_t = jnp.swapaxes(w_t, 1, 2).astype(BF16)
    qn_t = _deinterleave(qn_a, 1)[:, :, None]
    kn_t = _deinterleave(kn_a, 1)[:, :, None]
    lam_vecs = jnp.pad(jnp.stack([lam_q1, lam_k1, lam_q2, lam_k2], axis=1),
                       ((0, 0), (0, 4), (0, LANES - HEAD_DIM)))
    wa, wb, wc, wo = (w.astype(BF16) for w in (w_br_a, w_br_b, w_br_c, w_out))
    f1, f3, f2 = (w.astype(BF16) for w in (ffn_w1, ffn_w3, ffn_w2))
    m1, m3, m2 = (w.astype(BF16) for w in (moe_w1, moe_w3, moe_w2))
    router_p = jnp.pad(router, ((0, 0), (0, 0), (0, LANES - N_EXPERTS)))
    nbr_idx, nbr_lo, nbr_hi = _nbr_tables(n_lat // GRID_W)
    nbr_idx = jnp.asarray(nbr_idx)

    for layer in range(depth):
        lam_init = 0.8 - 0.6 * math.exp(-0.3 * layer)
        modc = mod_chunks[layer]
        g_mix_l = g_mix[layer][None]
        proj_n = _inproj(x_all, g_mix_l, modc, w_n[layer], transposed=False)
        proj_t = _inproj(x_all, g_mix_l, modc, w_t[layer], transposed=True)
        k_b, k_a = _kprep(proj_t, cos_t, sin_t, kn_t[layer])

        o_a = _attention("A", proj_t, k_a, cos_t, sin_t, [qn_t[layer]], lam_init, n_lat)
        o_a = _attention("A", proj_t, k_a, cos_t, sin_t, [qn_t[layer]], lam_init, n_lat, ctx_into=o_a)
        aux_b = [lam_vecs[layer], subln_b[layer][:, None]]
        o_b = _attention("B", proj_t, k_b, cos_t, sin_t, aux_b, lam_init, n_lat)
        o_b = _attention("B", proj_t, k_b, cos_t, sin_t, aux_b, lam_init, n_lat, ctx_into=o_b)
        tp = _nbr_bias_table(rpb_c[layer], nbr_lo, nbr_hi)
        o_c = _nbr_attention(proj_n, nbr_idx, tp, n_lat)
        o_c = _ctx_c_attention(proj_n, o_c, n_lat)

        is_moe = layer % 2 == 1
        li = layer // 2
        merged = _merge(o_a, o_b, o_c, proj_n, x_all, wa[layer], wb[layer], wc[layer], wo[layer],
                        modc, g_ffn[layer][None], router_p[li] if is_moe else None)
        if is_moe:
            x_mid, h2, comb = merged
            x_all = _ffn(h2, x_mid, m1, m3, m2, li, modc, comb=comb)
        else:
            x_mid, h2 = merged
            x_all = _ffn(h2, x_mid, f1, f3, f2, li, modc)

    return _final_norm(x_all, g_final[None], n_lat)
```

```python
import functools
import math

import numpy as np
import jax
import jax.numpy as jnp
from jax import lax
from jax.experimental import pallas as pl
from jax.experimental.pallas import tpu as pltpu

F32 = jnp.float32
BF16 = jnp.bfloat16
HIGHEST = lax.Precision.HIGHEST

D_MODEL = 1024
HEAD_DIM = 64
HALF = HEAD_DIM // 2
GRID_W = 64
ROPE_THETA = 10000.0
EPS = 1e-6
N_HEADS = 8
A_GROUP = 4
B_HEADS = 4
B_V_DIM = 128
NA_KH = 8
NA_KW = 16
N_EXPERTS = 8
LANES = 128
CHUNK = 256
ROW_TILE = 768
Q_TILE = 512
KV_TILE = 768
MOE_TILE = 512
NBR_ROWS = 8
BAND_ROWS = 16
NEG = -1e30
SM_SCALE = HEAD_DIM ** -0.5
LOG2E = math.log2(math.e)
SUM_ROWS = 16
STAB_MARGIN = 64.0
VMEM_LIMIT = 56 * 1024 * 1024

_OFF = {}
_o = 0
for _name, _size in (("aq", 512), ("ak", 128), ("av", 128), ("bq", 512), ("bk", 512),
                     ("bv", 512), ("cq", 512), ("ck", 512), ("cv", 512), ("gate", 3072)):
    _OFF[_name] = (_o, _o + _size)
    _o += _size
D_IN = _o
N_COLS = 4608
T_COLS = 2304


def _params(sem):
    return pltpu.CompilerParams(dimension_semantics=sem, vmem_limit_bytes=VMEM_LIMIT)


def _sigmoid(x):
    return 1.0 / (1.0 + jnp.exp(-x))


def _dot(a, b, **kw):
    return jnp.dot(a, b, preferred_element_type=F32, **kw)


def _dot_nt(a, b):
    return lax.dot_general(a, b, (((1,), (1,)), ((), ())), preferred_element_type=F32)


def _norm_mod(x, g, shift, scale):
    ms = jnp.mean(x * x, axis=-1, keepdims=True)
    y = x * lax.rsqrt(ms + EPS) * g
    return y * (1.0 + scale) + shift


def _mod_kernel(c_ref, w_ref, b_ref, o_ref):
    c = c_ref[...]
    s = c * _sigmoid(c)
    o_ref[0] = _dot(s, w_ref[0], precision=HIGHEST) + b_ref[0]


def _mod_vectors(c_rows, w_mod, b_mod):
    depth, d, n = w_mod.shape
    tn = 1536
    return pl.pallas_call(
        _mod_kernel,
        grid=(depth, n // tn),
        in_specs=[pl.BlockSpec((8, d), lambda l, j: (0, 0)),
                  pl.BlockSpec((1, d, tn), lambda l, j: (l, 0, j)),
                  pl.BlockSpec((1, 1, tn), lambda l, j: (l, 0, j))],
        out_specs=pl.BlockSpec((1, 8, tn), lambda l, j: (l, 0, j)),
        out_shape=jax.ShapeDtypeStruct((depth, 8, n), F32),
        compiler_params=_params(("arbitrary", "arbitrary")),
        name="mod_vectors",
    )(c_rows, w_mod, b_mod.reshape(depth, 1, n))


def _inproj_kernel(x_ref, g_ref, mod_ref, w_ref, o_ref, h_ref, *, transposed):
    @pl.when(pl.program_id(2) == 0)
    def _():
        for c in range(ROW_TILE // CHUNK):
            rows = slice(c * CHUNK, (c + 1) * CHUNK)
            h = _norm_mod(x_ref[0, rows, :], g_ref[...], mod_ref[0, c, 0:1, :], mod_ref[0, c, 1:2, :])
            h_ref[rows, :] = h.astype(BF16)

    if transposed:
        o_ref[0] = _dot_nt(w_ref[...], h_ref[...]).astype(BF16)
    else:
        o_ref[0] = _dot(h_ref[...], w_ref[...]).astype(BF16)


def _inproj(x_all, g, modc, w, transposed):
    b, n_tok, d = x_all.shape
    nt = n_tok // ROW_TILE
    if transposed:
        cols, tn = w.shape[0], 768
        w_spec = pl.BlockSpec((tn, d), lambda bi, i, j: (j, 0))
        o_spec = pl.BlockSpec((1, tn, ROW_TILE), lambda bi, i, j: (bi, j, i))
        o_shape = (b, cols, n_tok)
    else:
        cols, tn = w.shape[1], 2304
        w_spec = pl.BlockSpec((d, tn), lambda bi, i, j: (0, j))
        o_spec = pl.BlockSpec((1, ROW_TILE, tn), lambda bi, i, j: (bi, i, j))
        o_shape = (b, n_tok, cols)
    return pl.pallas_call(
        functools.partial(_inproj_kernel, transposed=transposed),
        grid=(b, nt, cols // tn),
        in_specs=[pl.BlockSpec((1, ROW_TILE, d), lambda bi, i, j: (bi, i, 0)),
                  pl.BlockSpec((1, d), lambda bi, i, j: (0, 0)),
                  pl.BlockSpec((1, ROW_TILE // CHUNK, 6, d), lambda bi, i, j: (bi, i, 0, 0)),
                  w_spec],
        out_specs=o_spec,
        out_shape=jax.ShapeDtypeStruct(o_shape, BF16),
        scratch_shapes=[pltpu.VMEM((ROW_TILE, d), BF16)],
        compiler_params=_params(("arbitrary", "arbitrary", "arbitrary")),
        name="inproj_t" if transposed else "inproj_n",
    )(x_all, g, modc, w)


def _rope_t(x, c, s):
    x1, x2 = x[:HALF], x[HALF:]
    return jnp.concatenate([x1 * c - x2 * s, x1 * s + x2 * c], axis=0)


def _rmsnorm_t(x, g):
    ms = jnp.mean(x * x, axis=0, keepdims=True)
    return x * lax.rsqrt(ms + EPS) * g


def _kprep_kernel(kb_ref, ka_ref, cos_ref, sin_ref, kn_ref, ob_ref, oa_ref):
    c, s = cos_ref[...], sin_ref[...]
    kn = kn_ref[...]
    outs = []
    for h in range(N_HEADS):
        x = kb_ref[0, HEAD_DIM * h:HEAD_DIM * (h + 1), :].astype(F32)
        outs.append(_rope_t(x, c, s))
    ob_ref[0] = jnp.concatenate(outs, axis=0).T.astype(BF16)
    outs = []
    for h in range(N_HEADS // A_GROUP):
        x = ka_ref[0, HEAD_DIM * h:HEAD_DIM * (h + 1), :].astype(F32)
        outs.append(_rope_t(_rmsnorm_t(x, kn), c, s))
    oa_ref[0] = jnp.concatenate(outs, axis=0).T.astype(BF16)


def _kprep(proj_t, cos_t, sin_t, kn):
    b, _, n_tok = proj_t.shape
    nt = n_tok // ROW_TILE
    return pl.pallas_call(
        _kprep_kernel,
        grid=(b, nt),
        in_specs=[pl.BlockSpec((1, 512, ROW_TILE), lambda bi, i: (bi, 3, i)),
                  pl.BlockSpec((1, 128, ROW_TILE), lambda bi, i: (bi, 17, i)),
                  pl.BlockSpec((HALF, ROW_TILE), lambda bi, i: (0, i)),
                  pl.BlockSpec((HALF, ROW_TILE), lambda bi, i: (0, i)),
                  pl.BlockSpec((HEAD_DIM, 1), lambda bi, i: (0, 0))],
        out_specs=[pl.BlockSpec((1, ROW_TILE, 512), lambda bi, i: (bi, i, 0)),
                   pl.BlockSpec((1, ROW_TILE, 128), lambda bi, i: (bi, i, 0))],
        out_shape=[jax.ShapeDtypeStruct((b, n_tok, 512), BF16),
                   jax.ShapeDtypeStruct((b, n_tok, 128), BF16)],
        compiler_params=_params(("arbitrary", "arbitrary")),
        name="kprep",
    )(proj_t, proj_t, cos_t, sin_t, kn)


def _attn_kernel(*refs, mode, nkv, lam_init, aliased):
    if mode == "A":
        q_ref, k_ref, v_ref, cos_ref, sin_ref, qn_ref = refs[:6]
        rest = refs[6:]
    else:
        q_ref, k_ref, v_ref, cos_ref, sin_ref, lamv_ref, sub_ref = refs[:7]
        rest = refs[7:]
    if aliased:
        rest = rest[1:]
    o_ref, qz_ref, m_ref, acc_ref, redo_ref = rest
    kv = pl.program_id(2)
    dv = HEAD_DIM if mode == "A" else B_V_DIM

    @pl.when(kv == 0)
    def _init():
        c, s = cos_ref[...], sin_ref[...]
        for h in range(N_HEADS):
            x = q_ref[0, HEAD_DIM * h:HEAD_DIM * (h + 1), :].astype(F32)
            if mode == "A":
                x = _rmsnorm_t(x, qn_ref[...])
            xb = (_rope_t(x, c, s) * (SM_SCALE * LOG2E)).astype(BF16)
            z = jnp.zeros_like(xb)
            half = (h // A_GROUP) if mode == "A" else (h % 2)
            qz_ref[h] = jnp.concatenate([xb, z] if half == 0 else [z, xb], axis=0)
        m_ref[...] = jnp.full(m_ref.shape, NEG, F32)
        acc_ref[0] = jnp.zeros(acc_ref.shape[1:], F32)
        redo_ref[0] = 1

    acc_in = acc_ref.at[kv % 2]
    acc_out = acc_ref.at[(kv + 1) % 2]
    ones = jnp.ones((SUM_ROWS, k_ref.shape[1]), BF16)

    def scores(h):
        if mode == "A":
            kblk = k_ref[0]
        else:
            kblk = k_ref[0, :, LANES * (h // 2):LANES * (h // 2 + 1)]
        return _dot(kblk, qz_ref[h])

    def values(h):
        if mode == "A":
            g = h // A_GROUP
            vt = v_ref[0, HEAD_DIM * g:HEAD_DIM * (g + 1), :]
        else:
            hh = h % B_HEADS
            vt = v_ref[0, B_V_DIM * hh:B_V_DIM * (hh + 1), :]
        return jnp.concatenate([vt, ones], axis=0)

    @pl.when(kv > 0)
    def _single_pass():
        excess = None
        for h in range(N_HEADS):
            s = scores(h)
            m = m_ref[h:h + 1, :]
            d = jnp.max(s, axis=0, keepdims=True) - m
            excess = d if excess is None else jnp.maximum(excess, d)
            p = jnp.exp2((s - m).astype(BF16))
            acc_out[h] = acc_in[h] + _dot(values(h), p)
        redo_ref[0] = (jnp.max(excess) > STAB_MARGIN).astype(jnp.int32)

    @pl.when(redo_ref[0] != 0)
    def _exact_pass():
        for h in range(N_HEADS):
            s = scores(h)
            m_prev = m_ref[h:h + 1, :]
            m_new = jnp.maximum(m_prev, jnp.max(s, axis=0, keepdims=True))
            alpha = jnp.exp2(m_prev - m_new)
            p = jnp.exp2((s - m_new).astype(BF16))
            acc_out[h] = alpha * acc_in[h] + _dot(values(h), p)
            m_ref[h:h + 1, :] = m_new

    @pl.when(kv == nkv - 1)
    def _fin():
        acc = acc_ref.at[nkv % 2]

        def normalized(h):
            return acc[h, 0:dv, :] * (1.0 / acc[h, dv:dv + 1, :])

        outs = []
        if mode == "A":
            for h in range(N_HEADS):
                outs.append(normalized(h))
        else:
            lv = lamv_ref[...]
            lam = (jnp.exp(jnp.sum(lv[0:1] * lv[1:2], axis=1, keepdims=True))
                   - jnp.exp(jnp.sum(lv[2:3] * lv[3:4], axis=1, keepdims=True)) + lam_init)
            for hh in range(B_HEADS):
                o = normalized(hh) - lam * normalized(hh + B_HEADS)
                outs.append(_rmsnorm_t(o, sub_ref[...]) * (1.0 - lam_init))
        o_ref[0] = jnp.concatenate(outs, axis=0).T.astype(BF16)


def _attention(mode, proj_t, k, cos_t, sin_t, aux, lam_init, n_lat, ctx_into=None):
    b, _, n_tok = proj_t.shape
    is_ctx = ctx_into is not None
    if is_ctx:
        tq = tk = n_tok - n_lat
        nq, nkv = 1, 1
        q_off = k_off = n_lat // tq
    else:
        tq, tk = Q_TILE, KV_TILE
        nq, nkv = n_lat // tq, n_tok // tk
        q_off = k_off = 0
    q_blk = 0 if mode == "A" else 1
    if mode == "A":
        kw, vw, v_blk, dv = 128, 128, 16, HEAD_DIM
    else:
        kw, vw, v_blk, dv = 512, 512, 2, B_V_DIM
    in_specs = [pl.BlockSpec((1, 512, tq), lambda bi, i, j: (bi, q_blk, i + q_off)),
                pl.BlockSpec((1, tk, kw), lambda bi, i, j: (bi, j + k_off, 0)),
                pl.BlockSpec((1, vw, tk), lambda bi, i, j: (bi, v_blk, j + k_off)),
                pl.BlockSpec((HALF, tq), lambda bi, i, j: (0, i + q_off)),
                pl.BlockSpec((HALF, tq), lambda bi, i, j: (0, i + q_off))]
    args = [proj_t, k, proj_t, cos_t, sin_t]
    for a in aux:
        in_specs.append(pl.BlockSpec(a.shape, lambda bi, i, j: (0, 0)))
        args.append(a)
    aliases = {}
    if is_ctx:
        in_specs.append(pl.BlockSpec(memory_space=pl.ANY))
        args.append(ctx_into)
        aliases = {len(args) - 1: 0}
    return pl.pallas_call(
        functools.partial(_attn_kernel, mode=mode, nkv=nkv, lam_init=lam_init, aliased=is_ctx),
        grid=(b, nq, nkv),
        in_specs=in_specs,
        out_specs=pl.BlockSpec((1, tq, 512), lambda bi, i, j: (bi, i + q_off, 0)),
        out_shape=jax.ShapeDtypeStruct((b, n_tok, 512), BF16),
        scratch_shapes=[pltpu.VMEM((N_HEADS, LANES, tq), BF16),
                        pltpu.VMEM((N_HEADS, tq), F32),
                        pltpu.VMEM((2, N_HEADS, dv + SUM_ROWS, tq), F32),
                        pltpu.SMEM((1,), jnp.int32)],
        input_output_aliases=aliases,
        compiler_params=_params(("arbitrary", "arbitrary", "arbitrary")),
        name=f"attn_{mode.lower()}_{'ctx' if is_ctx else 'lat'}",
    )(*args)


def _nbr_tables(rows):
    nblk = rows // NBR_ROWS
    combos, index = {}, np.zeros((3, NBR_ROWS, BAND_ROWS // 2), np.int32)
    for v, blk in enumerate((0, 1, nblk - 1)):
        bs = min(max(NBR_ROWS * blk - NA_KH // 2, 0), rows - BAND_ROWS)
        for qr in range(NBR_ROWS):
            r = NBR_ROWS * blk + qr
            rs = min(max(r - NA_KH // 2, 0), rows - NA_KH)
            for kp in range(BAND_ROWS // 2):
                codes = []
                for kr in (bs + 2 * kp, bs + 2 * kp + 1):
                    codes.append(kr - r + NA_KH - 1 if rs <= kr < rs + NA_KH else 15)
                index[v, qr, kp] = combos.setdefault(tuple(codes), len(combos))
    lo = np.array([c[0] for c in combos], np.int32)
    hi = np.array([c[1] for c in combos], np.int32)
    return index.reshape(-1), lo, hi


def _nbr_bias_table(rpb, lo, hi):
    nh = rpb.shape[0]
    qc = np.arange(GRID_W)[:, None]
    kc = np.arange(GRID_W)[None, :]
    cs = np.clip(qc - NA_KW // 2, 0, GRID_W - NA_KW)
    inside = (kc >= cs) & (kc < cs + NA_KW)
    dc = np.clip(kc - qc + NA_KW - 1, 0, 2 * NA_KW - 2)
    onehot = (dc.reshape(-1)[None, :] == np.arange(2 * NA_KW - 1)[:, None]).astype(np.float32)
    toep = jnp.einsum("hrc,cq->hrq", rpb, jnp.asarray(onehot), precision=HIGHEST)
    toep = toep.reshape(nh, 2 * NA_KH - 1, GRID_W, GRID_W)
    toep = jnp.where(jnp.asarray(inside)[None, None], toep, NEG)
    toep = jnp.concatenate([toep, jnp.full((nh, 1, GRID_W, GRID_W), NEG, F32)], axis=1)
    return jnp.concatenate([toep[:, lo], toep[:, hi]], axis=-1)


def _pair_heads(q, lane):
    zero = jnp.zeros_like(q)
    return jnp.where(lane < HEAD_DIM, q, zero), jnp.where(lane >= HEAD_DIM, q, zero)


def _nbr_kernel(idx_ref, q_ref, k0, k1, k2, k3, v0, v1, v2, v3, kc_ref, vc_ref, tp_ref,
                o_ref, s_ref, *, nblk):
    blk = pl.program_id(1)
    variant = jnp.where(blk == 0, 0, jnp.where(blk == nblk - 1, 2, 1))
    lane = lax.broadcasted_iota(jnp.int32, (1, LANES), 1)
    outs = []
    for p in range(N_HEADS // 2):
        cols = slice(LANES * p, LANES * (p + 1))
        kband = jnp.concatenate([k0[0, :, cols], k1[0, :, cols], k2[0, :, cols], k3[0, :, cols]], axis=0)
        vband = jnp.concatenate([v0[0, :, cols], v1[0, :, cols], v2[0, :, cols], v3[0, :, cols]], axis=0)
        kctx, vctx = kc_ref[0, :, cols], vc_ref[0, :, cols]
        qp = q_ref[0, :, cols] * jnp.asarray(SM_SCALE, BF16)
        res = []
        for e, qm in enumerate(_pair_heads(qp, lane)):
            h = 2 * p + e
            s_raw = _dot_nt(qm, kband)
            sc = _dot_nt(qm, kctx)
            for qr in range(NBR_ROWS):
                rws = slice(GRID_W * qr, GRID_W * (qr + 1))
                for kp in range(BAND_ROWS // 2):
                    u = idx_ref[variant * (NBR_ROWS * BAND_ROWS // 2) + qr * (BAND_ROWS // 2) + kp]
                    cl = slice(LANES * kp, LANES * (kp + 1))
                    s_ref[rws, cl] = s_raw[rws, cl] + tp_ref[h, u]
            s = s_ref[...]
            m = jnp.maximum(jnp.max(s, axis=-1, keepdims=True), jnp.max(sc, axis=-1, keepdims=True))
            pn = jnp.exp(s - m)
            pc = jnp.exp(sc - m)
            l = jnp.sum(pn, axis=-1, keepdims=True) + jnp.sum(pc, axis=-1, keepdims=True)
            o = _dot(pn.astype(BF16), vband) + _dot(pc.astype(BF16), vctx)
            res.append(o * (1.0 / l))
        outs.append(jnp.where(lane < HEAD_DIM, res[0], res[1]))
    o_ref[0] = jnp.concatenate(outs, axis=1).astype(BF16)


def _nbr_attention(proj_n, idx, tp, n_lat):
    b, n_tok, _ = proj_n.shape
    rows = n_lat // GRID_W
    nblk = rows // NBR_ROWS
    tq = NBR_ROWS * GRID_W
    blk_rows = CHUNK // GRID_W
    n_band = BAND_ROWS // blk_rows
    max_start = rows // blk_rows - n_band
    ctx_blk = n_lat // CHUNK
    q_col, k_col, v_col = 6, 7, 8

    def band_spec(j, col):
        def imap(bi, i, idx_ref):
            start = jnp.clip(2 * i - 1, 0, max_start)
            return (bi, start + j, col)
        return pl.BlockSpec((1, CHUNK, 512), imap)

    in_specs = [pl.BlockSpec((1, tq, 512), lambda bi, i, idx_ref: (bi, i, q_col))]
    in_specs += [band_spec(j, k_col) for j in range(n_band)]
    in_specs += [band_spec(j, v_col) for j in range(n_band)]
    in_specs += [pl.BlockSpec((1, CHUNK, 512), lambda bi, i, idx_ref: (bi, ctx_blk, k_col)),
                 pl.BlockSpec((1, CHUNK, 512), lambda bi, i, idx_ref: (bi, ctx_blk, v_col)),
                 pl.BlockSpec(tp.shape, lambda bi, i, idx_ref: (0, 0, 0, 0))]
    grid_spec = pltpu.PrefetchScalarGridSpec(
        num_scalar_prefetch=1, grid=(b, nblk), in_specs=in_specs,
        out_specs=pl.BlockSpec((1, tq, 512), lambda bi, i, idx_ref: (bi, i, 0)),
        scratch_shapes=[pltpu.VMEM((tq, BAND_ROWS * GRID_W), F32)])
    return pl.pallas_call(
        functools.partial(_nbr_kernel, nblk=nblk),
        grid_spec=grid_spec,
        out_shape=jax.ShapeDtypeStruct((b, n_tok, 512), BF16),
        compiler_params=_params(("arbitrary", "arbitrary")),
        name="nbr_attn",
    )(idx, *([proj_n] * (1 + 2 * n_band + 2)), tp)


def _ctx_c_kernel(q_ref, k_ref, v_ref, prev_ref, o_ref):
    del prev_ref
    lane = lax.broadcasted_iota(jnp.int32, (1, LANES), 1)
    outs = []
    for p in range(N_HEADS // 2):
        cols = slice(LANES * p, LANES * (p + 1))
        kp_, vp = k_ref[0, :, cols], v_ref[0, :, cols]
        qp = q_ref[0, :, cols] * jnp.asarray(SM_SCALE, BF16)
        res = []
        for qm in _pair_heads(qp, lane):
            s = _dot_nt(qm, kp_)
            m = jnp.max(s, axis=-1, keepdims=True)
            pr = jnp.exp(s - m)
            l = jnp.sum(pr, axis=-1, keepdims=True)
            res.append(_dot(pr.astype(BF16), vp) * (1.0 / l))
        outs.append(jnp.where(lane < HEAD_DIM, res[0], res[1]))
    o_ref[0] = jnp.concatenate(outs, axis=1).astype(BF16)


def _ctx_c_attention(proj_n, o_c, n_lat):
    b, n_tok, _ = proj_n.shape
    n_ctx = n_tok - n_lat
    blk = n_lat // n_ctx
    return pl.pallas_call(
        _ctx_c_kernel,
        grid=(b,),
        in_specs=[pl.BlockSpec((1, n_ctx, 512), lambda bi: (bi, blk, 6)),
                  pl.BlockSpec((1, n_ctx, 512), lambda bi: (bi, blk, 7)),
                  pl.BlockSpec((1, n_ctx, 512), lambda bi: (bi, blk, 8)),
                  pl.BlockSpec(memory_space=pl.ANY)],
        out_specs=pl.BlockSpec((1, n_ctx, 512), lambda bi: (bi, blk, 0)),
        out_shape=jax.ShapeDtypeStruct(o_c.shape, BF16),
        input_output_aliases={3: 0},
        compiler_params=_params(("arbitrary",)),
        name="ctx_c_attn",
    )(proj_n, proj_n, proj_n, o_c)


def _merge_kernel(*refs, with_router):
    (oa_ref, ob_ref, oc_ref, gate_ref, x_ref, wa_ref, wb_ref, wc_ref, wo_ref,
     mod_ref, g_ref) = refs[:11]
    if with_router:
        r_ref, xo_ref, h_ref, route_ref = refs[11:]
    else:
        xo_ref, h_ref = refs[11:]
    d = D_MODEL
    gate = lambda k: _sigmoid(gate_ref[0, :, k * d:(k + 1) * d].astype(F32))
    m = (gate(0) * _dot(oa_ref[0], wa_ref[...])
         + gate(1) * _dot(ob_ref[0], wb_ref[...])
         + gate(2) * _dot(oc_ref[0], wc_ref[...]))
    y = _dot(m.astype(BF16), wo_ref[...])
    for c in range(ROW_TILE // CHUNK):
        rows = slice(c * CHUNK, (c + 1) * CHUNK)
        xn = x_ref[0, rows, :] + mod_ref[0, c, 2:3, :] * y[rows, :]
        xo_ref[0, rows, :] = xn
        h = _norm_mod(xn, g_ref[...], mod_ref[0, c, 3:4, :], mod_ref[0, c, 4:5, :])
        if not with_router:
            h_ref[0, rows, :] = h.astype(BF16)
        else:
            h_ref[0, rows, :] = h
            lane = lax.broadcasted_iota(jnp.int32, (CHUNK, LANES), 1)
            logits = jnp.where(lane < N_EXPERTS, _dot(h, r_ref[...], precision=HIGHEST), NEG)
            m1 = jnp.max(logits, axis=-1, keepdims=True)
            i1 = jnp.min(jnp.where(logits == m1, lane, LANES), axis=-1, keepdims=True)
            rest = jnp.where(lane == i1, NEG, logits)
            m2 = jnp.max(rest, axis=-1, keepdims=True)
            i2 = jnp.min(jnp.where(rest == m2, lane, LANES), axis=-1, keepdims=True)
            e = jnp.exp(m2 - m1)
            w1 = 1.0 / (1.0 + e)
            route_ref[0, rows, :] = jnp.where(
                lane == 0, i1.astype(F32), jnp.where(
                    lane == 1, i2.astype(F32), jnp.where(
                        lane == 2, w1, jnp.where(lane == 3, e * w1, 0.0))))


def _merge(o_a, o_b, o_c, proj_n, x_all, wa, wb, wc, wo, modc, g_ffn, router):
    b, n_tok, d = x_all.shape
    nt = n_tok // ROW_TILE
    with_router = router is not None
    tile = lambda w: pl.BlockSpec((1, ROW_TILE, w), lambda bi, i: (bi, i, 0))
    full = lambda a: pl.BlockSpec(a.shape, lambda bi, i: (0,) * a.ndim)
    in_specs = [tile(512), tile(512), tile(512), tile(3 * d), tile(d),
                full(wa), full(wb), full(wc), full(wo),
                pl.BlockSpec((1, ROW_TILE // CHUNK, 6, d), lambda bi, i: (bi, i, 0, 0)),
                full(g_ffn)]
    args = [o_a, o_b, o_c, proj_n, x_all, wa, wb, wc, wo, modc, g_ffn]
    out_specs = [tile(d), tile(d)]
    out_shape = [jax.ShapeDtypeStruct((b, n_tok, d), F32), jax.ShapeDtypeStruct((b, n_tok, d), BF16)]
    if with_router:
        in_specs.append(full(router))
        args.append(router)
        out_shape[1] = jax.ShapeDtypeStruct((b, n_tok, d), F32)
        out_specs.append(tile(LANES))
        out_shape.append(jax.ShapeDtypeStruct((b, n_tok, LANES), F32))
    return pl.pallas_call(
        functools.partial(_merge_kernel, with_router=with_router),
        grid=(b, nt),
        in_specs=in_specs,
        out_specs=out_specs,
        out_shape=out_shape,
        compiler_params=_params(("arbitrary", "arbitrary")),
        name="merge_router" if with_router else "merge",
    )(*args)


def _swiglu_partial(h, w1, w3, w2):
    a = _dot(h, w1)
    t = (a * _sigmoid(a)) * _dot(h, w3)
    return _dot(t.astype(BF16), w2)


def _ffn_kernel(h_ref, x_ref, w1_ref, w3_ref, w2_ref, mod_ref, o_ref, acc_ref, *, n_f):
    @pl.when(pl.program_id(2) == 0)
    def _():
        acc_ref[...] = jnp.zeros(acc_ref.shape, F32)

    acc_ref[...] += _swiglu_partial(h_ref[0], w1_ref[0], w3_ref[0], w2_ref[0])

    @pl.when(pl.program_id(2) == n_f - 1)
    def _():
        for c in range(ROW_TILE // CHUNK):
            rows = slice(c * CHUNK, (c + 1) * CHUNK)
            o_ref[0, rows, :] = x_ref[0, rows, :] + mod_ref[0, c, 5:6, :] * acc_ref[rows, :]


def _ff_tile(f, cap):
    return max(t for t in range(LANES, min(f, cap) + 1, LANES) if f % t == 0)


def _ffn(h2, x_all, w1, w3, w2, li, modc):
    b, n_tok, d = x_all.shape
    nt = n_tok // ROW_TILE
    f = w1.shape[-1]
    tf = _ff_tile(f, 1408)
    n_f = f // tf
    tile = lambda w: pl.BlockSpec((1, ROW_TILE, w), lambda bi, i, j: (bi, i, 0))
    w13 = pl.BlockSpec((1, d, tf), lambda bi, i, j: (li, 0, j))
    w2s = pl.BlockSpec((1, tf, d), lambda bi, i, j: (li, j, 0))
    mods = pl.BlockSpec((1, ROW_TILE // CHUNK, 6, d), lambda bi, i, j: (bi, i, 0, 0))
    return pl.pallas_call(
        functools.partial(_ffn_kernel, n_f=n_f),
        grid=(b, nt, n_f),
        in_specs=[tile(d), tile(d), w13, w13, w2s, mods],
        out_specs=tile(d),
        out_shape=jax.ShapeDtypeStruct((b, n_tok, d), F32),
        scratch_shapes=[pltpu.VMEM((ROW_TILE, d), F32)],
        compiler_params=_params(("arbitrary",) * 3),
        name="ffn",
    )(h2, x_all, w1, w3, w2, modc)


def _row_copy(src_ref, dst_ref, src_row, dst_row, sem):
    return pltpu.make_async_copy(src_ref.at[pl.ds(src_row, 1), :], dst_ref.at[pl.ds(dst_row, 1), :], sem)


def _gather_start(idx_ref, idx_base, n_rows, src_ref, dst_ref, sem):
    def issue(i, carry):
        _row_copy(src_ref, dst_ref, idx_ref[idx_base + i], i, sem).start()
        return carry

    lax.fori_loop(0, n_rows, issue, 0, unroll=8)


def _gather_wait(n_rows, src_ref, dst_ref, sem):
    def drain(i, carry):
        _row_copy(src_ref, dst_ref, 0, i, sem).wait()
        return carry

    lax.fori_loop(0, n_rows, drain, 0, unroll=8)


def _dispatch_kernel(idx_ref, src_ref, o_ref, sem):
    _gather_start(idx_ref, pl.program_id(0) * MOE_TILE, MOE_TILE, src_ref, o_ref, sem)
    _gather_wait(MOE_TILE, src_ref, o_ref, sem)


def _dispatch(h_rows, slot_token):
    n_slots = slot_token.shape[0]
    d = h_rows.shape[1]
    grid_spec = pltpu.PrefetchScalarGridSpec(
        num_scalar_prefetch=1, grid=(n_slots // MOE_TILE,),
        in_specs=[pl.BlockSpec(memory_space=pl.ANY)],
        out_specs=pl.BlockSpec((MOE_TILE, d), lambda t, idx: (t, 0)),
        scratch_shapes=[pltpu.SemaphoreType.DMA])
    return pl.pallas_call(
        _dispatch_kernel,
        grid_spec=grid_spec,
        out_shape=jax.ShapeDtypeStruct((n_slots, d), F32),
        compiler_params=_params(("arbitrary",)),
        name="moe_dispatch",
    )(slot_token, h_rows)


def _expert_kernel(te_ref, tv_ref, xg_ref, w1_ref, w3_ref, w2_ref, y_ref, h_ref, acc_ref, *, n_f):
    del te_ref
    t, f = pl.program_id(0), pl.program_id(1)

    @pl.when(tv_ref[t] != 0)
    def _():
        @pl.when(f == 0)
        def _():
            h_ref[...] = xg_ref[...].astype(BF16)
            acc_ref[...] = jnp.zeros(acc_ref.shape, F32)

        acc_ref[...] += _swiglu_partial(h_ref[...], w1_ref[0, 0], w3_ref[0, 0], w2_ref[0, 0])

        @pl.when(f == n_f - 1)
        def _():
            y_ref[...] = acc_ref[...]

    @pl.when((tv_ref[t] == 0) & (f == n_f - 1))
    def _():
        y_ref[...] = jnp.zeros(y_ref.shape, F32)


def _experts(xg, tile_expert, tile_valid, w1, w3, w2, li):
    n_slots = xg.shape[0]
    d, f = w1.shape[-2:]
    tf = _ff_tile(f, 896)
    n_f = f // tf
    grid_spec = pltpu.PrefetchScalarGridSpec(
        num_scalar_prefetch=2, grid=(n_slots // MOE_TILE, n_f),
        in_specs=[pl.BlockSpec((MOE_TILE, d), lambda t, j, te, tv: (t, 0)),
                  pl.BlockSpec((1, 1, d, tf), lambda t, j, te, tv: (li, te[t], 0, j)),
                  pl.BlockSpec((1, 1, d, tf), lambda t, j, te, tv: (li, te[t], 0, j)),
                  pl.BlockSpec((1, 1, tf, d), lambda t, j, te, tv: (li, te[t], j, 0))],
        out_specs=pl.BlockSpec((MOE_TILE, d), lambda t, j, te, tv: (t, 0)),
        scratch_shapes=[pltpu.VMEM((MOE_TILE, d), BF16), pltpu.VMEM((MOE_TILE, d), F32)])
    return pl.pallas_call(
        functools.partial(_expert_kernel, n_f=n_f),
        grid_spec=grid_spec,
        out_shape=jax.ShapeDtypeStruct((n_slots, d), F32),
        compiler_params=_params(("arbitrary", "arbitrary")),
        name="moe_experts",
    )(tile_expert, tile_valid, xg, w1, w3, w2)


def _combine_kernel(pos_ref, y_ref, x_ref, route_ref, mod_ref, o_ref, g1_ref, g2_ref, sems, *, n_rows):
    base = (pl.program_id(0) * pl.num_programs(1) + pl.program_id(1)) * ROW_TILE
    _gather_start(pos_ref, base, ROW_TILE, y_ref, g1_ref, sems.at[0])
    _gather_start(pos_ref, n_rows + base, ROW_TILE, y_ref, g2_ref, sems.at[1])
    _gather_wait(ROW_TILE, y_ref, g1_ref, sems.at[0])
    _gather_wait(ROW_TILE, y_ref, g2_ref, sems.at[1])
    for c in range(ROW_TILE // CHUNK):
        rows = slice(c * CHUNK, (c + 1) * CHUNK)
        w1 = route_ref[0, rows, 2:3]
        w2 = route_ref[0, rows, 3:4]
        y = w1 * g1_ref[rows, :] + w2 * g2_ref[rows, :]
        o_ref[0, rows, :] = x_ref[0, rows, :] + mod_ref[0, c, 5:6, :] * y


def _combine(y, pos, x_mid, route, modc):
    b, n_tok, d = x_mid.shape
    nt = n_tok // ROW_TILE
    tile = lambda w: pl.BlockSpec((1, ROW_TILE, w), lambda bi, i, p: (bi, i, 0))
    grid_spec = pltpu.PrefetchScalarGridSpec(
        num_scalar_prefetch=1, grid=(b, nt),
        in_specs=[pl.BlockSpec(memory_space=pl.ANY), tile(d), tile(LANES),
                  pl.BlockSpec((1, ROW_TILE // CHUNK, 6, d), lambda bi, i, p: (bi, i, 0, 0))],
        out_specs=tile(d),
        scratch_shapes=[pltpu.VMEM((ROW_TILE, d), F32),
                        pltpu.VMEM((ROW_TILE, d), F32),
                        pltpu.SemaphoreType.DMA((2,))])
    return pl.pallas_call(
        functools.partial(_combine_kernel, n_rows=b * n_tok),
        grid_spec=grid_spec,
        out_shape=jax.ShapeDtypeStruct((b, n_tok, d), F32),
        compiler_params=_params(("arbitrary", "arbitrary")),
        name="moe_combine",
    )(pos, y, x_mid, route, modc)


def _routing_tables(route):
    n_rows = route.shape[0] * route.shape[1]
    r2 = route.reshape(n_rows, LANES)
    e_flat = jnp.concatenate([r2[:, 0], r2[:, 1]]).astype(jnp.int32)
    onehot = (e_flat[:, None] == jnp.arange(N_EXPERTS, dtype=jnp.int32)[None, :]).astype(jnp.int32)
    csum = jnp.cumsum(onehot, axis=0)
    rank = jnp.take_along_axis(csum, e_flat[:, None], axis=1)[:, 0] - 1
    padded = ((csum[-1] + MOE_TILE - 1) // MOE_TILE) * MOE_TILE
    ends = jnp.cumsum(padded)
    pos = (ends - padded)[e_flat] + rank
    n_slots = 2 * n_rows + N_EXPERTS * MOE_TILE
    token = jnp.arange(2 * n_rows, dtype=jnp.int32) % n_rows
    slot_token = jnp.zeros((n_slots,), jnp.int32).at[pos].set(token)
    tile_start = jnp.arange(n_slots // MOE_TILE, dtype=jnp.int32) * MOE_TILE
    tile_expert = jnp.minimum(jnp.searchsorted(ends, tile_start, side="right"), N_EXPERTS - 1)
    tile_valid = (tile_start < ends[-1]).astype(jnp.int32)
    return slot_token, tile_expert.astype(jnp.int32), tile_valid, pos.astype(jnp.int32)


def _moe(h_rows, x_mid, route, w1, w3, w2, li, modc):
    b, n_tok = x_mid.shape[:2]
    slot_token, tile_expert, tile_valid, pos = _routing_tables(route)
    xg = _dispatch(h_rows.reshape(b * n_tok, -1), slot_token)
    y = _experts(xg, tile_expert, tile_valid, w1, w3, w2, li)
    return _combine(y, pos, x_mid, route, modc)


def _final_kernel(x_ref, g_ref, o_ref):
    x = x_ref[0]
    ms = jnp.mean(x * x, axis=-1, keepdims=True)
    o_ref[0] = x * lax.rsqrt(ms + EPS) * g_ref[...]


def _final_norm(x_all, g, n_lat):
    b, _, d = x_all.shape
    tm = 1024
    return pl.pallas_call(
        _final_kernel,
        grid=(b, n_lat // tm),
        in_specs=[pl.BlockSpec((1, tm, d), lambda bi, i: (bi, i, 0)),
                  pl.BlockSpec((1, d), lambda bi, i: (0, 0))],
        out_specs=pl.BlockSpec((1, tm, d), lambda bi, i: (bi, i, 0)),
        out_shape=jax.ShapeDtypeStruct((b, n_lat, d), F32),
        compiler_params=_params(("arbitrary", "arbitrary")),
        name="final_norm",
    )(x_all, g)


def _deinterleave(w, n_heads):
    lead = w.shape[:-1]
    w = w.reshape(lead + (n_heads, HALF, 2))
    return jnp.swapaxes(w, -1, -2).reshape(lead + (n_heads * HEAD_DIM,))


def _rope_tables(n_lat, n_tok):
    t = np.arange(n_lat)
    inv = ROPE_THETA ** (-np.arange(HEAD_DIM // 4, dtype=np.float64) / (HEAD_DIM // 4))
    ang = np.concatenate([(t // GRID_W)[:, None] * inv, (t % GRID_W)[:, None] * inv], axis=-1)
    ang = np.concatenate([ang, np.zeros((n_tok - n_lat, HALF))], axis=0)
    return jnp.asarray(np.cos(ang).T, F32), jnp.asarray(np.sin(ang).T, F32)


def kernel(x, c, ctx, c_ctx, w_mod, b_mod, g_mix, g_ffn, w_in, qn_a, kn_a, lam_q1, lam_k1, lam_q2,
           lam_k2, subln_b, rpb_c, w_br_a, w_br_b, w_br_c, w_out, ffn_w1, ffn_w3, ffn_w2, router,
           moe_w1, moe_w3, moe_w2, g_final):
    b, n_lat, d = x.shape
    n_ctx = ctx.shape[1]
    n_tok = n_lat + n_ctx
    depth = w_in.shape[0]
    assert d == D_MODEL and n_ctx == CHUNK and b + 1 <= 8
    assert n_tok % ROW_TILE == 0 and n_tok % KV_TILE == 0 and n_lat % Q_TILE == 0
    assert n_lat % (NBR_ROWS * GRID_W) == 0 and n_lat // GRID_W >= 24
    n_chunks = n_tok // CHUNK

    x_all = jnp.concatenate([x, ctx], axis=1)
    cos_t, sin_t = _rope_tables(n_lat, n_tok)

    c_rows = jnp.concatenate([c, c_ctx[None], jnp.zeros((7 - b, d), F32)], axis=0)
    mods = _mod_vectors(c_rows, w_mod, b_mod).reshape(depth, 8, 6, d)
    mod_lat = jnp.broadcast_to(mods[:, :b, None], (depth, b, n_chunks - 1, 6, d))
    mod_ctx = jnp.broadcast_to(mods[:, b:b + 1, None], (depth, b, 1, 6, d))
    mod_chunks = jnp.concatenate([mod_lat, mod_ctx], axis=2)

    sl = lambda name: w_in[:, :, _OFF[name][0]:_OFF[name][1]]
    w_n = jnp.concatenate([sl("gate"), sl("cq"), sl("ck"), sl("cv")], axis=-1).astype(BF16)
    w_t = jnp.concatenate([_deinterleave(sl("aq"), 8), _deinterleave(sl("bq"), 8), sl("bv"),
                           _deinterleave(sl("bk"), 8), sl("av"), _deinterleave(sl("ak"), 2)], axis=-1)
    w_t = jnp.swapaxes(w_t, 1, 2).astype(BF16)
    qn_t = _deinterleave(qn_a, 1)[:, :, None]
    kn_t = _deinterleave(kn_a, 1)[:, :, None]
    lam_vecs = jnp.pad(jnp.stack([lam_q1, lam_k1, lam_q2, lam_k2], axis=1),
                       ((0, 0), (0, 4), (0, LANES - HEAD_DIM)))
    wa, wb, wc, wo = (w.astype(BF16) for w in (w_br_a, w_br_b, w_br_c, w_out))
    f1, f3, f2 = (w.astype(BF16) for w in (ffn_w1, ffn_w3, ffn_w2))
    m1, m3, m2 = (w.astype(BF16) for w in (moe_w1, moe_w3, moe_w2))
    router_p = jnp.pad(router, ((0, 0), (0, 0), (0, LANES - N_EXPERTS)))
    nbr_idx, nbr_lo, nbr_hi = _nbr_tables(n_lat // GRID_W)
    nbr_idx = jnp.asarray(nbr_idx)

    for layer in range(depth):
        lam_init = 0.8 - 0.6 * math.exp(-0.3 * layer)
        modc = mod_chunks[layer]
        g_mix_l = g_mix[layer][None]
        proj_n = _inproj(x_all, g_mix_l, modc, w_n[layer], transposed=False)
        proj_t = _inproj(x_all, g_mix_l, modc, w_t[layer], transposed=True)
        k_b, k_a = _kprep(proj_t, cos_t, sin_t, kn_t[layer])

        o_a = _attention("A", proj_t, k_a, cos_t, sin_t, [qn_t[layer]], lam_init, n_lat)
        o_a = _attention("A", proj_t, k_a, cos_t, sin_t, [qn_t[layer]], lam_init, n_lat, ctx_into=o_a)
        aux_b = [lam_vecs[layer], subln_b[layer][:, None]]
        o_b = _attention("B", proj_t, k_b, cos_t, sin_t, aux_b, lam_init, n_lat)
        o_b = _attention("B", proj_t, k_b, cos_t, sin_t, aux_b, lam_init, n_lat, ctx_into=o_b)
        tp = _nbr_bias_table(rpb_c[layer], nbr_lo, nbr_hi)
        o_c = _nbr_attention(proj_n, nbr_idx, tp, n_lat)
        o_c = _ctx_c_attention(proj_n, o_c, n_lat)

        is_moe = layer % 2 == 1
        li = layer // 2
        merged = _merge(o_a, o_b, o_c, proj_n, x_all, wa[layer], wb[layer], wc[layer], wo[layer],
                        modc, g_ffn[layer][None], router_p[li] if is_moe else None)
        if is_moe:
            x_mid, h_rows, route = merged
            x_all = _moe(h_rows, x_mid, route, m1, m3, m2, li, modc)
        else:
            x_mid, h2 = merged
            x_all = _ffn(h2, x_mid, f1, f3, f2, li, modc)

    return _final_norm(x_all, g_final[None], n_lat)
```

```python
import functools
import math

import numpy as np
import jax
import jax.numpy as jnp
from jax import lax
from jax.experimental import pallas as pl
from jax.experimental.pallas import tpu as pltpu

F32 = jnp.float32
BF16 = jnp.bfloat16
HIGHEST = lax.Precision.HIGHEST

D_MODEL = 1024
HEAD_DIM = 64
HALF = HEAD_DIM // 2
GRID_W = 64
ROPE_THETA = 10000.0
EPS = 1e-6
N_HEADS = 8
A_GROUP = 4
B_HEADS = 4
B_V_DIM = 128
NA_KH = 8
NA_KW = 16
N_EXPERTS = 8
LANES = 128
CHUNK = 256
ROW_TILE = 768
Q_TILE = 1024
KV_TILE = 768
MOE_TILE = 512
NBR_ROWS = 8
BAND_ROWS = 16
HALF_ROWS = 4
HALF_BAND = 12
NEG = -1e30
SM_SCALE = HEAD_DIM ** -0.5
LOG2E = math.log2(math.e)
SUM_ROWS = 16
STAB_MARGIN = 64.0
STAB_KEYS = 128
VMEM_LIMIT = 56 * 1024 * 1024

_OFF = {}
_o = 0
for _name, _size in (("aq", 512), ("ak", 128), ("av", 128), ("bq", 512), ("bk", 512),
                     ("bv", 512), ("cq", 512), ("ck", 512), ("cv", 512), ("gate", 3072)):
    _OFF[_name] = (_o, _o + _size)
    _o += _size
D_IN = _o
N_COLS = 4608
T_COLS = 2304


def _params(sem):
    return pltpu.CompilerParams(dimension_semantics=sem, vmem_limit_bytes=VMEM_LIMIT)


def _sigmoid(x):
    return 1.0 / (1.0 + jnp.exp(-x))


def _dot(a, b, **kw):
    return jnp.dot(a, b, preferred_element_type=F32, **kw)


def _dot_nt(a, b):
    return lax.dot_general(a, b, (((1,), (1,)), ((), ())), preferred_element_type=F32)


def _norm_mod(x, g, shift, scale):
    ms = jnp.mean(x * x, axis=-1, keepdims=True)
    y = x * lax.rsqrt(ms + EPS) * g
    return y * (1.0 + scale) + shift


def _mod_kernel(c_ref, w_ref, b_ref, o_ref):
    c = c_ref[...]
    s = c * _sigmoid(c)
    o_ref[0] = _dot(s, w_ref[0], precision=HIGHEST) + b_ref[0]


def _mod_vectors(c_rows, w_mod, b_mod):
    depth, d, n = w_mod.shape
    tn = 1536
    return pl.pallas_call(
        _mod_kernel,
        grid=(depth, n // tn),
        in_specs=[pl.BlockSpec((8, d), lambda l, j: (0, 0)),
                  pl.BlockSpec((1, d, tn), lambda l, j: (l, 0, j)),
                  pl.BlockSpec((1, 1, tn), lambda l, j: (l, 0, j))],
        out_specs=pl.BlockSpec((1, 8, tn), lambda l, j: (l, 0, j)),
        out_shape=jax.ShapeDtypeStruct((depth, 8, n), F32),
        compiler_params=_params(("arbitrary", "arbitrary")),
        name="mod_vectors",
    )(c_rows, w_mod, b_mod.reshape(depth, 1, n))


def _inproj_kernel(x_ref, g_ref, mod_ref, w_ref, o_ref, h_ref, *, transposed):
    @pl.when(pl.program_id(2) == 0)
    def _():
        for c in range(ROW_TILE // CHUNK):
            rows = slice(c * CHUNK, (c + 1) * CHUNK)
            h = _norm_mod(x_ref[0, rows, :], g_ref[...], mod_ref[0, c, 0:1, :], mod_ref[0, c, 1:2, :])
            h_ref[rows, :] = h.astype(BF16)

    if transposed:
        o_ref[0] = _dot_nt(w_ref[...], h_ref[...]).astype(BF16)
    else:
        o_ref[0] = _dot(h_ref[...], w_ref[...]).astype(BF16)


def _inproj(x_all, g, modc, w, transposed):
    b, n_tok, d = x_all.shape
    nt = n_tok // ROW_TILE
    if transposed:
        cols, tn = w.shape[0], 768
        w_spec = pl.BlockSpec((tn, d), lambda bi, i, j: (j, 0))
        o_spec = pl.BlockSpec((1, tn, ROW_TILE), lambda bi, i, j: (bi, j, i))
        o_shape = (b, cols, n_tok)
    else:
        cols, tn = w.shape[1], 2304
        w_spec = pl.BlockSpec((d, tn), lambda bi, i, j: (0, j))
        o_spec = pl.BlockSpec((1, ROW_TILE, tn), lambda bi, i, j: (bi, i, j))
        o_shape = (b, n_tok, cols)
    return pl.pallas_call(
        functools.partial(_inproj_kernel, transposed=transposed),
        grid=(b, nt, cols // tn),
        in_specs=[pl.BlockSpec((1, ROW_TILE, d), lambda bi, i, j: (bi, i, 0)),
                  pl.BlockSpec((1, d), lambda bi, i, j: (0, 0)),
                  pl.BlockSpec((1, ROW_TILE // CHUNK, 6, d), lambda bi, i, j: (bi, i, 0, 0)),
                  w_spec],
        out_specs=o_spec,
        out_shape=jax.ShapeDtypeStruct(o_shape, BF16),
        scratch_shapes=[pltpu.VMEM((ROW_TILE, d), BF16)],
        compiler_params=_params(("arbitrary", "arbitrary", "arbitrary")),
        name="inproj_t" if transposed else "inproj_n",
    )(x_all, g, modc, w)


def _rope_t(x, c, s):
    x1, x2 = x[:HALF], x[HALF:]
    return jnp.concatenate([x1 * c - x2 * s, x1 * s + x2 * c], axis=0)


def _rmsnorm_t(x, g):
    ms = jnp.mean(x * x, axis=0, keepdims=True)
    return x * lax.rsqrt(ms + EPS) * g


def _kprep_kernel(kb_ref, ka_ref, cos_ref, sin_ref, kn_ref, ob_ref, oa_ref):
    c, s = cos_ref[...], sin_ref[...]
    kn = kn_ref[...]
    outs = []
    for h in range(N_HEADS):
        x = kb_ref[0, HEAD_DIM * h:HEAD_DIM * (h + 1), :].astype(F32)
        outs.append(_rope_t(x, c, s))
    ob_ref[0] = jnp.concatenate(outs, axis=0).T.astype(BF16)
    outs = []
    for h in range(N_HEADS // A_GROUP):
        x = ka_ref[0, HEAD_DIM * h:HEAD_DIM * (h + 1), :].astype(F32)
        outs.append(_rope_t(_rmsnorm_t(x, kn), c, s))
    oa_ref[0] = jnp.concatenate(outs, axis=0).T.astype(BF16)


def _kprep(proj_t, cos_t, sin_t, kn):
    b, _, n_tok = proj_t.shape
    nt = n_tok // ROW_TILE
    return pl.pallas_call(
        _kprep_kernel,
        grid=(b, nt),
        in_specs=[pl.BlockSpec((1, 512, ROW_TILE), lambda bi, i: (bi, 3, i)),
                  pl.BlockSpec((1, 128, ROW_TILE), lambda bi, i: (bi, 17, i)),
                  pl.BlockSpec((HALF, ROW_TILE), lambda bi, i: (0, i)),
                  pl.BlockSpec((HALF, ROW_TILE), lambda bi, i: (0, i)),
                  pl.BlockSpec((HEAD_DIM, 1), lambda bi, i: (0, 0))],
        out_specs=[pl.BlockSpec((1, ROW_TILE, 512), lambda bi, i: (bi, i, 0)),
                   pl.BlockSpec((1, ROW_TILE, 128), lambda bi, i: (bi, i, 0))],
        out_shape=[jax.ShapeDtypeStruct((b, n_tok, 512), BF16),
                   jax.ShapeDtypeStruct((b, n_tok, 128), BF16)],
        compiler_params=_params(("arbitrary", "arbitrary")),
        name="kprep",
    )(proj_t, proj_t, cos_t, sin_t, kn)


def _attn_kernel(*refs, mode, nkv, lam_init, aliased):
    if mode == "A":
        q_ref, k_ref, v_ref, cos_ref, sin_ref, qn_ref = refs[:6]
        rest = refs[6:]
    else:
        q_ref, k_ref, v_ref, cos_ref, sin_ref, lamv_ref, sub_ref = refs[:7]
        rest = refs[7:]
    if aliased:
        rest = rest[1:]
    o_ref, qz_ref, m_ref, acc_ref, redo_ref = rest
    kv = pl.program_id(2)
    dv = HEAD_DIM if mode == "A" else B_V_DIM

    def scores(h, rows=slice(None)):
        if mode == "A":
            kblk = k_ref[0, rows, :]
        else:
            kblk = k_ref[0, rows, LANES * (h // 2):LANES * (h // 2 + 1)]
        return _dot(kblk, qz_ref[h])

    @pl.when(kv == 0)
    def _init():
        c, s = cos_ref[...], sin_ref[...]
        for h in range(N_HEADS):
            x = q_ref[0, HEAD_DIM * h:HEAD_DIM * (h + 1), :].astype(F32)
            if mode == "A":
                x = _rmsnorm_t(x, qn_ref[...])
            xb = (_rope_t(x, c, s) * (SM_SCALE * LOG2E)).astype(BF16)
            z = jnp.zeros_like(xb)
            half = (h // A_GROUP) if mode == "A" else (h % 2)
            qz_ref[h] = jnp.concatenate([xb, z] if half == 0 else [z, xb], axis=0)
            m_ref[h:h + 1, :] = jnp.max(scores(h, slice(0, STAB_KEYS)), axis=0, keepdims=True)
        acc_ref[0] = jnp.zeros(acc_ref.shape[1:], F32)

    acc_in = acc_ref.at[kv % 2]
    acc_out = acc_ref.at[(kv + 1) % 2]
    ones = jnp.ones((SUM_ROWS, k_ref.shape[1]), BF16)

    def values(h):
        if mode == "A":
            g = h // A_GROUP
            vt = v_ref[0, HEAD_DIM * g:HEAD_DIM * (g + 1), :]
        else:
            hh = h % B_HEADS
            vt = v_ref[0, B_V_DIM * hh:B_V_DIM * (hh + 1), :]
        return jnp.concatenate([vt, ones], axis=0)

    excess = None
    for h in range(N_HEADS):
        s = scores(h)
        m = m_ref[h:h + 1, :]
        d = jnp.max(s, axis=0, keepdims=True) - m
        excess = d if excess is None else jnp.maximum(excess, d)
        p = jnp.exp2((s - m).astype(BF16))
        acc_out[h] = acc_in[h] + _dot(values(h), p)
    redo_ref[0] = (jnp.max(excess) > STAB_MARGIN).astype(jnp.int32)

    @pl.when(redo_ref[0] != 0)
    def _exact_pass():
        for h in range(N_HEADS):
            s = scores(h)
            m_prev = m_ref[h:h + 1, :]
            m_new = jnp.maximum(m_prev, jnp.max(s, axis=0, keepdims=True))
            alpha = jnp.exp2(m_prev - m_new)
            p = jnp.exp2((s - m_new).astype(BF16))
            acc_out[h] = alpha * acc_in[h] + _dot(values(h), p)
            m_ref[h:h + 1, :] = m_new

    @pl.when(kv == nkv - 1)
    def _fin():
        acc = acc_ref.at[nkv % 2]

        def normalized(h):
            return acc[h, 0:dv, :] * (1.0 / acc[h, dv:dv + 1, :])

        outs = []
        if mode == "A":
            for h in range(N_HEADS):
                outs.append(normalized(h))
        else:
            lv = lamv_ref[...]
            lam = (jnp.exp(jnp.sum(lv[0:1] * lv[1:2], axis=1, keepdims=True))
                   - jnp.exp(jnp.sum(lv[2:3] * lv[3:4], axis=1, keepdims=True)) + lam_init)
            for hh in range(B_HEADS):
                o = normalized(hh) - lam * normalized(hh + B_HEADS)
                outs.append(_rmsnorm_t(o, sub_ref[...]) * (1.0 - lam_init))
        o_ref[0] = jnp.concatenate(outs, axis=0).T.astype(BF16)


def _attention(mode, proj_t, k, cos_t, sin_t, aux, lam_init, n_lat, ctx_into=None):
    b, _, n_tok = proj_t.shape
    is_ctx = ctx_into is not None
    if is_ctx:
        tq = tk = n_tok - n_lat
        nq, nkv = 1, 1
        q_off = k_off = n_lat // tq
    else:
        tq, tk = Q_TILE, KV_TILE
        nq, nkv = n_lat // tq, n_tok // tk
        q_off = k_off = 0
    q_blk = 0 if mode == "A" else 1
    if mode == "A":
        kw, vw, v_blk, dv = 128, 128, 16, HEAD_DIM
    else:
        kw, vw, v_blk, dv = 512, 512, 2, B_V_DIM
    in_specs = [pl.BlockSpec((1, 512, tq), lambda bi, i, j: (bi, q_blk, i + q_off)),
                pl.BlockSpec((1, tk, kw), lambda bi, i, j: (bi, j + k_off, 0)),
                pl.BlockSpec((1, vw, tk), lambda bi, i, j: (bi, v_blk, j + k_off)),
                pl.BlockSpec((HALF, tq), lambda bi, i, j: (0, i + q_off)),
                pl.BlockSpec((HALF, tq), lambda bi, i, j: (0, i + q_off))]
    args = [proj_t, k, proj_t, cos_t, sin_t]
    for a in aux:
        in_specs.append(pl.BlockSpec(a.shape, lambda bi, i, j: (0, 0)))
        args.append(a)
    aliases = {}
    if is_ctx:
        in_specs.append(pl.BlockSpec(memory_space=pl.ANY))
        args.append(ctx_into)
        aliases = {len(args) - 1: 0}
    return pl.pallas_call(
        functools.partial(_attn_kernel, mode=mode, nkv=nkv, lam_init=lam_init, aliased=is_ctx),
        grid=(b, nq, nkv),
        in_specs=in_specs,
        out_specs=pl.BlockSpec((1, tq, 512), lambda bi, i, j: (bi, i + q_off, 0)),
        out_shape=jax.ShapeDtypeStruct((b, n_tok, 512), BF16),
        scratch_shapes=[pltpu.VMEM((N_HEADS, LANES, tq), BF16),
                        pltpu.VMEM((N_HEADS, tq), F32),
                        pltpu.VMEM((2, N_HEADS, dv + SUM_ROWS, tq), F32),
                        pltpu.SMEM((1,), jnp.int32)],
        input_output_aliases=aliases,
        compiler_params=_params(("arbitrary", "arbitrary", "arbitrary")),
        name=f"attn_{mode.lower()}_{'ctx' if is_ctx else 'lat'}",
    )(*args)


def _nbr_tables(rows):
    nblk = rows // NBR_ROWS
    combos, index = {}, np.zeros((3, NBR_ROWS, BAND_ROWS // 2), np.int32)
    for v, blk in enumerate((0, 1, nblk - 1)):
        bs = NBR_ROWS * blk - NA_KH // 2
        for qr in range(NBR_ROWS):
            r = NBR_ROWS * blk + qr
            rs = min(max(r - NA_KH // 2, 0), rows - NA_KH)
            half = qr // HALF_ROWS
            assert bs + NA_KH // 2 * half <= rs and rs + NA_KH <= bs + NA_KH // 2 * half + HALF_BAND
            for kp in range(BAND_ROWS // 2):
                codes = []
                for kr in (bs + 2 * kp, bs + 2 * kp + 1):
                    codes.append(kr - r + NA_KH - 1 if rs <= kr < rs + NA_KH else 15)
                index[v, qr, kp] = combos.setdefault(tuple(codes), len(combos))
    lo = np.array([c[0] for c in combos], np.int32)
    hi = np.array([c[1] for c in combos], np.int32)
    return index.reshape(-1), lo, hi


def _nbr_bias_table(rpb, lo, hi):
    nh = rpb.shape[0]
    qc = np.arange(GRID_W)[:, None]
    kc = np.arange(GRID_W)[None, :]
    cs = np.clip(qc - NA_KW // 2, 0, GRID_W - NA_KW)
    inside = (kc >= cs) & (kc < cs + NA_KW)
    dc = np.clip(kc - qc + NA_KW - 1, 0, 2 * NA_KW - 2)
    onehot = (dc.reshape(-1)[None, :] == np.arange(2 * NA_KW - 1)[:, None]).astype(np.float32)
    toep = jnp.einsum("hrc,cq->hrq", rpb, jnp.asarray(onehot), precision=HIGHEST)
    toep = toep.reshape(nh, 2 * NA_KH - 1, GRID_W, GRID_W)
    toep = jnp.where(jnp.asarray(inside)[None, None], toep, NEG)
    toep = jnp.concatenate([toep, jnp.full((nh, 1, GRID_W, GRID_W), NEG, F32)], axis=1)
    return jnp.concatenate([toep[:, lo], toep[:, hi]], axis=-1)


def _pair_heads(q, lane):
    zero = jnp.zeros_like(q)
    return jnp.where(lane < HEAD_DIM, q, zero), jnp.where(lane >= HEAD_DIM, q, zero)


def _nbr_kernel(idx_ref, q_ref, k0, k1, k2, k3, v0, v1, v2, v3, kc_ref, vc_ref, tp_ref,
                o_ref, s_ref, *, nblk):
    blk = pl.program_id(1)
    variant = jnp.where(blk == 0, 0, jnp.where(blk == nblk - 1, 2, 1))
    lane = lax.broadcasted_iota(jnp.int32, (1, LANES), 1)
    n_half = HALF_ROWS * GRID_W
    kp_per_half = HALF_BAND // 2
    kblocks, vblocks = (k0, k1, k2, k3), (v0, v1, v2, v3)
    outs = []
    for p in range(N_HEADS // 2):
        cols = slice(LANES * p, LANES * (p + 1))
        kctx, vctx = kc_ref[0, :, cols], vc_ref[0, :, cols]
        halves = []
        for half in range(NBR_ROWS // HALF_ROWS):
            kband = jnp.concatenate([r[0, :, cols] for r in kblocks[half:half + 3]], axis=0)
            vband = jnp.concatenate([r[0, :, cols] for r in vblocks[half:half + 3]], axis=0)
            qp = q_ref[0, n_half * half:n_half * (half + 1), cols] * jnp.asarray(SM_SCALE, BF16)
            res = []
            for e, qm in enumerate(_pair_heads(qp, lane)):
                h = 2 * p + e
                s_raw = _dot_nt(qm, kband)
                sc = _dot_nt(qm, kctx)
                for qi in range(HALF_ROWS):
                    qr = HALF_ROWS * half + qi
                    rws = slice(GRID_W * qi, GRID_W * (qi + 1))
                    for ki in range(kp_per_half):
                        kp = 2 * half + ki
                        u = idx_ref[variant * (NBR_ROWS * BAND_ROWS // 2) + qr * (BAND_ROWS // 2) + kp]
                        cl = slice(LANES * ki, LANES * (ki + 1))
                        s_ref[rws, cl] = s_raw[rws, cl] + tp_ref[h, u]
                s = s_ref[...]
                m = jnp.maximum(jnp.max(s, axis=-1, keepdims=True), jnp.max(sc, axis=-1, keepdims=True))
                pn = jnp.exp(s - m)
                pc = jnp.exp(sc - m)
                l = jnp.sum(pn, axis=-1, keepdims=True) + jnp.sum(pc, axis=-1, keepdims=True)
                o = _dot(pn.astype(BF16), vband) + _dot(pc.astype(BF16), vctx)
                res.append(o * (1.0 / l))
            halves.append(jnp.where(lane < HEAD_DIM, res[0], res[1]))
        outs.append(jnp.concatenate(halves, axis=0))
    o_ref[0] = jnp.concatenate(outs, axis=1).astype(BF16)


def _nbr_attention(proj_n, idx, tp, n_lat):
    b, n_tok, _ = proj_n.shape
    rows = n_lat // GRID_W
    nblk = rows // NBR_ROWS
    tq = NBR_ROWS * GRID_W
    blk_rows = CHUNK // GRID_W
    n_band = BAND_ROWS // blk_rows
    last_blk = rows // blk_rows - 1
    ctx_blk = n_lat // CHUNK
    q_col, k_col, v_col = 6, 7, 8

    def band_spec(j, col):
        def imap(bi, i, idx_ref):
            return (bi, jnp.clip(2 * i - 1 + j, 0, last_blk), col)
        return pl.BlockSpec((1, CHUNK, 512), imap)

    in_specs = [pl.BlockSpec((1, tq, 512), lambda bi, i, idx_ref: (bi, i, q_col))]
    in_specs += [band_spec(j, k_col) for j in range(n_band)]
    in_specs += [band_spec(j, v_col) for j in range(n_band)]
    in_specs += [pl.BlockSpec((1, CHUNK, 512), lambda bi, i, idx_ref: (bi, ctx_blk, k_col)),
                 pl.BlockSpec((1, CHUNK, 512), lambda bi, i, idx_ref: (bi, ctx_blk, v_col)),
                 pl.BlockSpec(tp.shape, lambda bi, i, idx_ref: (0, 0, 0, 0))]
    grid_spec = pltpu.PrefetchScalarGridSpec(
        num_scalar_prefetch=1, grid=(b, nblk), in_specs=in_specs,
        out_specs=pl.BlockSpec((1, tq, 512), lambda bi, i, idx_ref: (bi, i, 0)),
        scratch_shapes=[pltpu.VMEM((HALF_ROWS * GRID_W, HALF_BAND * GRID_W), F32)])
    return pl.pallas_call(
        functools.partial(_nbr_kernel, nblk=nblk),
        grid_spec=grid_spec,
        out_shape=jax.ShapeDtypeStruct((b, n_tok, 512), BF16),
        compiler_params=_params(("arbitrary", "arbitrary")),
        name="nbr_attn",
    )(idx, *([proj_n] * (1 + 2 * n_band + 2)), tp)


def _ctx_c_kernel(q_ref, k_ref, v_ref, prev_ref, o_ref):
    del prev_ref
    lane = lax.broadcasted_iota(jnp.int32, (1, LANES), 1)
    outs = []
    for p in range(N_HEADS // 2):
        cols = slice(LANES * p, LANES * (p + 1))
        kp_, vp = k_ref[0, :, cols], v_ref[0, :, cols]
        qp = q_ref[0, :, cols] * jnp.asarray(SM_SCALE, BF16)
        res = []
        for qm in _pair_heads(qp, lane):
            s = _dot_nt(qm, kp_)
            m = jnp.max(s, axis=-1, keepdims=True)
            pr = jnp.exp(s - m)
            l = jnp.sum(pr, axis=-1, keepdims=True)
            res.append(_dot(pr.astype(BF16), vp) * (1.0 / l))
        outs.append(jnp.where(lane < HEAD_DIM, res[0], res[1]))
    o_ref[0] = jnp.concatenate(outs, axis=1).astype(BF16)


def _ctx_c_attention(proj_n, o_c, n_lat):
    b, n_tok, _ = proj_n.shape
    n_ctx = n_tok - n_lat
    blk = n_lat // n_ctx
    return pl.pallas_call(
        _ctx_c_kernel,
        grid=(b,),
        in_specs=[pl.BlockSpec((1, n_ctx, 512), lambda bi: (bi, blk, 6)),
                  pl.BlockSpec((1, n_ctx, 512), lambda bi: (bi, blk, 7)),
                  pl.BlockSpec((1, n_ctx, 512), lambda bi: (bi, blk, 8)),
                  pl.BlockSpec(memory_space=pl.ANY)],
        out_specs=pl.BlockSpec((1, n_ctx, 512), lambda bi: (bi, blk, 0)),
        out_shape=jax.ShapeDtypeStruct(o_c.shape, BF16),
        input_output_aliases={3: 0},
        compiler_params=_params(("arbitrary",)),
        name="ctx_c_attn",
    )(proj_n, proj_n, proj_n, o_c)


def _merge_kernel(*refs, with_router):
    (oa_ref, ob_ref, oc_ref, gate_ref, x_ref, wa_ref, wb_ref, wc_ref, wo_ref,
     mod_ref, g_ref) = refs[:11]
    if with_router:
        r_ref, xo_ref, h_ref, route_ref = refs[11:]
    else:
        xo_ref, h_ref = refs[11:]
    d = D_MODEL
    gate = lambda k: _sigmoid(gate_ref[0, :, k * d:(k + 1) * d].astype(F32))
    m = (gate(0) * _dot(oa_ref[0], wa_ref[...])
         + gate(1) * _dot(ob_ref[0], wb_ref[...])
         + gate(2) * _dot(oc_ref[0], wc_ref[...]))
    y = _dot(m.astype(BF16), wo_ref[...])
    for c in range(ROW_TILE // CHUNK):
        rows = slice(c * CHUNK, (c + 1) * CHUNK)
        xn = x_ref[0, rows, :] + mod_ref[0, c, 2:3, :] * y[rows, :]
        xo_ref[0, rows, :] = xn
        h = _norm_mod(xn, g_ref[...], mod_ref[0, c, 3:4, :], mod_ref[0, c, 4:5, :])
        if not with_router:
            h_ref[0, rows, :] = h.astype(BF16)
        else:
            h_ref[0, rows, :] = h
            lane = lax.broadcasted_iota(jnp.int32, (CHUNK, LANES), 1)
            logits = jnp.where(lane < N_EXPERTS, _dot(h, r_ref[...], precision=HIGHEST), NEG)
            m1 = jnp.max(logits, axis=-1, keepdims=True)
            i1 = jnp.min(jnp.where(logits == m1, lane, LANES), axis=-1, keepdims=True)
            rest = jnp.where(lane == i1, NEG, logits)
            m2 = jnp.max(rest, axis=-1, keepdims=True)
            i2 = jnp.min(jnp.where(rest == m2, lane, LANES), axis=-1, keepdims=True)
            e = jnp.exp(m2 - m1)
            w1 = 1.0 / (1.0 + e)
            route_ref[0, rows, :] = jnp.where(
                lane == 0, i1.astype(F32), jnp.where(
                    lane == 1, i2.astype(F32), jnp.where(
                        lane == 2, w1, jnp.where(lane == 3, e * w1, 0.0))))


def _merge(o_a, o_b, o_c, proj_n, x_all, wa, wb, wc, wo, modc, g_ffn, router):
    b, n_tok, d = x_all.shape
    nt = n_tok // ROW_TILE
    with_router = router is not None
    tile = lambda w: pl.BlockSpec((1, ROW_TILE, w), lambda bi, i: (bi, i, 0))
    full = lambda a: pl.BlockSpec(a.shape, lambda bi, i: (0,) * a.ndim)
    in_specs = [tile(512), tile(512), tile(512), tile(3 * d), tile(d),
                full(wa), full(wb), full(wc), full(wo),
                pl.BlockSpec((1, ROW_TILE // CHUNK, 6, d), lambda bi, i: (bi, i, 0, 0)),
                full(g_ffn)]
    args = [o_a, o_b, o_c, proj_n, x_all, wa, wb, wc, wo, modc, g_ffn]
    out_specs = [tile(d), tile(d)]
    out_shape = [jax.ShapeDtypeStruct((b, n_tok, d), F32), jax.ShapeDtypeStruct((b, n_tok, d), BF16)]
    if with_router:
        in_specs.append(full(router))
        args.append(router)
        out_shape[1] = jax.ShapeDtypeStruct((b, n_tok, d), F32)
        out_specs.append(tile(LANES))
        out_shape.append(jax.ShapeDtypeStruct((b, n_tok, LANES), F32))
    return pl.pallas_call(
        functools.partial(_merge_kernel, with_router=with_router),
        grid=(b, nt),
        in_specs=in_specs,
        out_specs=out_specs,
        out_shape=out_shape,
        compiler_params=_params(("arbitrary", "arbitrary")),
        name="merge_router" if with_router else "merge",
    )(*args)


def _swiglu_partial(h, w1, w3, w2):
    a = _dot(h, w1)
    t = (a * _sigmoid(a)) * _dot(h, w3)
    return _dot(t.astype(BF16), w2)


def _ffn_kernel(h_ref, x_ref, w1_ref, w3_ref, w2_ref, mod_ref, o_ref, acc_ref, *, n_f):
    @pl.when(pl.program_id(2) == 0)
    def _():
        acc_ref[...] = jnp.zeros(acc_ref.shape, F32)

    acc_ref[...] += _swiglu_partial(h_ref[0], w1_ref[0], w3_ref[0], w2_ref[0])

    @pl.when(pl.program_id(2) == n_f - 1)
    def _():
        for c in range(ROW_TILE // CHUNK):
            rows = slice(c * CHUNK, (c + 1) * CHUNK)
            o_ref[0, rows, :] = x_ref[0, rows, :] + mod_ref[0, c, 5:6, :] * acc_ref[rows, :]


def _ff_tile(f, cap):
    return max(t for t in range(LANES, min(f, cap) + 1, LANES) if f % t == 0)


def _ffn(h2, x_all, w1, w3, w2, li, modc):
    b, n_tok, d = x_all.shape
    nt = n_tok // ROW_TILE
    f = w1.shape[-1]
    tf = _ff_tile(f, 1408)
    n_f = f // tf
    tile = lambda w: pl.BlockSpec((1, ROW_TILE, w), lambda bi, i, j: (bi, i, 0))
    w13 = pl.BlockSpec((1, d, tf), lambda bi, i, j: (li, 0, j))
    w2s = pl.BlockSpec((1, tf, d), lambda bi, i, j: (li, j, 0))
    mods = pl.BlockSpec((1, ROW_TILE // CHUNK, 6, d), lambda bi, i, j: (bi, i, 0, 0))
    return pl.pallas_call(
        functools.partial(_ffn_kernel, n_f=n_f),
        grid=(b, nt, n_f),
        in_specs=[tile(d), tile(d), w13, w13, w2s, mods],
        out_specs=tile(d),
        out_shape=jax.ShapeDtypeStruct((b, n_tok, d), F32),
        scratch_shapes=[pltpu.VMEM((ROW_TILE, d), F32)],
        compiler_params=_params(("arbitrary",) * 3),
        name="ffn",
    )(h2, x_all, w1, w3, w2, modc)


def _row_copy(src_ref, dst_ref, src_row, dst_row, sem):
    return pltpu.make_async_copy(src_ref.at[pl.ds(src_row, 1), :], dst_ref.at[pl.ds(dst_row, 1), :], sem)


def _gather_start(idx_ref, idx_base, n_rows, src_ref, dst_ref, sem):
    def issue(i, carry):
        _row_copy(src_ref, dst_ref, idx_ref[idx_base + i], i, sem).start()
        return carry

    lax.fori_loop(0, n_rows, issue, 0, unroll=8)


def _gather_wait(n_rows, src_ref, dst_ref, sem):
    def drain(i, carry):
        _row_copy(src_ref, dst_ref, 0, i, sem).wait()
        return carry

    lax.fori_loop(0, n_rows, drain, 0, unroll=8)


def _dispatch_kernel(idx_ref, src_ref, o_ref, sem):
    _gather_start(idx_ref, pl.program_id(0) * MOE_TILE, MOE_TILE, src_ref, o_ref, sem)
    _gather_wait(MOE_TILE, src_ref, o_ref, sem)


def _dispatch(h_rows, slot_token):
    n_slots = slot_token.shape[0]
    d = h_rows.shape[1]
    grid_spec = pltpu.PrefetchScalarGridSpec(
        num_scalar_prefetch=1, grid=(n_slots // MOE_TILE,),
        in_specs=[pl.BlockSpec(memory_space=pl.ANY)],
        out_specs=pl.BlockSpec((MOE_TILE, d), lambda t, idx: (t, 0)),
        scratch_shapes=[pltpu.SemaphoreType.DMA])
    return pl.pallas_call(
        _dispatch_kernel,
        grid_spec=grid_spec,
        out_shape=jax.ShapeDtypeStruct((n_slots, d), F32),
        compiler_params=_params(("arbitrary",)),
        name="moe_dispatch",
    )(slot_token, h_rows)


def _expert_kernel(te_ref, tv_ref, xg_ref, w1_ref, w3_ref, w2_ref, y_ref, h_ref, acc_ref, *, n_f):
    del te_ref
    t, f = pl.program_id(0), pl.program_id(1)

    @pl.when(tv_ref[t] != 0)
    def _():
        @pl.when(f == 0)
        def _():
            h_ref[...] = xg_ref[...].astype(BF16)
            acc_ref[...] = jnp.zeros(acc_ref.shape, F32)

        acc_ref[...] += _swiglu_partial(h_ref[...], w1_ref[0, 0], w3_ref[0, 0], w2_ref[0, 0])

        @pl.when(f == n_f - 1)
        def _():
            y_ref[...] = acc_ref[...]

    @pl.when((tv_ref[t] == 0) & (f == n_f - 1))
    def _():
        y_ref[...] = jnp.zeros(y_ref.shape, F32)


def _experts(xg, tile_expert, tile_valid, w1, w3, w2, li):
    n_slots = xg.shape[0]
    d, f = w1.shape[-2:]
    tf = _ff_tile(f, 896)
    n_f = f // tf
    grid_spec = pltpu.PrefetchScalarGridSpec(
        num_scalar_prefetch=2, grid=(n_slots // MOE_TILE, n_f),
        in_specs=[pl.BlockSpec((MOE_TILE, d), lambda t, j, te, tv: (t, 0)),
                  pl.BlockSpec((1, 1, d, tf), lambda t, j, te, tv: (li, te[t], 0, j)),
                  pl.BlockSpec((1, 1, d, tf), lambda t, j, te, tv: (li, te[t], 0, j)),
                  pl.BlockSpec((1, 1, tf, d), lambda t, j, te, tv: (li, te[t], j, 0))],
        out_specs=pl.BlockSpec((MOE_TILE, d), lambda t, j, te, tv: (t, 0)),
        scratch_shapes=[pltpu.VMEM((MOE_TILE, d), BF16), pltpu.VMEM((MOE_TILE, d), F32)])
    return pl.pallas_call(
        functools.partial(_expert_kernel, n_f=n_f),
        grid_spec=grid_spec,
        out_shape=jax.ShapeDtypeStruct((n_slots, d), F32),
        compiler_params=_params(("arbitrary", "arbitrary")),
        name="moe_experts",
    )(tile_expert, tile_valid, xg, w1, w3, w2)


def _combine_kernel(pos_ref, y_ref, x_ref, route_ref, mod_ref, o_ref, g1_ref, g2_ref, sems, *, n_rows):
    base = (pl.program_id(0) * pl.num_programs(1) + pl.program_id(1)) * ROW_TILE
    _gather_start(pos_ref, base, ROW_TILE, y_ref, g1_ref, sems.at[0])
    _gather_start(pos_ref, n_rows + base, ROW_TILE, y_ref, g2_ref, sems.at[1])
    _gather_wait(ROW_TILE, y_ref, g1_ref, sems.at[0])
    _gather_wait(ROW_TILE, y_ref, g2_ref, sems.at[1])
    for c in range(ROW_TILE // CHUNK):
        rows = slice(c * CHUNK, (c + 1) * CHUNK)
        w1 = route_ref[0, rows, 2:3]
        w2 = route_ref[0, rows, 3:4]
        y = w1 * g1_ref[rows, :] + w2 * g2_ref[rows, :]
        o_ref[0, rows, :] = x_ref[0, rows, :] + mod_ref[0, c, 5:6, :] * y


def _combine(y, pos, x_mid, route, modc):
    b, n_tok, d = x_mid.shape
    nt = n_tok // ROW_TILE
    tile = lambda w: pl.BlockSpec((1, ROW_TILE, w), lambda bi, i, p: (bi, i, 0))
    grid_spec = pltpu.PrefetchScalarGridSpec(
        num_scalar_prefetch=1, grid=(b, nt),
        in_specs=[pl.BlockSpec(memory_space=pl.ANY), tile(d), tile(LANES),
                  pl.BlockSpec((1, ROW_TILE // CHUNK, 6, d), lambda bi, i, p: (bi, i, 0, 0))],
        out_specs=tile(d),
        scratch_shapes=[pltpu.VMEM((ROW_TILE, d), F32),
                        pltpu.VMEM((ROW_TILE, d), F32),
                        pltpu.SemaphoreType.DMA((2,))])
    return pl.pallas_call(
        functools.partial(_combine_kernel, n_rows=b * n_tok),
        grid_spec=grid_spec,
        out_shape=jax.ShapeDtypeStruct((b, n_tok, d), F32),
        compiler_params=_params(("arbitrary", "arbitrary")),
        name="moe_combine",
    )(pos, y, x_mid, route, modc)


def _routing_tables(route):
    n_rows = route.shape[0] * route.shape[1]
    r2 = route.reshape(n_rows, LANES)
    e_flat = jnp.concatenate([r2[:, 0], r2[:, 1]]).astype(jnp.int32)
    onehot = (e_flat[:, None] == jnp.arange(N_EXPERTS, dtype=jnp.int32)[None, :]).astype(jnp.int32)
    csum = jnp.cumsum(onehot, axis=0)
    rank = jnp.take_along_axis(csum, e_flat[:, None], axis=1)[:, 0] - 1
    padded = ((csum[-1] + MOE_TILE - 1) // MOE_TILE) * MOE_TILE
    ends = jnp.cumsum(padded)
    pos = (ends - padded)[e_flat] + rank
    n_slots = 2 * n_rows + N_EXPERTS * MOE_TILE
    token = jnp.arange(2 * n_rows, dtype=jnp.int32) % n_rows
    slot_token = jnp.zeros((n_slots,), jnp.int32).at[pos].set(token)
    tile_start = jnp.arange(n_slots // MOE_TILE, dtype=jnp.int32) * MOE_TILE
    tile_expert = jnp.minimum(jnp.searchsorted(ends, tile_start, side="right"), N_EXPERTS - 1)
    tile_valid = (tile_start < ends[-1]).astype(jnp.int32)
    return slot_token, tile_expert.astype(jnp.int32), tile_valid, pos.astype(jnp.int32)


def _moe(h_rows, x_mid, route, w1, w3, w2, li, modc):
    b, n_tok = x_mid.shape[:2]
    slot_token, tile_expert, tile_valid, pos = _routing_tables(route)
    xg = _dispatch(h_rows.reshape(b * n_tok, -1), slot_token)
    y = _experts(xg, tile_expert, tile_valid, w1, w3, w2, li)
    return _combine(y, pos, x_mid, route, modc)


def _final_kernel(x_ref, g_ref, o_ref):
    x = x_ref[0]
    ms = jnp.mean(x * x, axis=-1, keepdims=True)
    o_ref[0] = x * lax.rsqrt(ms + EPS) * g_ref[...]


def _final_norm(x_all, g, n_lat):
    b, _, d = x_all.shape
    tm = 1024
    return pl.pallas_call(
        _final_kernel,
        grid=(b, n_lat // tm),
        in_specs=[pl.BlockSpec((1, tm, d), lambda bi, i: (bi, i, 0)),
                  pl.BlockSpec((1, d), lambda bi, i: (0, 0))],
        out_specs=pl.BlockSpec((1, tm, d), lambda bi, i: (bi, i, 0)),
        out_shape=jax.ShapeDtypeStruct((b, n_lat, d), F32),
        compiler_params=_params(("arbitrary", "arbitrary")),
        name="final_norm",
    )(x_all, g)


def _deinterleave(w, n_heads):
    lead = w.shape[:-1]
    w = w.reshape(lead + (n_heads, HALF, 2))
    return jnp.swapaxes(w, -1, -2).reshape(lead + (n_heads * HEAD_DIM,))


def _rope_tables(n_lat, n_tok):
    t = np.arange(n_lat)
    inv = ROPE_THETA ** (-np.arange(HEAD_DIM // 4, dtype=np.float64) / (HEAD_DIM // 4))
    ang = np.concatenate([(t // GRID_W)[:, None] * inv, (t % GRID_W)[:, None] * inv], axis=-1)
    ang = np.concatenate([ang, np.zeros((n_tok - n_lat, HALF))], axis=0)
    return jnp.asarray(np.cos(ang).T, F32), jnp.asarray(np.sin(ang).T, F32)


def kernel(x, c, ctx, c_ctx, w_mod, b_mod, g_mix, g_ffn, w_in, qn_a, kn_a, lam_q1, lam_k1, lam_q2,
           lam_k2, subln_b, rpb_c, w_br_a, w_br_b, w_br_c, w_out, ffn_w1, ffn_w3, ffn_w2, router,
           moe_w1, moe_w3, moe_w2, g_final):
    b, n_lat, d = x.shape
    n_ctx = ctx.shape[1]
    n_tok = n_lat + n_ctx
    depth = w_in.shape[0]
    assert d == D_MODEL and n_ctx == CHUNK and b + 1 <= 8
    assert n_tok % ROW_TILE == 0 and n_tok % KV_TILE == 0 and n_lat % Q_TILE == 0
    assert n_lat % (NBR_ROWS * GRID_W) == 0 and n_lat // GRID_W >= 24
    n_chunks = n_tok // CHUNK

    x_all = jnp.concatenate([x, ctx], axis=1)
    cos_t, sin_t = _rope_tables(n_lat, n_tok)

    c_rows = jnp.concatenate([c, c_ctx[None], jnp.zeros((7 - b, d), F32)], axis=0)
    mods = _mod_vectors(c_rows, w_mod, b_mod).reshape(depth, 8, 6, d)
    mod_lat = jnp.broadcast_to(mods[:, :b, None], (depth, b, n_chunks - 1, 6, d))
    mod_ctx = jnp.broadcast_to(mods[:, b:b + 1, None], (depth, b, 1, 6, d))
    mod_chunks = jnp.concatenate([mod_lat, mod_ctx], axis=2)

    sl = lambda name: w_in[:, :, _OFF[name][0]:_OFF[name][1]]
    w_n = jnp.concatenate([sl("gate"), sl("cq"), sl("ck"), sl("cv")], axis=-1).astype(BF16)
    w_t = jnp.concatenate([_deinterleave(sl("aq"), 8), _deinterleave(sl("bq"), 8), sl("bv"),
                           _deinterleave(sl("bk"), 8), sl("av"), _deinterleave(sl("ak"), 2)], axis=-1)
    w_t = jnp.swapaxes(w_t, 1, 2).astype(BF16)
    qn_t = _deinterleave(qn_a, 1)[:, :, None]
    kn_t = _deinterleave(kn_a, 1)[:, :, None]
    lam_vecs = jnp.pad(jnp.stack([lam_q1, lam_k1, lam_q2, lam_k2], axis=1),
                       ((0, 0), (0, 4), (0, LANES - HEAD_DIM)))
    wa, wb, wc, wo = (w.astype(BF16) for w in (w_br_a, w_br_b, w_br_c, w_out))
    f1, f3, f2 = (w.astype(BF16) for w in (ffn_w1, ffn_w3, ffn_w2))
    m1, m3, m2 = (w.astype(BF16) for w in (moe_w1, moe_w3, moe_w2))
    router_p = jnp.pad(router, ((0, 0), (0, 0), (0, LANES - N_EXPERTS)))
    nbr_idx, nbr_lo, nbr_hi = _nbr_tables(n_lat // GRID_W)
    nbr_idx = jnp.asarray(nbr_idx)

    for layer in range(depth):
        lam_init = 0.8 - 0.6 * math.exp(-0.3 * layer)
        modc = mod_chunks[layer]
        g_mix_l = g_mix[layer][None]
        proj_n = _inproj(x_all, g_mix_l, modc, w_n[layer], transposed=False)
        proj_t = _inproj(x_all, g_mix_l, modc, w_t[layer], transposed=True)
        k_b, k_a = _kprep(proj_t, cos_t, sin_t, kn_t[layer])

        o_a = _attention("A", proj_t, k_a, cos_t, sin_t, [qn_t[layer]], lam_init, n_lat)
        o_a = _attention("A", proj_t, k_a, cos_t, sin_t, [qn_t[layer]], lam_init, n_lat, ctx_into=o_a)
        aux_b = [lam_vecs[layer], subln_b[layer][:, None]]
        o_b = _attention("B", proj_t, k_b, cos_t, sin_t, aux_b, lam_init, n_lat)
        o_b = _attention("B", proj_t, k_b, cos_t, sin_t, aux_b, lam_init, n_lat, ctx_into=o_b)
        tp = _nbr_bias_table(rpb_c[layer], nbr_lo, nbr_hi)
        o_c = _nbr_attention(proj_n, nbr_idx, tp, n_lat)
        o_c = _ctx_c_attention(proj_n, o_c, n_lat)

        is_moe = layer % 2 == 1
        li = layer // 2
        merged = _merge(o_a, o_b, o_c, proj_n, x_all, wa[layer], wb[layer], wc[layer], wo[layer],
                        modc, g_ffn[layer][None], router_p[li] if is_moe else None)
        if is_moe:
            x_mid, h_rows, route = merged
            x_all = _moe(h_rows, x_mid, route, m1, m3, m2, li, modc)
        else:
            x_mid, h2 = merged
            x_all = _ffn(h2, x_mid, f1, f3, f2, li, modc)

    return _final_norm(x_all, g_final[None], n_lat)
```

```python
import functools
import math

import numpy as np
import jax
import jax.numpy as jnp
from jax import lax
from jax.experimental import pallas as pl
from jax.experimental.pallas import tpu as pltpu

F32 = jnp.float32
BF16 = jnp.bfloat16
HIGHEST = lax.Precision.HIGHEST

D_MODEL = 1024
HEAD_DIM = 64
HALF = HEAD_DIM // 2
GRID_W = 64
ROPE_THETA = 10000.0
EPS = 1e-6
N_HEADS = 8
A_GROUP = 4
B_HEADS = 4
B_V_DIM = 128
NA_KH = 8
NA_KW = 16
N_EXPERTS = 8
LANES = 128
CHUNK = 256
ROW_TILE = 768
Q_TILE = 1024
KV_TILE = 768
MOE_TILE = 512
DMA_PRIORITIES = 2
NBR_ROWS = 8
BAND_ROWS = 16
NEG = -1e30
SM_SCALE = HEAD_DIM ** -0.5
LOG2E = math.log2(math.e)
SUM_ROWS = 16
STAB_MARGIN = 64.0
STAB_KEYS = 128
VMEM_LIMIT = 56 * 1024 * 1024

_OFF = {}
_o = 0
for _name, _size in (("aq", 512), ("ak", 128), ("av", 128), ("bq", 512), ("bk", 512),
                     ("bv", 512), ("cq", 512), ("ck", 512), ("cv", 512), ("gate", 3072)):
    _OFF[_name] = (_o, _o + _size)
    _o += _size
D_IN = _o
N_COLS = 4608
T_COLS = 2304


def _params(sem):
    return pltpu.CompilerParams(dimension_semantics=sem, vmem_limit_bytes=VMEM_LIMIT)


def _sigmoid(x):
    return 1.0 / (1.0 + jnp.exp(-x))


def _dot(a, b, **kw):
    return jnp.dot(a, b, preferred_element_type=F32, **kw)


def _dot_nt(a, b):
    return lax.dot_general(a, b, (((1,), (1,)), ((), ())), preferred_element_type=F32)


def _norm_mod(x, g, shift, scale):
    ms = jnp.mean(x * x, axis=-1, keepdims=True)
    y = x * lax.rsqrt(ms + EPS) * g
    return y * (1.0 + scale) + shift


def _mod_kernel(c_ref, w_ref, b_ref, o_ref):
    c = c_ref[...]
    s = c * _sigmoid(c)
    o_ref[0] = _dot(s, w_ref[0], precision=HIGHEST) + b_ref[0]


def _mod_vectors(c_rows, w_mod, b_mod):
    depth, d, n = w_mod.shape
    tn = 1536
    return pl.pallas_call(
        _mod_kernel,
        grid=(depth, n // tn),
        in_specs=[pl.BlockSpec((8, d), lambda l, j: (0, 0)),
                  pl.BlockSpec((1, d, tn), lambda l, j: (l, 0, j)),
                  pl.BlockSpec((1, 1, tn), lambda l, j: (l, 0, j))],
        out_specs=pl.BlockSpec((1, 8, tn), lambda l, j: (l, 0, j)),
        out_shape=jax.ShapeDtypeStruct((depth, 8, n), F32),
        compiler_params=_params(("arbitrary", "arbitrary")),
        name="mod_vectors",
    )(c_rows, w_mod, b_mod.reshape(depth, 1, n))


def _inproj_kernel(x_ref, g_ref, mod_ref, w_ref, o_ref, h_ref, *, transposed):
    @pl.when(pl.program_id(2) == 0)
    def _():
        for c in range(ROW_TILE // CHUNK):
            rows = slice(c * CHUNK, (c + 1) * CHUNK)
            h = _norm_mod(x_ref[0, rows, :], g_ref[...], mod_ref[0, c, 0:1, :], mod_ref[0, c, 1:2, :])
            h_ref[rows, :] = h.astype(BF16)

    if transposed:
        o_ref[0] = _dot_nt(w_ref[...], h_ref[...]).astype(BF16)
    else:
        o_ref[0] = _dot(h_ref[...], w_ref[...]).astype(BF16)


def _inproj(x_all, g, modc, w, transposed):
    b, n_tok, d = x_all.shape
    nt = n_tok // ROW_TILE
    if transposed:
        cols, tn = w.shape[0], 768
        w_spec = pl.BlockSpec((tn, d), lambda bi, i, j: (j, 0))
        o_spec = pl.BlockSpec((1, tn, ROW_TILE), lambda bi, i, j: (bi, j, i))
        o_shape = (b, cols, n_tok)
    else:
        cols, tn = w.shape[1], 2304
        w_spec = pl.BlockSpec((d, tn), lambda bi, i, j: (0, j))
        o_spec = pl.BlockSpec((1, ROW_TILE, tn), lambda bi, i, j: (bi, i, j))
        o_shape = (b, n_tok, cols)
    return pl.pallas_call(
        functools.partial(_inproj_kernel, transposed=transposed),
        grid=(b, nt, cols // tn),
        in_specs=[pl.BlockSpec((1, ROW_TILE, d), lambda bi, i, j: (bi, i, 0)),
                  pl.BlockSpec((1, d), lambda bi, i, j: (0, 0)),
                  pl.BlockSpec((1, ROW_TILE // CHUNK, 6, d), lambda bi, i, j: (bi, i, 0, 0)),
                  w_spec],
        out_specs=o_spec,
        out_shape=jax.ShapeDtypeStruct(o_shape, BF16),
        scratch_shapes=[pltpu.VMEM((ROW_TILE, d), BF16)],
        compiler_params=_params(("arbitrary", "arbitrary", "arbitrary")),
        name="inproj_t" if transposed else "inproj_n",
    )(x_all, g, modc, w)


def _rope_t(x, c, s):
    x1, x2 = x[:HALF], x[HALF:]
    return jnp.concatenate([x1 * c - x2 * s, x1 * s + x2 * c], axis=0)


def _rmsnorm_t(x, g):
    ms = jnp.mean(x * x, axis=0, keepdims=True)
    return x * lax.rsqrt(ms + EPS) * g


def _kprep_kernel(kb_ref, ka_ref, cos_ref, sin_ref, kn_ref, ob_ref, oa_ref):
    c, s = cos_ref[...], sin_ref[...]
    kn = kn_ref[...]
    outs = []
    for h in range(N_HEADS):
        x = kb_ref[0, HEAD_DIM * h:HEAD_DIM * (h + 1), :].astype(F32)
        outs.append(_rope_t(x, c, s))
    ob_ref[0] = jnp.concatenate(outs, axis=0).T.astype(BF16)
    outs = []
    for h in range(N_HEADS // A_GROUP):
        x = ka_ref[0, HEAD_DIM * h:HEAD_DIM * (h + 1), :].astype(F32)
        outs.append(_rope_t(_rmsnorm_t(x, kn), c, s))
    oa_ref[0] = jnp.concatenate(outs, axis=0).T.astype(BF16)


def _kprep(proj_t, cos_t, sin_t, kn):
    b, _, n_tok = proj_t.shape
    nt = n_tok // ROW_TILE
    return pl.pallas_call(
        _kprep_kernel,
        grid=(b, nt),
        in_specs=[pl.BlockSpec((1, 512, ROW_TILE), lambda bi, i: (bi, 3, i)),
                  pl.BlockSpec((1, 128, ROW_TILE), lambda bi, i: (bi, 17, i)),
                  pl.BlockSpec((HALF, ROW_TILE), lambda bi, i: (0, i)),
                  pl.BlockSpec((HALF, ROW_TILE), lambda bi, i: (0, i)),
                  pl.BlockSpec((HEAD_DIM, 1), lambda bi, i: (0, 0))],
        out_specs=[pl.BlockSpec((1, ROW_TILE, 512), lambda bi, i: (bi, i, 0)),
                   pl.BlockSpec((1, ROW_TILE, 128), lambda bi, i: (bi, i, 0))],
        out_shape=[jax.ShapeDtypeStruct((b, n_tok, 512), BF16),
                   jax.ShapeDtypeStruct((b, n_tok, 128), BF16)],
        compiler_params=_params(("arbitrary", "arbitrary")),
        name="kprep",
    )(proj_t, proj_t, cos_t, sin_t, kn)


def _attn_kernel(*refs, nkv, lam_init, aliased):
    qa_ref, ka_ref, va_ref, qb_ref, kb_ref, vb_ref, cos_ref, sin_ref, qn_ref, lamv_ref, sub_ref = refs[:11]
    rest = refs[13:] if aliased else refs[11:]
    oa_ref, ob_ref = rest[:2]
    scratch_a, scratch_b = rest[2:6], rest[6:10]
    _attn_body("A", qa_ref, ka_ref, va_ref, cos_ref, sin_ref, (qn_ref,), oa_ref, *scratch_a,
               nkv=nkv, lam_init=lam_init)
    _attn_body("B", qb_ref, kb_ref, vb_ref, cos_ref, sin_ref, (lamv_ref, sub_ref), ob_ref, *scratch_b,
               nkv=nkv, lam_init=lam_init)


def _attn_body(mode, q_ref, k_ref, v_ref, cos_ref, sin_ref, aux, o_ref, qz_ref, m_ref, acc_ref, redo_ref,
               *, nkv, lam_init):
    if mode == "A":
        qn_ref, = aux
    else:
        lamv_ref, sub_ref = aux
    kv = pl.program_id(2)
    dv = HEAD_DIM if mode == "A" else B_V_DIM

    def scores(h, rows=slice(None)):
        if mode == "A":
            kblk = k_ref[0, rows, :]
        else:
            kblk = k_ref[0, rows, LANES * (h // 2):LANES * (h // 2 + 1)]
        return _dot(kblk, qz_ref[h])

    @pl.when(kv == 0)
    def _init():
        c, s = cos_ref[...], sin_ref[...]
        for h in range(N_HEADS):
            x = q_ref[0, HEAD_DIM * h:HEAD_DIM * (h + 1), :].astype(F32)
            if mode == "A":
                x = _rmsnorm_t(x, qn_ref[...])
            xb = (_rope_t(x, c, s) * (SM_SCALE * LOG2E)).astype(BF16)
            z = jnp.zeros_like(xb)
            half = (h // A_GROUP) if mode == "A" else (h % 2)
            qz_ref[h] = jnp.concatenate([xb, z] if half == 0 else [z, xb], axis=0)
            m_ref[h:h + 1, :] = jnp.max(scores(h, slice(0, STAB_KEYS)), axis=0, keepdims=True)
        acc_ref[0] = jnp.zeros(acc_ref.shape[1:], F32)

    acc_in = acc_ref.at[kv % 2]
    acc_out = acc_ref.at[(kv + 1) % 2]
    ones = jnp.ones((SUM_ROWS, k_ref.shape[1]), BF16)

    def values(h):
        if mode == "A":
            g = h // A_GROUP
            vt = v_ref[0, HEAD_DIM * g:HEAD_DIM * (g + 1), :]
        else:
            hh = h % B_HEADS
            vt = v_ref[0, B_V_DIM * hh:B_V_DIM * (hh + 1), :]
        return jnp.concatenate([vt, ones], axis=0)

    excess = None
    for h in range(N_HEADS):
        s = scores(h)
        m = m_ref[h:h + 1, :]
        d = jnp.max(s, axis=0, keepdims=True) - m
        excess = d if excess is None else jnp.maximum(excess, d)
        p = jnp.exp2((s - m).astype(BF16))
        acc_out[h] = acc_in[h] + _dot(values(h), p)
    redo_ref[0] = (jnp.max(excess) > STAB_MARGIN).astype(jnp.int32)

    @pl.when(redo_ref[0] != 0)
    def _exact_pass():
        for h in range(N_HEADS):
            s = scores(h)
            m_prev = m_ref[h:h + 1, :]
            m_new = jnp.maximum(m_prev, jnp.max(s, axis=0, keepdims=True))
            alpha = jnp.exp2(m_prev - m_new)
            p = jnp.exp2((s - m_new).astype(BF16))
            acc_out[h] = alpha * acc_in[h] + _dot(values(h), p)
            m_ref[h:h + 1, :] = m_new

    @pl.when(kv == nkv - 1)
    def _fin():
        acc = acc_ref.at[nkv % 2]

        def normalized(h):
            return acc[h, 0:dv, :] * (1.0 / acc[h, dv:dv + 1, :])

        outs = []
        if mode == "A":
            for h in range(N_HEADS):
                outs.append(normalized(h))
        else:
            lv = lamv_ref[...]
            lam = (jnp.exp(jnp.sum(lv[0:1] * lv[1:2], axis=1, keepdims=True))
                   - jnp.exp(jnp.sum(lv[2:3] * lv[3:4], axis=1, keepdims=True)) + lam_init)
            for hh in range(B_HEADS):
                o = normalized(hh) - lam * normalized(hh + B_HEADS)
                outs.append(_rmsnorm_t(o, sub_ref[...]) * (1.0 - lam_init))
        o_ref[0] = jnp.concatenate(outs, axis=0).T.astype(BF16)


def _attention(proj_t, k_a, k_b, cos_t, sin_t, qn, lam_vecs, subln, lam_init, n_lat, ctx_into=None):
    b, _, n_tok = proj_t.shape
    is_ctx = ctx_into is not None
    if is_ctx:
        tq = tk = n_tok - n_lat
        nq, nkv = 1, 1
        q_off = k_off = n_lat // tq
    else:
        tq, tk = Q_TILE, KV_TILE
        nq, nkv = n_lat // tq, n_tok // tk
        q_off = k_off = 0
    q_spec = lambda blk: pl.BlockSpec((1, 512, tq), lambda bi, i, j: (bi, blk, i + q_off))
    k_spec = lambda w: pl.BlockSpec((1, tk, w), lambda bi, i, j: (bi, j + k_off, 0))
    v_spec = lambda w, blk: pl.BlockSpec((1, w, tk), lambda bi, i, j: (bi, blk, j + k_off))
    rope_spec = pl.BlockSpec((HALF, tq), lambda bi, i, j: (0, i + q_off))
    full = lambda a: pl.BlockSpec(a.shape, lambda bi, i, j: (0, 0))
    in_specs = [q_spec(0), k_spec(128), v_spec(128, 16), q_spec(1), k_spec(512), v_spec(512, 2),
                rope_spec, rope_spec, full(qn), full(lam_vecs), full(subln)]
    args = [proj_t, k_a, proj_t, proj_t, k_b, proj_t, cos_t, sin_t, qn, lam_vecs, subln]
    aliases = {}
    if is_ctx:
        in_specs += [pl.BlockSpec(memory_space=pl.ANY)] * 2
        args += list(ctx_into)
        aliases = {len(args) - 2: 0, len(args) - 1: 1}
    o_spec = pl.BlockSpec((1, tq, 512), lambda bi, i, j: (bi, i + q_off, 0))
    o_shape = jax.ShapeDtypeStruct((b, n_tok, 512), BF16)
    scratch = []
    for dv in (HEAD_DIM, B_V_DIM):
        scratch += [pltpu.VMEM((N_HEADS, LANES, tq), BF16),
                    pltpu.VMEM((N_HEADS, tq), F32),
                    pltpu.VMEM((2, N_HEADS, dv + SUM_ROWS, tq), F32),
                    pltpu.SMEM((1,), jnp.int32)]
    return pl.pallas_call(
        functools.partial(_attn_kernel, nkv=nkv, lam_init=lam_init, aliased=is_ctx),
        grid=(b, nq, nkv),
        in_specs=in_specs,
        out_specs=[o_spec, o_spec],
        out_shape=[o_shape, o_shape],
        scratch_shapes=scratch,
        input_output_aliases=aliases,
        compiler_params=_params(("arbitrary", "arbitrary", "arbitrary")),
        name="attn_ctx" if is_ctx else "attn_lat",
    )(*args)


def _nbr_tables(rows):
    nblk = rows // NBR_ROWS
    combos, index = {}, np.zeros((3, NBR_ROWS, BAND_ROWS // 2), np.int32)
    for v, blk in enumerate((0, 1, nblk - 1)):
        bs = min(max(NBR_ROWS * blk - NA_KH // 2, 0), rows - BAND_ROWS)
        for qr in range(NBR_ROWS):
            r = NBR_ROWS * blk + qr
            rs = min(max(r - NA_KH // 2, 0), rows - NA_KH)
            for kp in range(BAND_ROWS // 2):
                codes = []
                for kr in (bs + 2 * kp, bs + 2 * kp + 1):
                    codes.append(kr - r + NA_KH - 1 if rs <= kr < rs + NA_KH else 15)
                index[v, qr, kp] = combos.setdefault(tuple(codes), len(combos))
    lo = np.array([c[0] for c in combos], np.int32)
    hi = np.array([c[1] for c in combos], np.int32)
    return index.reshape(-1), lo, hi


def _nbr_bias_table(rpb, lo, hi):
    nh = rpb.shape[0]
    qc = np.arange(GRID_W)[:, None]
    kc = np.arange(GRID_W)[None, :]
    cs = np.clip(qc - NA_KW // 2, 0, GRID_W - NA_KW)
    inside = (kc >= cs) & (kc < cs + NA_KW)
    dc = np.clip(kc - qc + NA_KW - 1, 0, 2 * NA_KW - 2)
    onehot = (dc.reshape(-1)[None, :] == np.arange(2 * NA_KW - 1)[:, None]).astype(np.float32)
    toep = jnp.einsum("hrc,cq->hrq", rpb, jnp.asarray(onehot), precision=HIGHEST)
    toep = toep.reshape(nh, 2 * NA_KH - 1, GRID_W, GRID_W)
    toep = jnp.where(jnp.asarray(inside)[None, None], toep, NEG)
    toep = jnp.concatenate([toep, jnp.full((nh, 1, GRID_W, GRID_W), NEG, F32)], axis=1)
    return jnp.concatenate([toep[:, lo], toep[:, hi]], axis=-1)


def _pair_heads(q, lane):
    zero = jnp.zeros_like(q)
    return jnp.where(lane < HEAD_DIM, q, zero), jnp.where(lane >= HEAD_DIM, q, zero)


def _nbr_kernel(idx_ref, q_ref, k0, k1, k2, k3, v0, v1, v2, v3, kc_ref, vc_ref, tp_ref,
                o_ref, s_ref, *, nblk):
    blk = pl.program_id(1)
    variant = jnp.where(blk == 0, 0, jnp.where(blk == nblk - 1, 2, 1))
    lane = lax.broadcasted_iota(jnp.int32, (1, LANES), 1)
    outs = []
    for p in range(N_HEADS // 2):
        cols = slice(LANES * p, LANES * (p + 1))
        kband = jnp.concatenate([k0[0, :, cols], k1[0, :, cols], k2[0, :, cols], k3[0, :, cols]], axis=0)
        vband = jnp.concatenate([v0[0, :, cols], v1[0, :, cols], v2[0, :, cols], v3[0, :, cols]], axis=0)
        kctx, vctx = kc_ref[0, :, cols], vc_ref[0, :, cols]
        qp = q_ref[0, :, cols] * jnp.asarray(SM_SCALE, BF16)
        res = []
        for e, qm in enumerate(_pair_heads(qp, lane)):
            h = 2 * p + e
            s_raw = _dot_nt(qm, kband)
            sc = _dot_nt(qm, kctx)
            for qr in range(NBR_ROWS):
                rws = slice(GRID_W * qr, GRID_W * (qr + 1))
                for kp in range(BAND_ROWS // 2):
                    u = idx_ref[variant * (NBR_ROWS * BAND_ROWS // 2) + qr * (BAND_ROWS // 2) + kp]
                    cl = slice(LANES * kp, LANES * (kp + 1))
                    s_ref[rws, cl] = s_raw[rws, cl] + tp_ref[h, u]
            s = s_ref[...]
            m = jnp.maximum(jnp.max(s, axis=-1, keepdims=True), jnp.max(sc, axis=-1, keepdims=True))
            pn = jnp.exp(s - m)
            pc = jnp.exp(sc - m)
            l = jnp.sum(pn, axis=-1, keepdims=True) + jnp.sum(pc, axis=-1, keepdims=True)
            o = _dot(pn.astype(BF16), vband) + _dot(pc.astype(BF16), vctx)
            res.append(o * (1.0 / l))
        outs.append(jnp.where(lane < HEAD_DIM, res[0], res[1]))
    o_ref[0] = jnp.concatenate(outs, axis=1).astype(BF16)


def _nbr_attention(proj_n, idx, tp, n_lat):
    b, n_tok, _ = proj_n.shape
    rows = n_lat // GRID_W
    nblk = rows // NBR_ROWS
    tq = NBR_ROWS * GRID_W
    blk_rows = CHUNK // GRID_W
    n_band = BAND_ROWS // blk_rows
    max_start = rows // blk_rows - n_band
    ctx_blk = n_lat // CHUNK
    q_col, k_col, v_col = 6, 7, 8

    def band_spec(j, col):
        def imap(bi, i, idx_ref):
            start = jnp.clip(2 * i - 1, 0, max_start)
            return (bi, start + j, col)
        return pl.BlockSpec((1, CHUNK, 512), imap)

    in_specs = [pl.BlockSpec((1, tq, 512), lambda bi, i, idx_ref: (bi, i, q_col))]
    in_specs += [band_spec(j, k_col) for j in range(n_band)]
    in_specs += [band_spec(j, v_col) for j in range(n_band)]
    in_specs += [pl.BlockSpec((1, CHUNK, 512), lambda bi, i, idx_ref: (bi, ctx_blk, k_col)),
                 pl.BlockSpec((1, CHUNK, 512), lambda bi, i, idx_ref: (bi, ctx_blk, v_col)),
                 pl.BlockSpec(tp.shape, lambda bi, i, idx_ref: (0, 0, 0, 0))]
    grid_spec = pltpu.PrefetchScalarGridSpec(
        num_scalar_prefetch=1, grid=(b, nblk), in_specs=in_specs,
        out_specs=pl.BlockSpec((1, tq, 512), lambda bi, i, idx_ref: (bi, i, 0)),
        scratch_shapes=[pltpu.VMEM((tq, BAND_ROWS * GRID_W), F32)])
    return pl.pallas_call(
        functools.partial(_nbr_kernel, nblk=nblk),
        grid_spec=grid_spec,
        out_shape=jax.ShapeDtypeStruct((b, n_tok, 512), BF16),
        compiler_params=_params(("arbitrary", "arbitrary")),
        name="nbr_attn",
    )(idx, *([proj_n] * (1 + 2 * n_band + 2)), tp)


def _ctx_c_kernel(q_ref, k_ref, v_ref, prev_ref, o_ref):
    del prev_ref
    lane = lax.broadcasted_iota(jnp.int32, (1, LANES), 1)
    outs = []
    for p in range(N_HEADS // 2):
        cols = slice(LANES * p, LANES * (p + 1))
        kp_, vp = k_ref[0, :, cols], v_ref[0, :, cols]
        qp = q_ref[0, :, cols] * jnp.asarray(SM_SCALE, BF16)
        res = []
        for qm in _pair_heads(qp, lane):
            s = _dot_nt(qm, kp_)
            m = jnp.max(s, axis=-1, keepdims=True)
            pr = jnp.exp(s - m)
            l = jnp.sum(pr, axis=-1, keepdims=True)
            res.append(_dot(pr.astype(BF16), vp) * (1.0 / l))
        outs.append(jnp.where(lane < HEAD_DIM, res[0], res[1]))
    o_ref[0] = jnp.concatenate(outs, axis=1).astype(BF16)


def _ctx_c_attention(proj_n, o_c, n_lat):
    b, n_tok, _ = proj_n.shape
    n_ctx = n_tok - n_lat
    blk = n_lat // n_ctx
    return pl.pallas_call(
        _ctx_c_kernel,
        grid=(b,),
        in_specs=[pl.BlockSpec((1, n_ctx, 512), lambda bi: (bi, blk, 6)),
                  pl.BlockSpec((1, n_ctx, 512), lambda bi: (bi, blk, 7)),
                  pl.BlockSpec((1, n_ctx, 512), lambda bi: (bi, blk, 8)),
                  pl.BlockSpec(memory_space=pl.ANY)],
        out_specs=pl.BlockSpec((1, n_ctx, 512), lambda bi: (bi, blk, 0)),
        out_shape=jax.ShapeDtypeStruct(o_c.shape, BF16),
        input_output_aliases={3: 0},
        compiler_params=_params(("arbitrary",)),
        name="ctx_c_attn",
    )(proj_n, proj_n, proj_n, o_c)


def _merge_kernel(*refs, with_router):
    (oa_ref, ob_ref, oc_ref, gate_ref, x_ref, wa_ref, wb_ref, wc_ref, wo_ref,
     mod_ref, g_ref) = refs[:11]
    if with_router:
        r_ref, xo_ref, h_ref, route_ref = refs[11:]
    else:
        xo_ref, h_ref = refs[11:]
    d = D_MODEL
    gate = lambda k: _sigmoid(gate_ref[0, :, k * d:(k + 1) * d].astype(F32))
    m = (gate(0) * _dot(oa_ref[0], wa_ref[...])
         + gate(1) * _dot(ob_ref[0], wb_ref[...])
         + gate(2) * _dot(oc_ref[0], wc_ref[...]))
    y = _dot(m.astype(BF16), wo_ref[...])
    for c in range(ROW_TILE // CHUNK):
        rows = slice(c * CHUNK, (c + 1) * CHUNK)
        xn = x_ref[0, rows, :] + mod_ref[0, c, 2:3, :] * y[rows, :]
        xo_ref[0, rows, :] = xn
        h = _norm_mod(xn, g_ref[...], mod_ref[0, c, 3:4, :], mod_ref[0, c, 4:5, :])
        if not with_router:
            h_ref[0, rows, :] = h.astype(BF16)
        else:
            h_ref[0, rows, :] = h
            lane = lax.broadcasted_iota(jnp.int32, (CHUNK, LANES), 1)
            logits = jnp.where(lane < N_EXPERTS, _dot(h, r_ref[...], precision=HIGHEST), NEG)
            m1 = jnp.max(logits, axis=-1, keepdims=True)
            i1 = jnp.min(jnp.where(logits == m1, lane, LANES), axis=-1, keepdims=True)
            rest = jnp.where(lane == i1, NEG, logits)
            m2 = jnp.max(rest, axis=-1, keepdims=True)
            i2 = jnp.min(jnp.where(rest == m2, lane, LANES), axis=-1, keepdims=True)
            e = jnp.exp(m2 - m1)
            w1 = 1.0 / (1.0 + e)
            route_ref[0, rows, :] = jnp.where(
                lane == 0, i1.astype(F32), jnp.where(
                    lane == 1, i2.astype(F32), jnp.where(
                        lane == 2, w1, jnp.where(lane == 3, e * w1, 0.0))))


def _merge(o_a, o_b, o_c, proj_n, x_all, wa, wb, wc, wo, modc, g_ffn, router):
    b, n_tok, d = x_all.shape
    nt = n_tok // ROW_TILE
    with_router = router is not None
    tile = lambda w: pl.BlockSpec((1, ROW_TILE, w), lambda bi, i: (bi, i, 0))
    full = lambda a: pl.BlockSpec(a.shape, lambda bi, i: (0,) * a.ndim)
    in_specs = [tile(512), tile(512), tile(512), tile(3 * d), tile(d),
                full(wa), full(wb), full(wc), full(wo),
                pl.BlockSpec((1, ROW_TILE // CHUNK, 6, d), lambda bi, i: (bi, i, 0, 0)),
                full(g_ffn)]
    args = [o_a, o_b, o_c, proj_n, x_all, wa, wb, wc, wo, modc, g_ffn]
    out_specs = [tile(d), tile(d)]
    out_shape = [jax.ShapeDtypeStruct((b, n_tok, d), F32), jax.ShapeDtypeStruct((b, n_tok, d), BF16)]
    if with_router:
        in_specs.append(full(router))
        args.append(router)
        out_shape[1] = jax.ShapeDtypeStruct((b, n_tok, d), F32)
        out_specs.append(tile(LANES))
        out_shape.append(jax.ShapeDtypeStruct((b, n_tok, LANES), F32))
    return pl.pallas_call(
        functools.partial(_merge_kernel, with_router=with_router),
        grid=(b, nt),
        in_specs=in_specs,
        out_specs=out_specs,
        out_shape=out_shape,
        compiler_params=_params(("arbitrary", "arbitrary")),
        name="merge_router" if with_router else "merge",
    )(*args)


def _swiglu_partial(h, w1, w3, w2):
    a = _dot(h, w1)
    t = (a * _sigmoid(a)) * _dot(h, w3)
    return _dot(t.astype(BF16), w2)


def _ffn_kernel(h_ref, x_ref, w1_ref, w3_ref, w2_ref, mod_ref, o_ref, acc_ref, *, n_f):
    @pl.when(pl.program_id(2) == 0)
    def _():
        acc_ref[...] = jnp.zeros(acc_ref.shape, F32)

    acc_ref[...] += _swiglu_partial(h_ref[0], w1_ref[0], w3_ref[0], w2_ref[0])

    @pl.when(pl.program_id(2) == n_f - 1)
    def _():
        for c in range(ROW_TILE // CHUNK):
            rows = slice(c * CHUNK, (c + 1) * CHUNK)
            o_ref[0, rows, :] = x_ref[0, rows, :] + mod_ref[0, c, 5:6, :] * acc_ref[rows, :]


def _ff_tile(f, cap):
    return max(t for t in range(LANES, min(f, cap) + 1, LANES) if f % t == 0)


def _ffn(h2, x_all, w1, w3, w2, li, modc):
    b, n_tok, d = x_all.shape
    nt = n_tok // ROW_TILE
    f = w1.shape[-1]
    tf = _ff_tile(f, 1408)
    n_f = f // tf
    tile = lambda w: pl.BlockSpec((1, ROW_TILE, w), lambda bi, i, j: (bi, i, 0))
    w13 = pl.BlockSpec((1, d, tf), lambda bi, i, j: (li, 0, j))
    w2s = pl.BlockSpec((1, tf, d), lambda bi, i, j: (li, j, 0))
    mods = pl.BlockSpec((1, ROW_TILE // CHUNK, 6, d), lambda bi, i, j: (bi, i, 0, 0))
    return pl.pallas_call(
        functools.partial(_ffn_kernel, n_f=n_f),
        grid=(b, nt, n_f),
        in_specs=[tile(d), tile(d), w13, w13, w2s, mods],
        out_specs=tile(d),
        out_shape=jax.ShapeDtypeStruct((b, n_tok, d), F32),
        scratch_shapes=[pltpu.VMEM((ROW_TILE, d), F32)],
        compiler_params=_params(("arbitrary",) * 3),
        name="ffn",
    )(h2, x_all, w1, w3, w2, modc)


def _row_copy(src_ref, dst_ref, src_row, dst_row, sem):
    return pltpu.make_async_copy(src_ref.at[pl.ds(src_row, 1), :], dst_ref.at[pl.ds(dst_row, 1), :], sem)


def _gather_start(idx_ref, idx_base, n_rows, src_ref, dst_ref, sem):
    def issue(j, carry):
        for prio in range(DMA_PRIORITIES):
            i = DMA_PRIORITIES * j + prio
            _row_copy(src_ref, dst_ref, idx_ref[idx_base + i], i, sem).start(priority=prio)
        return carry

    lax.fori_loop(0, n_rows // DMA_PRIORITIES, issue, 0, unroll=4)


def _gather_wait(n_rows, src_ref, dst_ref, sem):
    def drain(i, carry):
        _row_copy(src_ref, dst_ref, 0, i, sem).wait()
        return carry

    lax.fori_loop(0, n_rows, drain, 0, unroll=8)


def _dispatch_kernel(idx_ref, src_ref, o_ref, sem):
    _gather_start(idx_ref, pl.program_id(0) * MOE_TILE, MOE_TILE, src_ref, o_ref, sem)
    _gather_wait(MOE_TILE, src_ref, o_ref, sem)


def _dispatch(h_rows, slot_token):
    n_slots = slot_token.shape[0]
    d = h_rows.shape[1]
    grid_spec = pltpu.PrefetchScalarGridSpec(
        num_scalar_prefetch=1, grid=(n_slots // MOE_TILE,),
        in_specs=[pl.BlockSpec(memory_space=pl.ANY)],
        out_specs=pl.BlockSpec((MOE_TILE, d), lambda t, idx: (t, 0)),
        scratch_shapes=[pltpu.SemaphoreType.DMA])
    return pl.pallas_call(
        _dispatch_kernel,
        grid_spec=grid_spec,
        out_shape=jax.ShapeDtypeStruct((n_slots, d), F32),
        compiler_params=_params(("arbitrary",)),
        name="moe_dispatch",
    )(slot_token, h_rows)


def _expert_kernel(te_ref, tv_ref, xg_ref, w1_ref, w3_ref, w2_ref, y_ref, h_ref, acc_ref, *, n_f):
    del te_ref
    t, f = pl.program_id(0), pl.program_id(1)

    @pl.when(tv_ref[t] != 0)
    def _():
        @pl.when(f == 0)
        def _():
            h_ref[...] = xg_ref[...].astype(BF16)
            acc_ref[...] = jnp.zeros(acc_ref.shape, F32)

        acc_ref[...] += _swiglu_partial(h_ref[...], w1_ref[0, 0], w3_ref[0, 0], w2_ref[0, 0])

        @pl.when(f == n_f - 1)
        def _():
            y_ref[...] = acc_ref[...]

    @pl.when((tv_ref[t] == 0) & (f == n_f - 1))
    def _():
        y_ref[...] = jnp.zeros(y_ref.shape, F32)


def _experts(xg, tile_expert, tile_valid, w1, w3, w2, li):
    n_slots = xg.shape[0]
    d, f = w1.shape[-2:]
    tf = _ff_tile(f, 896)
    n_f = f // tf
    grid_spec = pltpu.PrefetchScalarGridSpec(
        num_scalar_prefetch=2, grid=(n_slots // MOE_TILE, n_f),
        in_specs=[pl.BlockSpec((MOE_TILE, d), lambda t, j, te, tv: (t, 0)),
                  pl.BlockSpec((1, 1, d, tf), lambda t, j, te, tv: (li, te[t], 0, j)),
                  pl.BlockSpec((1, 1, d, tf), lambda t, j, te, tv: (li, te[t], 0, j)),
                  pl.BlockSpec((1, 1, tf, d), lambda t, j, te, tv: (li, te[t], j, 0))],
        out_specs=pl.BlockSpec((MOE_TILE, d), lambda t, j, te, tv: (t, 0)),
        scratch_shapes=[pltpu.VMEM((MOE_TILE, d), BF16), pltpu.VMEM((MOE_TILE, d), F32)])
    return pl.pallas_call(
        functools.partial(_expert_kernel, n_f=n_f),
        grid_spec=grid_spec,
        out_shape=jax.ShapeDtypeStruct((n_slots, d), F32),
        compiler_params=_params(("arbitrary", "arbitrary")),
        name="moe_experts",
    )(tile_expert, tile_valid, xg, w1, w3, w2)


def _combine_kernel(pos_ref, y_ref, x_ref, route_ref, mod_ref, o_ref, g1_ref, g2_ref, sems, *, n_rows):
    base = (pl.program_id(0) * pl.num_programs(1) + pl.program_id(1)) * ROW_TILE
    _gather_start(pos_ref, base, ROW_TILE, y_ref, g1_ref, sems.at[0])
    _gather_start(pos_ref, n_rows + base, ROW_TILE, y_ref, g2_ref, sems.at[1])
    _gather_wait(ROW_TILE, y_ref, g1_ref, sems.at[0])
    _gather_wait(ROW_TILE, y_ref, g2_ref, sems.at[1])
    for c in range(ROW_TILE // CHUNK):
        rows = slice(c * CHUNK, (c + 1) * CHUNK)
        w1 = route_ref[0, rows, 2:3]
        w2 = route_ref[0, rows, 3:4]
        y = w1 * g1_ref[rows, :] + w2 * g2_ref[rows, :]
        o_ref[0, rows, :] = x_ref[0, rows, :] + mod_ref[0, c, 5:6, :] * y


def _combine(y, pos, x_mid, route, modc):
    b, n_tok, d = x_mid.shape
    nt = n_tok // ROW_TILE
    tile = lambda w: pl.BlockSpec((1, ROW_TILE, w), lambda bi, i, p: (bi, i, 0))
    grid_spec = pltpu.PrefetchScalarGridSpec(
        num_scalar_prefetch=1, grid=(b, nt),
        in_specs=[pl.BlockSpec(memory_space=pl.ANY), tile(d), tile(LANES),
                  pl.BlockSpec((1, ROW_TILE // CHUNK, 6, d), lambda bi, i, p: (bi, i, 0, 0))],
        out_specs=tile(d),
        scratch_shapes=[pltpu.VMEM((ROW_TILE, d), F32),
                        pltpu.VMEM((ROW_TILE, d), F32),
                        pltpu.SemaphoreType.DMA((2,))])
    return pl.pallas_call(
        functools.partial(_combine_kernel, n_rows=b * n_tok),
        grid_spec=grid_spec,
        out_shape=jax.ShapeDtypeStruct((b, n_tok, d), F32),
        compiler_params=_params(("arbitrary", "arbitrary")),
        name="moe_combine",
    )(pos, y, x_mid, route, modc)


def _routing_tables(route):
    n_rows = route.shape[0] * route.shape[1]
    r2 = route.reshape(n_rows, LANES)
    e_flat = jnp.concatenate([r2[:, 0], r2[:, 1]]).astype(jnp.int32)
    onehot = (e_flat[:, None] == jnp.arange(N_EXPERTS, dtype=jnp.int32)[None, :]).astype(jnp.int32)
    csum = jnp.cumsum(onehot, axis=0)
    rank = jnp.take_along_axis(csum, e_flat[:, None], axis=1)[:, 0] - 1
    padded = ((csum[-1] + MOE_TILE - 1) // MOE_TILE) * MOE_TILE
    ends = jnp.cumsum(padded)
    pos = (ends - padded)[e_flat] + rank
    n_slots = 2 * n_rows + N_EXPERTS * MOE_TILE
    token = jnp.arange(2 * n_rows, dtype=jnp.int32) % n_rows
    slot_token = jnp.zeros((n_slots,), jnp.int32).at[pos].set(token)
    tile_start = jnp.arange(n_slots // MOE_TILE, dtype=jnp.int32) * MOE_TILE
    tile_expert = jnp.minimum(jnp.searchsorted(ends, tile_start, side="right"), N_EXPERTS - 1)
    tile_valid = (tile_start < ends[-1]).astype(jnp.int32)
    return slot_token, tile_expert.astype(jnp.int32), tile_valid, pos.astype(jnp.int32)


def _moe(h_rows, x_mid, route, w1, w3, w2, li, modc):
    b, n_tok = x_mid.shape[:2]
    slot_token, tile_expert, tile_valid, pos = _routing_tables(route)
    xg = _dispatch(h_rows.reshape(b * n_tok, -1), slot_token)
    y = _experts(xg, tile_expert, tile_valid, w1, w3, w2, li)
    return _combine(y, pos, x_mid, route, modc)


def _final_kernel(x_ref, g_ref, o_ref):
    x = x_ref[0]
    ms = jnp.mean(x * x, axis=-1, keepdims=True)
    o_ref[0] = x * lax.rsqrt(ms + EPS) * g_ref[...]


def _final_norm(x_all, g, n_lat):
    b, _, d = x_all.shape
    tm = 1024
    return pl.pallas_call(
        _final_kernel,
        grid=(b, n_lat // tm),
        in_specs=[pl.BlockSpec((1, tm, d), lambda bi, i: (bi, i, 0)),
                  pl.BlockSpec((1, d), lambda bi, i: (0, 0))],
        out_specs=pl.BlockSpec((1, tm, d), lambda bi, i: (bi, i, 0)),
        out_shape=jax.ShapeDtypeStruct((b, n_lat, d), F32),
        compiler_params=_params(("arbitrary", "arbitrary")),
        name="final_norm",
    )(x_all, g)


def _deinterleave(w, n_heads):
    lead = w.shape[:-1]
    w = w.reshape(lead + (n_heads, HALF, 2))
    return jnp.swapaxes(w, -1, -2).reshape(lead + (n_heads * HEAD_DIM,))


def _rope_tables(n_lat, n_tok):
    t = np.arange(n_lat)
    inv = ROPE_THETA ** (-np.arange(HEAD_DIM // 4, dtype=np.float64) / (HEAD_DIM // 4))
    ang = np.concatenate([(t // GRID_W)[:, None] * inv, (t % GRID_W)[:, None] * inv], axis=-1)
    ang = np.concatenate([ang, np.zeros((n_tok - n_lat, HALF))], axis=0)
    return jnp.asarray(np.cos(ang).T, F32), jnp.asarray(np.sin(ang).T, F32)


def kernel(x, c, ctx, c_ctx, w_mod, b_mod, g_mix, g_ffn, w_in, qn_a, kn_a, lam_q1, lam_k1, lam_q2,
           lam_k2, subln_b, rpb_c, w_br_a, w_br_b, w_br_c, w_out, ffn_w1, ffn_w3, ffn_w2, router,
           moe_w1, moe_w3, moe_w2, g_final):
    b, n_lat, d = x.shape
    n_ctx = ctx.shape[1]
    n_tok = n_lat + n_ctx
    depth = w_in.shape[0]
    assert d == D_MODEL and n_ctx == CHUNK and b + 1 <= 8
    assert n_tok % ROW_TILE == 0 and n_tok % KV_TILE == 0 and n_lat % Q_TILE == 0
    assert n_lat % (NBR_ROWS * GRID_W) == 0 and n_lat // GRID_W >= 24
    n_chunks = n_tok // CHUNK

    x_all = jnp.concatenate([x, ctx], axis=1)
    cos_t, sin_t = _rope_tables(n_lat, n_tok)

    c_rows = jnp.concatenate([c, c_ctx[None], jnp.zeros((7 - b, d), F32)], axis=0)
    mods = _mod_vectors(c_rows, w_mod, b_mod).reshape(depth, 8, 6, d)
    mod_lat = jnp.broadcast_to(mods[:, :b, None], (depth, b, n_chunks - 1, 6, d))
    mod_ctx = jnp.broadcast_to(mods[:, b:b + 1, None], (depth, b, 1, 6, d))
    mod_chunks = jnp.concatenate([mod_lat, mod_ctx], axis=2)

    sl = lambda name: w_in[:, :, _OFF[name][0]:_OFF[name][1]]
    w_n = jnp.concatenate([sl("gate"), sl("cq"), sl("ck"), sl("cv")], axis=-1).astype(BF16)
    w_t = jnp.concatenate([_deinterleave(sl("aq"), 8), _deinterleave(sl("bq"), 8), sl("bv"),
                           _deinterleave(sl("bk"), 8), sl("av"), _deinterleave(sl("ak"), 2)], axis=-1)
    w_t = jnp.swapaxes(w_t, 1, 2).astype(BF16)
    qn_t = _deinterleave(qn_a, 1)[:, :, None]
    kn_t = _deinterleave(kn_a, 1)[:, :, None]
    lam_vecs = jnp.pad(jnp.stack([lam_q1, lam_k1, lam_q2, lam_k2], axis=1),
                       ((0, 0), (0, 4), (0, LANES - HEAD_DIM)))
    wa, wb, wc, wo = (w.astype(BF16) for w in (w_br_a, w_br_b, w_br_c, w_out))
    f1, f3, f2 = (w.astype(BF16) for w in (ffn_w1, ffn_w3, ffn_w2))
    m1, m3, m2 = (w.astype(BF16) for w in (moe_w1, moe_w3, moe_w2))
    router_p = jnp.pad(router, ((0, 0), (0, 0), (0, LANES - N_EXPERTS)))
    nbr_idx, nbr_lo, nbr_hi = _nbr_tables(n_lat // GRID_W)
    nbr_idx = jnp.asarray(nbr_idx)

    for layer in range(depth):
        lam_init = 0.8 - 0.6 * math.exp(-0.3 * layer)
        modc = mod_chunks[layer]
        g_mix_l = g_mix[layer][None]
        proj_n = _inproj(x_all, g_mix_l, modc, w_n[layer], transposed=False)
        proj_t = _inproj(x_all, g_mix_l, modc, w_t[layer], transposed=True)
        k_b, k_a = _kprep(proj_t, cos_t, sin_t, kn_t[layer])

        attn_args = (proj_t, k_a, k_b, cos_t, sin_t, qn_t[layer], lam_vecs[layer], subln_b[layer][:, None],
                     lam_init, n_lat)
        o_a, o_b = _attention(*attn_args, ctx_into=_attention(*attn_args))
        tp = _nbr_bias_table(rpb_c[layer], nbr_lo, nbr_hi)
        o_c = _nbr_attention(proj_n, nbr_idx, tp, n_lat)
        o_c = _ctx_c_attention(proj_n, o_c, n_lat)

        is_moe = layer % 2 == 1
        li = layer // 2
        merged = _merge(o_a, o_b, o_c, proj_n, x_all, wa[layer], wb[layer], wc[layer], wo[layer],
                        modc, g_ffn[layer][None], router_p[li] if is_moe else None)
        if is_moe:
            x_mid, h_rows, route = merged
            x_all = _moe(h_rows, x_mid, route, m1, m3, m2, li, modc)
        else:
            x_mid, h2 = merged
            x_all = _ffn(h2, x_mid, f1, f3, f2, li, modc)

    return _final_norm(x_all, g_final[None], n_lat)
```

```python
import functools
import math

import numpy as np
import jax
import jax.numpy as jnp
from jax import lax
from jax.experimental import pallas as pl
from jax.experimental.pallas import tpu as pltpu

F32 = jnp.float32
BF16 = jnp.bfloat16
HIGHEST = lax.Precision.HIGHEST

D_MODEL = 1024
HEAD_DIM = 64
HALF = HEAD_DIM // 2
GRID_W = 64
ROPE_THETA = 10000.0
EPS = 1e-6
N_HEADS = 8
A_GROUP = 4
B_HEADS = 4
B_V_DIM = 128
NA_KH = 8
NA_KW = 16
N_EXPERTS = 8
LANES = 128
CHUNK = 256
ROW_TILE = 768
Q_TILE = 1024
KV_TILE = 768
MOE_TILE = 512
DMA_PRIORITIES = 2
NBR_ROWS = 8
BAND_ROWS = 16
NEG = -1e30
SM_SCALE = HEAD_DIM ** -0.5
LOG2E = math.log2(math.e)
SUM_ROWS = 16
STAB_MARGIN = 64.0
STAB_KEYS = 128
VMEM_LIMIT = 56 * 1024 * 1024

_OFF = {}
_o = 0
for _name, _size in (("aq", 512), ("ak", 128), ("av", 128), ("bq", 512), ("bk", 512),
                     ("bv", 512), ("cq", 512), ("ck", 512), ("cv", 512), ("gate", 3072)):
    _OFF[_name] = (_o, _o + _size)
    _o += _size
D_IN = _o
N_COLS = 4608
T_COLS = 2304


def _params(sem):
    return pltpu.CompilerParams(dimension_semantics=sem, vmem_limit_bytes=VMEM_LIMIT)


def _sigmoid(x):
    return 1.0 / (1.0 + jnp.exp(-x))


def _dot(a, b, **kw):
    return jnp.dot(a, b, preferred_element_type=F32, **kw)


def _dot_nt(a, b):
    return lax.dot_general(a, b, (((1,), (1,)), ((), ())), preferred_element_type=F32)


def _norm_mod(x, g, shift, scale):
    ms = jnp.mean(x * x, axis=-1, keepdims=True)
    y = x * lax.rsqrt(ms + EPS) * g
    return y * (1.0 + scale) + shift


def _mod_kernel(c_ref, w_ref, b_ref, o_ref):
    c = c_ref[...]
    s = c * _sigmoid(c)
    o_ref[0] = _dot(s, w_ref[0], precision=HIGHEST) + b_ref[0]


def _mod_vectors(c_rows, w_mod, b_mod):
    depth, d, n = w_mod.shape
    tn = 1536
    return pl.pallas_call(
        _mod_kernel,
        grid=(depth, n // tn),
        in_specs=[pl.BlockSpec((8, d), lambda l, j: (0, 0)),
                  pl.BlockSpec((1, d, tn), lambda l, j: (l, 0, j)),
                  pl.BlockSpec((1, 1, tn), lambda l, j: (l, 0, j))],
        out_specs=pl.BlockSpec((1, 8, tn), lambda l, j: (l, 0, j)),
        out_shape=jax.ShapeDtypeStruct((depth, 8, n), F32),
        compiler_params=_params(("arbitrary", "arbitrary")),
        name="mod_vectors",
    )(c_rows, w_mod, b_mod.reshape(depth, 1, n))


def _inproj_kernel(x_ref, g_ref, mod_ref, w_ref, o_ref, h_ref, *, transposed):
    @pl.when(pl.program_id(2) == 0)
    def _():
        for c in range(ROW_TILE // CHUNK):
            rows = slice(c * CHUNK, (c + 1) * CHUNK)
            h = _norm_mod(x_ref[0, rows, :], g_ref[...], mod_ref[0, c, 0:1, :], mod_ref[0, c, 1:2, :])
            h_ref[rows, :] = h.astype(BF16)

    if transposed:
        o_ref[0] = _dot_nt(w_ref[...], h_ref[...]).astype(BF16)
    else:
        o_ref[0] = _dot(h_ref[...], w_ref[...]).astype(BF16)


def _inproj(x_all, g, modc, w, transposed):
    b, n_tok, d = x_all.shape
    nt = n_tok // ROW_TILE
    if transposed:
        cols, tn = w.shape[0], 768
        w_spec = pl.BlockSpec((tn, d), lambda bi, i, j: (j, 0))
        o_spec = pl.BlockSpec((1, tn, ROW_TILE), lambda bi, i, j: (bi, j, i))
        o_shape = (b, cols, n_tok)
    else:
        cols, tn = w.shape[1], 2304
        w_spec = pl.BlockSpec((d, tn), lambda bi, i, j: (0, j))
        o_spec = pl.BlockSpec((1, ROW_TILE, tn), lambda bi, i, j: (bi, i, j))
        o_shape = (b, n_tok, cols)
    return pl.pallas_call(
        functools.partial(_inproj_kernel, transposed=transposed),
        grid=(b, nt, cols // tn),
        in_specs=[pl.BlockSpec((1, ROW_TILE, d), lambda bi, i, j: (bi, i, 0)),
                  pl.BlockSpec((1, d), lambda bi, i, j: (0, 0)),
                  pl.BlockSpec((1, ROW_TILE // CHUNK, 6, d), lambda bi, i, j: (bi, i, 0, 0)),
                  w_spec],
        out_specs=o_spec,
        out_shape=jax.ShapeDtypeStruct(o_shape, BF16),
        scratch_shapes=[pltpu.VMEM((ROW_TILE, d), BF16)],
        compiler_params=_params(("arbitrary", "arbitrary", "arbitrary")),
        name="inproj_t" if transposed else "inproj_n",
    )(x_all, g, modc, w)


def _rope_t(x, c, s):
    x1, x2 = x[:HALF], x[HALF:]
    return jnp.concatenate([x1 * c - x2 * s, x1 * s + x2 * c], axis=0)


def _rmsnorm_t(x, g):
    ms = jnp.mean(x * x, axis=0, keepdims=True)
    return x * lax.rsqrt(ms + EPS) * g


def _kprep_kernel(kb_ref, ka_ref, cos_ref, sin_ref, kn_ref, ob_ref, oa_ref):
    c, s = cos_ref[...], sin_ref[...]
    kn = kn_ref[...]
    outs = []
    for h in range(N_HEADS):
        x = kb_ref[0, HEAD_DIM * h:HEAD_DIM * (h + 1), :].astype(F32)
        outs.append(_rope_t(x, c, s))
    ob_ref[0] = jnp.concatenate(outs, axis=0).T.astype(BF16)
    outs = []
    for h in range(N_HEADS // A_GROUP):
        x = ka_ref[0, HEAD_DIM * h:HEAD_DIM * (h + 1), :].astype(F32)
        outs.append(_rope_t(_rmsnorm_t(x, kn), c, s))
    oa_ref[0] = jnp.concatenate(outs, axis=0).T.astype(BF16)


def _kprep(proj_t, cos_t, sin_t, kn):
    b, _, n_tok = proj_t.shape
    nt = n_tok // ROW_TILE
    return pl.pallas_call(
        _kprep_kernel,
        grid=(b, nt),
        in_specs=[pl.BlockSpec((1, 512, ROW_TILE), lambda bi, i: (bi, 3, i)),
                  pl.BlockSpec((1, 128, ROW_TILE), lambda bi, i: (bi, 17, i)),
                  pl.BlockSpec((HALF, ROW_TILE), lambda bi, i: (0, i)),
                  pl.BlockSpec((HALF, ROW_TILE), lambda bi, i: (0, i)),
                  pl.BlockSpec((HEAD_DIM, 1), lambda bi, i: (0, 0))],
        out_specs=[pl.BlockSpec((1, ROW_TILE, 512), lambda bi, i: (bi, i, 0)),
                   pl.BlockSpec((1, ROW_TILE, 128), lambda bi, i: (bi, i, 0))],
        out_shape=[jax.ShapeDtypeStruct((b, n_tok, 512), BF16),
                   jax.ShapeDtypeStruct((b, n_tok, 128), BF16)],
        compiler_params=_params(("arbitrary", "arbitrary")),
        name="kprep",
    )(proj_t, proj_t, cos_t, sin_t, kn)


def _attn_kernel(*refs, nkv, lam_init, aliased):
    qa_ref, ka_ref, va_ref, qb_ref, kb_ref, vb_ref, cos_ref, sin_ref, qn_ref, lamv_ref, sub_ref = refs[:11]
    rest = refs[13:] if aliased else refs[11:]
    oa_ref, ob_ref = rest[:2]
    scratch_a, scratch_b = rest[2:6], rest[6:10]
    _attn_body("A", qa_ref, ka_ref, va_ref, cos_ref, sin_ref, (qn_ref,), oa_ref, *scratch_a,
               nkv=nkv, lam_init=lam_init)
    _attn_body("B", qb_ref, kb_ref, vb_ref, cos_ref, sin_ref, (lamv_ref, sub_ref), ob_ref, *scratch_b,
               nkv=nkv, lam_init=lam_init)


def _attn_body(mode, q_ref, k_ref, v_ref, cos_ref, sin_ref, aux, o_ref, qz_ref, m_ref, acc_ref, redo_ref,
               *, nkv, lam_init):
    if mode == "A":
        qn_ref, = aux
    else:
        lamv_ref, sub_ref = aux
    kv = pl.program_id(2)
    dv = HEAD_DIM if mode == "A" else B_V_DIM

    def scores(h, rows=slice(None)):
        if mode == "A":
            kblk = k_ref[0, rows, :]
        else:
            kblk = k_ref[0, rows, LANES * (h // 2):LANES * (h // 2 + 1)]
        return _dot(kblk, qz_ref[h])

    @pl.when(kv == 0)
    def _init():
        c, s = cos_ref[...], sin_ref[...]
        for h in range(N_HEADS):
            x = q_ref[0, HEAD_DIM * h:HEAD_DIM * (h + 1), :].astype(F32)
            if mode == "A":
                x = _rmsnorm_t(x, qn_ref[...])
            xb = (_rope_t(x, c, s) * (SM_SCALE * LOG2E)).astype(BF16)
            z = jnp.zeros_like(xb)
            half = (h // A_GROUP) if mode == "A" else (h % 2)
            qz_ref[h] = jnp.concatenate([xb, z] if half == 0 else [z, xb], axis=0)
            m_ref[h:h + 1, :] = jnp.max(scores(h, slice(0, STAB_KEYS)), axis=0, keepdims=True)
        acc_ref[0] = jnp.zeros(acc_ref.shape[1:], F32)

    acc_in = acc_ref.at[kv % 2]
    acc_out = acc_ref.at[(kv + 1) % 2]
    ones = jnp.ones((SUM_ROWS, k_ref.shape[1]), BF16)

    def values(h):
        if mode == "A":
            g = h // A_GROUP
            vt = v_ref[0, HEAD_DIM * g:HEAD_DIM * (g + 1), :]
        else:
            hh = h % B_HEADS
            vt = v_ref[0, B_V_DIM * hh:B_V_DIM * (hh + 1), :]
        return jnp.concatenate([vt, ones], axis=0)

    excess = None
    for h in range(N_HEADS):
        s = scores(h)
        m = m_ref[h:h + 1, :]
        d = jnp.max(s, axis=0, keepdims=True) - m
        excess = d if excess is None else jnp.maximum(excess, d)
        p = jnp.exp2((s - m).astype(BF16))
        acc_out[h] = acc_in[h] + _dot(values(h), p)
    redo_ref[0] = (jnp.max(excess) > STAB_MARGIN).astype(jnp.int32)

    @pl.when(redo_ref[0] != 0)
    def _exact_pass():
        for h in range(N_HEADS):
            s = scores(h)
            m_prev = m_ref[h:h + 1, :]
            m_new = jnp.maximum(m_prev, jnp.max(s, axis=0, keepdims=True))
            alpha = jnp.exp2(m_prev - m_new)
            p = jnp.exp2((s - m_new).astype(BF16))
            acc_out[h] = alpha * acc_in[h] + _dot(values(h), p)
            m_ref[h:h + 1, :] = m_new

    @pl.when(kv == nkv - 1)
    def _fin():
        acc = acc_ref.at[nkv % 2]

        def normalized(h):
            return acc[h, 0:dv, :] * (1.0 / acc[h, dv:dv + 1, :])

        outs = []
        if mode == "A":
            for h in range(N_HEADS):
                outs.append(normalized(h))
        else:
            lv = lamv_ref[...]
            lam = (jnp.exp(jnp.sum(lv[0:1] * lv[1:2], axis=1, keepdims=True))
                   - jnp.exp(jnp.sum(lv[2:3] * lv[3:4], axis=1, keepdims=True)) + lam_init)
            for hh in range(B_HEADS):
                o = normalized(hh) - lam * normalized(hh + B_HEADS)
                outs.append(_rmsnorm_t(o, sub_ref[...]) * (1.0 - lam_init))
        o_ref[0] = jnp.concatenate(outs, axis=0).T.astype(BF16)


def _attention(proj_t, k_a, k_b, cos_t, sin_t, qn, lam_vecs, subln, lam_init, n_lat, ctx_into=None):
    b, _, n_tok = proj_t.shape
    is_ctx = ctx_into is not None
    if is_ctx:
        tq = tk = n_tok - n_lat
        nq, nkv = 1, 1
        q_off = k_off = n_lat // tq
    else:
        tq, tk = Q_TILE, KV_TILE
        nq, nkv = n_lat // tq, n_tok // tk
        q_off = k_off = 0
    q_spec = lambda blk: pl.BlockSpec((1, 512, tq), lambda bi, i, j: (bi, blk, i + q_off))
    k_spec = lambda w: pl.BlockSpec((1, tk, w), lambda bi, i, j: (bi, j + k_off, 0))
    v_spec = lambda w, blk: pl.BlockSpec((1, w, tk), lambda bi, i, j: (bi, blk, j + k_off))
    rope_spec = pl.BlockSpec((HALF, tq), lambda bi, i, j: (0, i + q_off))
    full = lambda a: pl.BlockSpec(a.shape, lambda bi, i, j: (0, 0))
    in_specs = [q_spec(0), k_spec(128), v_spec(128, 16), q_spec(1), k_spec(512), v_spec(512, 2),
                rope_spec, rope_spec, full(qn), full(lam_vecs), full(subln)]
    args = [proj_t, k_a, proj_t, proj_t, k_b, proj_t, cos_t, sin_t, qn, lam_vecs, subln]
    aliases = {}
    if is_ctx:
        in_specs += [pl.BlockSpec(memory_space=pl.ANY)] * 2
        args += list(ctx_into)
        aliases = {len(args) - 2: 0, len(args) - 1: 1}
    o_spec = pl.BlockSpec((1, tq, 512), lambda bi, i, j: (bi, i + q_off, 0))
    o_shape = jax.ShapeDtypeStruct((b, n_tok, 512), BF16)
    scratch = []
    for dv in (HEAD_DIM, B_V_DIM):
        scratch += [pltpu.VMEM((N_HEADS, LANES, tq), BF16),
                    pltpu.VMEM((N_HEADS, tq), F32),
                    pltpu.VMEM((2, N_HEADS, dv + SUM_ROWS, tq), F32),
                    pltpu.SMEM((1,), jnp.int32)]
    return pl.pallas_call(
        functools.partial(_attn_kernel, nkv=nkv, lam_init=lam_init, aliased=is_ctx),
        grid=(b, nq, nkv),
        in_specs=in_specs,
        out_specs=[o_spec, o_spec],
        out_shape=[o_shape, o_shape],
        scratch_shapes=scratch,
        input_output_aliases=aliases,
        compiler_params=_params(("arbitrary", "arbitrary", "arbitrary")),
        name="attn_ctx" if is_ctx else "attn_lat",
    )(*args)


def _nbr_tables(rows):
    nblk = rows // NBR_ROWS
    combos, index = {}, np.zeros((3, NBR_ROWS, BAND_ROWS // 2), np.int32)
    for v, blk in enumerate((0, 1, nblk - 1)):
        bs = min(max(NBR_ROWS * blk - NA_KH // 2, 0), rows - BAND_ROWS)
        for qr in range(NBR_ROWS):
            r = NBR_ROWS * blk + qr
            rs = min(max(r - NA_KH // 2, 0), rows - NA_KH)
            for kp in range(BAND_ROWS // 2):
                codes = []
                for kr in (bs + 2 * kp, bs + 2 * kp + 1):
                    codes.append(kr - r + NA_KH - 1 if rs <= kr < rs + NA_KH else 15)
                index[v, qr, kp] = combos.setdefault(tuple(codes), len(combos))
    lo = np.array([c[0] for c in combos], np.int32)
    hi = np.array([c[1] for c in combos], np.int32)
    return index.reshape(-1), lo, hi


def _nbr_bias_table(rpb, lo, hi):
    nh = rpb.shape[0]
    qc = np.arange(GRID_W)[:, None]
    kc = np.arange(GRID_W)[None, :]
    cs = np.clip(qc - NA_KW // 2, 0, GRID_W - NA_KW)
    inside = (kc >= cs) & (kc < cs + NA_KW)
    dc = np.clip(kc - qc + NA_KW - 1, 0, 2 * NA_KW - 2)
    onehot = (dc.reshape(-1)[None, :] == np.arange(2 * NA_KW - 1)[:, None]).astype(np.float32)
    toep = jnp.einsum("hrc,cq->hrq", rpb, jnp.asarray(onehot), precision=HIGHEST)
    toep = toep.reshape(nh, 2 * NA_KH - 1, GRID_W, GRID_W)
    toep = jnp.where(jnp.asarray(inside)[None, None], toep, NEG)
    toep = jnp.concatenate([toep, jnp.full((nh, 1, GRID_W, GRID_W), NEG, F32)], axis=1)
    return jnp.concatenate([toep[:, lo], toep[:, hi]], axis=-1)


def _pair_heads(q, lane):
    zero = jnp.zeros_like(q)
    return jnp.where(lane < HEAD_DIM, q, zero), jnp.where(lane >= HEAD_DIM, q, zero)


def _nbr_kernel(idx_ref, q_ref, k0, k1, k2, k3, v0, v1, v2, v3, kc_ref, vc_ref, tp_ref,
                o_ref, s_ref, *, nblk):
    blk = pl.program_id(1)
    variant = jnp.where(blk == 0, 0, jnp.where(blk == nblk - 1, 2, 1))
    lane = lax.broadcasted_iota(jnp.int32, (1, LANES), 1)
    outs = []
    for p in range(N_HEADS // 2):
        cols = slice(LANES * p, LANES * (p + 1))
        kband = jnp.concatenate([k0[0, :, cols], k1[0, :, cols], k2[0, :, cols], k3[0, :, cols]], axis=0)
        vband = jnp.concatenate([v0[0, :, cols], v1[0, :, cols], v2[0, :, cols], v3[0, :, cols]], axis=0)
        kctx, vctx = kc_ref[0, :, cols], vc_ref[0, :, cols]
        qp = q_ref[0, :, cols] * jnp.asarray(SM_SCALE, BF16)
        res = []
        for e, qm in enumerate(_pair_heads(qp, lane)):
            h = 2 * p + e
            s_raw = _dot_nt(qm, kband)
            sc = _dot_nt(qm, kctx)
            for qr in range(NBR_ROWS):
                rws = slice(GRID_W * qr, GRID_W * (qr + 1))
                for kp in range(BAND_ROWS // 2):
                    u = idx_ref[variant * (NBR_ROWS * BAND_ROWS // 2) + qr * (BAND_ROWS // 2) + kp]
                    cl = slice(LANES * kp, LANES * (kp + 1))
                    s_ref[rws, cl] = s_raw[rws, cl] + tp_ref[h, u]
            s = s_ref[...]
            m = jnp.maximum(jnp.max(s, axis=-1, keepdims=True), jnp.max(sc, axis=-1, keepdims=True))
            pn = jnp.exp(s - m)
            pc = jnp.exp(sc - m)
            l = jnp.sum(pn, axis=-1, keepdims=True) + jnp.sum(pc, axis=-1, keepdims=True)
            o = _dot(pn.astype(BF16), vband) + _dot(pc.astype(BF16), vctx)
            res.append(o * (1.0 / l))
        outs.append(jnp.where(lane < HEAD_DIM, res[0], res[1]))
    o_ref[0] = jnp.concatenate(outs, axis=1).astype(BF16)


def _nbr_attention(proj_n, idx, tp, n_lat):
    b, n_tok, _ = proj_n.shape
    rows = n_lat // GRID_W
    nblk = rows // NBR_ROWS
    tq = NBR_ROWS * GRID_W
    blk_rows = CHUNK // GRID_W
    n_band = BAND_ROWS // blk_rows
    max_start = rows // blk_rows - n_band
    ctx_blk = n_lat // CHUNK
    q_col, k_col, v_col = 6, 7, 8

    def band_spec(j, col):
        def imap(bi, i, idx_ref):
            start = jnp.clip(2 * i - 1, 0, max_start)
            return (bi, start + j, col)
        return pl.BlockSpec((1, CHUNK, 512), imap)

    in_specs = [pl.BlockSpec((1, tq, 512), lambda bi, i, idx_ref: (bi, i, q_col))]
    in_specs += [band_spec(j, k_col) for j in range(n_band)]
    in_specs += [band_spec(j, v_col) for j in range(n_band)]
    in_specs += [pl.BlockSpec((1, CHUNK, 512), lambda bi, i, idx_ref: (bi, ctx_blk, k_col)),
                 pl.BlockSpec((1, CHUNK, 512), lambda bi, i, idx_ref: (bi, ctx_blk, v_col)),
                 pl.BlockSpec(tp.shape, lambda bi, i, idx_ref: (0, 0, 0, 0))]
    grid_spec = pltpu.PrefetchScalarGridSpec(
        num_scalar_prefetch=1, grid=(b, nblk), in_specs=in_specs,
        out_specs=pl.BlockSpec((1, tq, 512), lambda bi, i, idx_ref: (bi, i, 0)),
        scratch_shapes=[pltpu.VMEM((tq, BAND_ROWS * GRID_W), F32)])
    return pl.pallas_call(
        functools.partial(_nbr_kernel, nblk=nblk),
        grid_spec=grid_spec,
        out_shape=jax.ShapeDtypeStruct((b, n_tok, 512), BF16),
        compiler_params=_params(("arbitrary", "arbitrary")),
        name="nbr_attn",
    )(idx, *([proj_n] * (1 + 2 * n_band + 2)), tp)


def _ctx_c_kernel(q_ref, k_ref, v_ref, prev_ref, o_ref):
    del prev_ref
    lane = lax.broadcasted_iota(jnp.int32, (1, LANES), 1)
    outs = []
    for p in range(N_HEADS // 2):
        cols = slice(LANES * p, LANES * (p + 1))
        kp_, vp = k_ref[0, :, cols], v_ref[0, :, cols]
        qp = q_ref[0, :, cols] * jnp.asarray(SM_SCALE, BF16)
        res = []
        for qm in _pair_heads(qp, lane):
            s = _dot_nt(qm, kp_)
            m = jnp.max(s, axis=-1, keepdims=True)
            pr = jnp.exp(s - m)
            l = jnp.sum(pr, axis=-1, keepdims=True)
            res.append(_dot(pr.astype(BF16), vp) * (1.0 / l))
        outs.append(jnp.where(lane < HEAD_DIM, res[0], res[1]))
    o_ref[0] = jnp.concatenate(outs, axis=1).astype(BF16)


def _ctx_c_attention(proj_n, o_c, n_lat):
    b, n_tok, _ = proj_n.shape
    n_ctx = n_tok - n_lat
    blk = n_lat // n_ctx
    return pl.pallas_call(
        _ctx_c_kernel,
        grid=(b,),
        in_specs=[pl.BlockSpec((1, n_ctx, 512), lambda bi: (bi, blk, 6)),
                  pl.BlockSpec((1, n_ctx, 512), lambda bi: (bi, blk, 7)),
                  pl.BlockSpec((1, n_ctx, 512), lambda bi: (bi, blk, 8)),
                  pl.BlockSpec(memory_space=pl.ANY)],
        out_specs=pl.BlockSpec((1, n_ctx, 512), lambda bi: (bi, blk, 0)),
        out_shape=jax.ShapeDtypeStruct(o_c.shape, BF16),
        input_output_aliases={3: 0},
        compiler_params=_params(("arbitrary",)),
        name="ctx_c_attn",
    )(proj_n, proj_n, proj_n, o_c)


def _merge_kernel(*refs, with_router):
    (oa_ref, ob_ref, oc_ref, gate_ref, x_ref, wa_ref, wb_ref, wc_ref, wo_ref,
     mod_ref, g_ref) = refs[:11]
    if with_router:
        r_ref, xo_ref, h_ref, route_ref = refs[11:]
    else:
        xo_ref, h_ref = refs[11:]
    d = D_MODEL
    gate = lambda k: _sigmoid(gate_ref[0, :, k * d:(k + 1) * d].astype(F32))
    m = (gate(0) * _dot(oa_ref[0], wa_ref[...])
         + gate(1) * _dot(ob_ref[0], wb_ref[...])
         + gate(2) * _dot(oc_ref[0], wc_ref[...]))
    y = _dot(m.astype(BF16), wo_ref[...])
    for c in range(ROW_TILE // CHUNK):
        rows = slice(c * CHUNK, (c + 1) * CHUNK)
        xn = x_ref[0, rows, :] + mod_ref[0, c, 2:3, :] * y[rows, :]
        xo_ref[0, rows, :] = xn
        h = _norm_mod(xn, g_ref[...], mod_ref[0, c, 3:4, :], mod_ref[0, c, 4:5, :])
        if not with_router:
            h_ref[0, rows, :] = h.astype(BF16)
        else:
            h_ref[0, rows, :] = h
            lane = lax.broadcasted_iota(jnp.int32, (CHUNK, LANES), 1)
            logits = jnp.where(lane < N_EXPERTS, _dot(h, r_ref[...], precision=HIGHEST), NEG)
            m1 = jnp.max(logits, axis=-1, keepdims=True)
            i1 = jnp.min(jnp.where(logits == m1, lane, LANES), axis=-1, keepdims=True)
            rest = jnp.where(lane == i1, NEG, logits)
            m2 = jnp.max(rest, axis=-1, keepdims=True)
            i2 = jnp.min(jnp.where(rest == m2, lane, LANES), axis=-1, keepdims=True)
            e = jnp.exp(m2 - m1)
            w1 = 1.0 / (1.0 + e)
            route_ref[0, rows, :] = jnp.where(
                lane == 0, i1.astype(F32), jnp.where(
                    lane == 1, i2.astype(F32), jnp.where(
                        lane == 2, w1, jnp.where(lane == 3, e * w1, 0.0))))


def _merge(o_a, o_b, o_c, proj_n, x_all, wa, wb, wc, wo, modc, g_ffn, router):
    b, n_tok, d = x_all.shape
    nt = n_tok // ROW_TILE
    with_router = router is not None
    tile = lambda w: pl.BlockSpec((1, ROW_TILE, w), lambda bi, i: (bi, i, 0))
    full = lambda a: pl.BlockSpec(a.shape, lambda bi, i: (0,) * a.ndim)
    in_specs = [tile(512), tile(512), tile(512), tile(3 * d), tile(d),
                full(wa), full(wb), full(wc), full(wo),
                pl.BlockSpec((1, ROW_TILE // CHUNK, 6, d), lambda bi, i: (bi, i, 0, 0)),
                full(g_ffn)]
    args = [o_a, o_b, o_c, proj_n, x_all, wa, wb, wc, wo, modc, g_ffn]
    out_specs = [tile(d), tile(d)]
    out_shape = [jax.ShapeDtypeStruct((b, n_tok, d), F32), jax.ShapeDtypeStruct((b, n_tok, d), BF16)]
    if with_router:
        in_specs.append(full(router))
        args.append(router)
        out_shape[1] = jax.ShapeDtypeStruct((b, n_tok, d), F32)
        out_specs.append(tile(LANES))
        out_shape.append(jax.ShapeDtypeStruct((b, n_tok, LANES), F32))
    return pl.pallas_call(
        functools.partial(_merge_kernel, with_router=with_router),
        grid=(b, nt),
        in_specs=in_specs,
        out_specs=out_specs,
        out_shape=out_shape,
        compiler_params=_params(("arbitrary", "arbitrary")),
        name="merge_router" if with_router else "merge",
    )(*args)


def _swiglu_partial(h, w1, w3, w2):
    a = _dot(h, w1)
    t = (a * _sigmoid(a)) * _dot(h, w3)
    return _dot(t.astype(BF16), w2)


def _ffn_kernel(h_ref, x_ref, w1_ref, w3_ref, w2_ref, mod_ref, o_ref, acc_ref, *, n_f):
    @pl.when(pl.program_id(2) == 0)
    def _():
        acc_ref[...] = jnp.zeros(acc_ref.shape, F32)

    acc_ref[...] += _swiglu_partial(h_ref[0], w1_ref[0], w3_ref[0], w2_ref[0])

    @pl.when(pl.program_id(2) == n_f - 1)
    def _():
        for c in range(ROW_TILE // CHUNK):
            rows = slice(c * CHUNK, (c + 1) * CHUNK)
            o_ref[0, rows, :] = x_ref[0, rows, :] + mod_ref[0, c, 5:6, :] * acc_ref[rows, :]


def _ff_tile(f, cap):
    return max(t for t in range(LANES, min(f, cap) + 1, LANES) if f % t == 0)


def _ffn(h2, x_all, w1, w3, w2, li, modc):
    b, n_tok, d = x_all.shape
    nt = n_tok // ROW_TILE
    f = w1.shape[-1]
    tf = _ff_tile(f, 1408)
    n_f = f // tf
    tile = lambda w: pl.BlockSpec((1, ROW_TILE, w), lambda bi, i, j: (bi, i, 0))
    w13 = pl.BlockSpec((1, d, tf), lambda bi, i, j: (li, 0, j))
    w2s = pl.BlockSpec((1, tf, d), lambda bi, i, j: (li, j, 0))
    mods = pl.BlockSpec((1, ROW_TILE // CHUNK, 6, d), lambda bi, i, j: (bi, i, 0, 0))
    return pl.pallas_call(
        functools.partial(_ffn_kernel, n_f=n_f),
        grid=(b, nt, n_f),
        in_specs=[tile(d), tile(d), w13, w13, w2s, mods],
        out_specs=tile(d),
        out_shape=jax.ShapeDtypeStruct((b, n_tok, d), F32),
        scratch_shapes=[pltpu.VMEM((ROW_TILE, d), F32)],
        compiler_params=_params(("arbitrary",) * 3),
        name="ffn",
    )(h2, x_all, w1, w3, w2, modc)


def _row_copy(src_ref, dst_ref, src_row, dst_row, sem):
    return pltpu.make_async_copy(src_ref.at[pl.ds(src_row, 1), :], dst_ref.at[pl.ds(dst_row, 1), :], sem)


def _gather_start(idx_ref, idx_base, n_rows, src_ref, dst_ref, sem):
    def issue(j, carry):
        for prio in range(DMA_PRIORITIES):
            i = DMA_PRIORITIES * j + prio
            _row_copy(src_ref, dst_ref, idx_ref[idx_base + i], i, sem).start(priority=prio)
        return carry

    lax.fori_loop(0, n_rows // DMA_PRIORITIES, issue, 0, unroll=4)


def _gather_wait(n_rows, src_ref, dst_ref, sem):
    def drain(i, carry):
        _row_copy(src_ref, dst_ref, 0, i, sem).wait()
        return carry

    lax.fori_loop(0, n_rows, drain, 0, unroll=8)


def _expert_kernel(st_ref, te_ref, tv_ref, h_hbm, w1_ref, w3_ref, w2_ref, y_ref,
                   x_ref, h_ref, acc_ref, sem, *, n_f, n_tiles):
    del te_ref
    t, f = pl.program_id(0), pl.program_id(1)
    rows_per_step = MOE_TILE // n_f

    @pl.when((t == 0) & (f == 0))
    def _():
        _gather_start(st_ref, 0, MOE_TILE, h_hbm, x_ref.at[0], sem)
        _gather_wait(MOE_TILE, h_hbm, x_ref.at[0], sem)

    @pl.when(tv_ref[t] != 0)
    def _():
        @pl.when(f == 0)
        def _():
            h_ref[...] = x_ref[t % 2].astype(BF16)
            acc_ref[...] = jnp.zeros(acc_ref.shape, F32)

        nxt = x_ref.at[(t + 1) % 2]
        row0 = f * rows_per_step
        idx0 = jnp.minimum(t + 1, n_tiles - 1) * MOE_TILE + row0
        for i in range(rows_per_step):
            _row_copy(h_hbm, nxt, st_ref[idx0 + i], row0 + i, sem).start(priority=i % DMA_PRIORITIES)

        acc_ref[...] += _swiglu_partial(h_ref[...], w1_ref[0, 0], w3_ref[0, 0], w2_ref[0, 0])

        for i in range(rows_per_step):
            _row_copy(h_hbm, nxt, 0, row0 + i, sem).wait()

        @pl.when(f == n_f - 1)
        def _():
            y_ref[...] = acc_ref[...]

    @pl.when((tv_ref[t] == 0) & (f == n_f - 1))
    def _():
        y_ref[...] = jnp.zeros(y_ref.shape, F32)


def _experts(h_rows, slot_token, tile_expert, tile_valid, w1, w3, w2, li):
    n_slots = slot_token.shape[0]
    n_tiles = n_slots // MOE_TILE
    d, f = w1.shape[-2:]
    tf = _ff_tile(f, 896)
    n_f = f // tf
    assert MOE_TILE % n_f == 0
    grid_spec = pltpu.PrefetchScalarGridSpec(
        num_scalar_prefetch=3, grid=(n_tiles, n_f),
        in_specs=[pl.BlockSpec(memory_space=pl.ANY),
                  pl.BlockSpec((1, 1, d, tf), lambda t, j, st, te, tv: (li, te[t], 0, j)),
                  pl.BlockSpec((1, 1, d, tf), lambda t, j, st, te, tv: (li, te[t], 0, j)),
                  pl.BlockSpec((1, 1, tf, d), lambda t, j, st, te, tv: (li, te[t], j, 0))],
        out_specs=pl.BlockSpec((MOE_TILE, d), lambda t, j, st, te, tv: (t, 0)),
        scratch_shapes=[pltpu.VMEM((2, MOE_TILE, d), F32), pltpu.VMEM((MOE_TILE, d), BF16),
                        pltpu.VMEM((MOE_TILE, d), F32), pltpu.SemaphoreType.DMA])
    return pl.pallas_call(
        functools.partial(_expert_kernel, n_f=n_f, n_tiles=n_tiles),
        grid_spec=grid_spec,
        out_shape=jax.ShapeDtypeStruct((n_slots, d), F32),
        compiler_params=_params(("arbitrary", "arbitrary")),
        name="moe_experts",
    )(slot_token, tile_expert, tile_valid, h_rows, w1, w3, w2)


def _combine_kernel(pos_ref, y_ref, x_ref, route_ref, mod_ref, o_ref, g1_ref, g2_ref, sems, *, n_rows):
    base = (pl.program_id(0) * pl.num_programs(1) + pl.program_id(1)) * ROW_TILE
    _gather_start(pos_ref, base, ROW_TILE, y_ref, g1_ref, sems.at[0])
    _gather_start(pos_ref, n_rows + base, ROW_TILE, y_ref, g2_ref, sems.at[1])
    _gather_wait(ROW_TILE, y_ref, g1_ref, sems.at[0])
    _gather_wait(ROW_TILE, y_ref, g2_ref, sems.at[1])
    for c in range(ROW_TILE // CHUNK):
        rows = slice(c * CHUNK, (c + 1) * CHUNK)
        w1 = route_ref[0, rows, 2:3]
        w2 = route_ref[0, rows, 3:4]
        y = w1 * g1_ref[rows, :] + w2 * g2_ref[rows, :]
        o_ref[0, rows, :] = x_ref[0, rows, :] + mod_ref[0, c, 5:6, :] * y


def _combine(y, pos, x_mid, route, modc):
    b, n_tok, d = x_mid.shape
    nt = n_tok // ROW_TILE
    tile = lambda w: pl.BlockSpec((1, ROW_TILE, w), lambda bi, i, p: (bi, i, 0))
    grid_spec = pltpu.PrefetchScalarGridSpec(
        num_scalar_prefetch=1, grid=(b, nt),
        in_specs=[pl.BlockSpec(memory_space=pl.ANY), tile(d), tile(LANES),
                  pl.BlockSpec((1, ROW_TILE // CHUNK, 6, d), lambda bi, i, p: (bi, i, 0, 0))],
        out_specs=tile(d),
        scratch_shapes=[pltpu.VMEM((ROW_TILE, d), F32),
                        pltpu.VMEM((ROW_TILE, d), F32),
                        pltpu.SemaphoreType.DMA((2,))])
    return pl.pallas_call(
        functools.partial(_combine_kernel, n_rows=b * n_tok),
        grid_spec=grid_spec,
        out_shape=jax.ShapeDtypeStruct((b, n_tok, d), F32),
        compiler_params=_params(("arbitrary", "arbitrary")),
        name="moe_combine",
    )(pos, y, x_mid, route, modc)


def _routing_tables(route):
    n_rows = route.shape[0] * route.shape[1]
    r2 = route.reshape(n_rows, LANES)
    e_flat = jnp.concatenate([r2[:, 0], r2[:, 1]]).astype(jnp.int32)
    onehot = (e_flat[:, None] == jnp.arange(N_EXPERTS, dtype=jnp.int32)[None, :]).astype(jnp.int32)
    csum = jnp.cumsum(onehot, axis=0)
    rank = jnp.take_along_axis(csum, e_flat[:, None], axis=1)[:, 0] - 1
    padded = ((csum[-1] + MOE_TILE - 1) // MOE_TILE) * MOE_TILE
    ends = jnp.cumsum(padded)
    pos = (ends - padded)[e_flat] + rank
    n_slots = 2 * n_rows + N_EXPERTS * MOE_TILE
    token = jnp.arange(2 * n_rows, dtype=jnp.int32) % n_rows
    slot_token = jnp.zeros((n_slots,), jnp.int32).at[pos].set(token)
    tile_start = jnp.arange(n_slots // MOE_TILE, dtype=jnp.int32) * MOE_TILE
    tile_expert = jnp.minimum(jnp.searchsorted(ends, tile_start, side="right"), N_EXPERTS - 1)
    tile_valid = (tile_start < ends[-1]).astype(jnp.int32)
    return slot_token, tile_expert.astype(jnp.int32), tile_valid, pos.astype(jnp.int32)


def _moe(h_rows, x_mid, route, w1, w3, w2, li, modc):
    b, n_tok = x_mid.shape[:2]
    slot_token, tile_expert, tile_valid, pos = _routing_tables(route)
    y = _experts(h_rows.reshape(b * n_tok, -1), slot_token, tile_expert, tile_valid, w1, w3, w2, li)
    return _combine(y, pos, x_mid, route, modc)


def _final_kernel(x_ref, g_ref, o_ref):
    x = x_ref[0]
    ms = jnp.mean(x * x, axis=-1, keepdims=True)
    o_ref[0] = x * lax.rsqrt(ms + EPS) * g_ref[...]


def _final_norm(x_all, g, n_lat):
    b, _, d = x_all.shape
    tm = 1024
    return pl.pallas_call(
        _final_kernel,
        grid=(b, n_lat // tm),
        in_specs=[pl.BlockSpec((1, tm, d), lambda bi, i: (bi, i, 0)),
                  pl.BlockSpec((1, d), lambda bi, i: (0, 0))],
        out_specs=pl.BlockSpec((1, tm, d), lambda bi, i: (bi, i, 0)),
        out_shape=jax.ShapeDtypeStruct((b, n_lat, d), F32),
        compiler_params=_params(("arbitrary", "arbitrary")),
        name="final_norm",
    )(x_all, g)


def _deinterleave(w, n_heads):
    lead = w.shape[:-1]
    w = w.reshape(lead + (n_heads, HALF, 2))
    return jnp.swapaxes(w, -1, -2).reshape(lead + (n_heads * HEAD_DIM,))


def _rope_tables(n_lat, n_tok):
    t = np.arange(n_lat)
    inv = ROPE_THETA ** (-np.arange(HEAD_DIM // 4, dtype=np.float64) / (HEAD_DIM // 4))
    ang = np.concatenate([(t // GRID_W)[:, None] * inv, (t % GRID_W)[:, None] * inv], axis=-1)
    ang = np.concatenate([ang, np.zeros((n_tok - n_lat, HALF))], axis=0)
    return jnp.asarray(np.cos(ang).T, F32), jnp.asarray(np.sin(ang).T, F32)


def kernel(x, c, ctx, c_ctx, w_mod, b_mod, g_mix, g_ffn, w_in, qn_a, kn_a, lam_q1, lam_k1, lam_q2,
           lam_k2, subln_b, rpb_c, w_br_a, w_br_b, w_br_c, w_out, ffn_w1, ffn_w3, ffn_w2, router,
           moe_w1, moe_w3, moe_w2, g_final):
    b, n_lat, d = x.shape
    n_ctx = ctx.shape[1]
    n_tok = n_lat + n_ctx
    depth = w_in.shape[0]
    assert d == D_MODEL and n_ctx == CHUNK and b + 1 <= 8
    assert n_tok % ROW_TILE == 0 and n_tok % KV_TILE == 0 and n_lat % Q_TILE == 0
    assert n_lat % (NBR_ROWS * GRID_W) == 0 and n_lat // GRID_W >= 24
    n_chunks = n_tok // CHUNK

    x_all = jnp.concatenate([x, ctx], axis=1)
    cos_t, sin_t = _rope_tables(n_lat, n_tok)

    c_rows = jnp.concatenate([c, c_ctx[None], jnp.zeros((7 - b, d), F32)], axis=0)
    mods = _mod_vectors(c_rows, w_mod, b_mod).reshape(depth, 8, 6, d)
    mod_lat = jnp.broadcast_to(mods[:, :b, None], (depth, b, n_chunks - 1, 6, d))
    mod_ctx = jnp.broadcast_to(mods[:, b:b + 1, None], (depth, b, 1, 6, d))
    mod_chunks = jnp.concatenate([mod_lat, mod_ctx], axis=2)

    sl = lambda name: w_in[:, :, _OFF[name][0]:_OFF[name][1]]
    w_n = jnp.concatenate([sl("gate"), sl("cq"), sl("ck"), sl("cv")], axis=-1).astype(BF16)
    w_t = jnp.concatenate([_deinterleave(sl("aq"), 8), _deinterleave(sl("bq"), 8), sl("bv"),
                           _deinterleave(sl("bk"), 8), sl("av"), _deinterleave(sl("ak"), 2)], axis=-1)
    w_t = jnp.swapaxes(w_t, 1, 2).astype(BF16)
    qn_t = _deinterleave(qn_a, 1)[:, :, None]
    kn_t = _deinterleave(kn_a, 1)[:, :, None]
    lam_vecs = jnp.pad(jnp.stack([lam_q1, lam_k1, lam_q2, lam_k2], axis=1),
                       ((0, 0), (0, 4), (0, LANES - HEAD_DIM)))
    wa, wb, wc, wo = (w.astype(BF16) for w in (w_br_a, w_br_b, w_br_c, w_out))
    f1, f3, f2 = (w.astype(BF16) for w in (ffn_w1, ffn_w3, ffn_w2))
    m1, m3, m2 = (w.astype(BF16) for w in (moe_w1, moe_w3, moe_w2))
    router_p = jnp.pad(router, ((0, 0), (0, 0), (0, LANES - N_EXPERTS)))
    nbr_idx, nbr_lo, nbr_hi = _nbr_tables(n_lat // GRID_W)
    nbr_idx = jnp.asarray(nbr_idx)

    for layer in range(depth):
        lam_init = 0.8 - 0.6 * math.exp(-0.3 * layer)
        modc = mod_chunks[layer]
        g_mix_l = g_mix[layer][None]
        proj_n = _inproj(x_all, g_mix_l, modc, w_n[layer], transposed=False)
        proj_t = _inproj(x_all, g_mix_l, modc, w_t[layer], transposed=True)
        k_b, k_a = _kprep(proj_t, cos_t, sin_t, kn_t[layer])

        attn_args = (proj_t, k_a, k_b, cos_t, sin_t, qn_t[layer], lam_vecs[layer], subln_b[layer][:, None],
                     lam_init, n_lat)
        o_a, o_b = _attention(*attn_args, ctx_into=_attention(*attn_args))
        tp = _nbr_bias_table(rpb_c[layer], nbr_lo, nbr_hi)
        o_c = _nbr_attention(proj_n, nbr_idx, tp, n_lat)
        o_c = _ctx_c_attention(proj_n, o_c, n_lat)

        is_moe = layer % 2 == 1
        li = layer // 2
        merged = _merge(o_a, o_b, o_c, proj_n, x_all, wa[layer], wb[layer], wc[layer], wo[layer],
                        modc, g_ffn[layer][None], router_p[li] if is_moe else None)
        if is_moe:
            x_mid, h_rows, route = merged
            x_all = _moe(h_rows, x_mid, route, m1, m3, m2, li, modc)
        else:
            x_mid, h2 = merged
            x_all = _ffn(h2, x_mid, f1, f3, f2, li, modc)

    return _final_norm(x_all, g_final[None], n_lat)
```

```python
import functools
import math

import numpy as np
import jax
import jax.numpy as jnp
from jax import lax
from jax.experimental import pallas as pl
from jax.experimental.pallas import tpu as pltpu

F32 = jnp.float32
BF16 = jnp.bfloat16
HIGHEST = lax.Precision.HIGHEST

D_MODEL = 1024
HEAD_DIM = 64
HALF = HEAD_DIM // 2
GRID_W = 64
ROPE_THETA = 10000.0
EPS = 1e-6
N_HEADS = 8
A_GROUP = 4
B_HEADS = 4
B_V_DIM = 128
NA_KH = 8
NA_KW = 16
N_EXPERTS = 8
LANES = 128
CHUNK = 256
ROW_TILE = 768
INPROJ_COLS = 2304
Q_TILE = 1024
KV_TILE = 768
MOE_TILE = 512
DMA_PRIORITIES = 2
NBR_ROWS = 8
BAND_ROWS = 16
NEG = -1e30
SM_SCALE = HEAD_DIM ** -0.5
LOG2E = math.log2(math.e)
SUM_ROWS = 16
STAB_MARGIN = 64.0
STAB_KEYS = 128
VMEM_LIMIT = 56 * 1024 * 1024

_OFF = {}
_o = 0
for _name, _size in (("aq", 512), ("ak", 128), ("av", 128), ("bq", 512), ("bk", 512),
                     ("bv", 512), ("cq", 512), ("ck", 512), ("cv", 512), ("gate", 3072)):
    _OFF[_name] = (_o, _o + _size)
    _o += _size
D_IN = _o
N_COLS = 4608
T_COLS = 2304


def _params(sem):
    return pltpu.CompilerParams(dimension_semantics=sem, vmem_limit_bytes=VMEM_LIMIT)


def _sigmoid(x):
    return 1.0 / (1.0 + jnp.exp(-x))


def _dot(a, b, **kw):
    return jnp.dot(a, b, preferred_element_type=F32, **kw)


def _dot_nt(a, b):
    return lax.dot_general(a, b, (((1,), (1,)), ((), ())), preferred_element_type=F32)


def _norm_mod(x, g, shift, scale):
    ms = jnp.mean(x * x, axis=-1, keepdims=True)
    y = x * lax.rsqrt(ms + EPS) * g
    return y * (1.0 + scale) + shift


def _mod_kernel(c_ref, w_ref, b_ref, o_ref):
    c = c_ref[...]
    s = c * _sigmoid(c)
    o_ref[0] = _dot(s, w_ref[0], precision=HIGHEST) + b_ref[0]


def _mod_vectors(c_rows, w_mod, b_mod):
    depth, d, n = w_mod.shape
    tn = 1536
    return pl.pallas_call(
        _mod_kernel,
        grid=(depth, n // tn),
        in_specs=[pl.BlockSpec((8, d), lambda l, j: (0, 0)),
                  pl.BlockSpec((1, d, tn), lambda l, j: (l, 0, j)),
                  pl.BlockSpec((1, 1, tn), lambda l, j: (l, 0, j))],
        out_specs=pl.BlockSpec((1, 8, tn), lambda l, j: (l, 0, j)),
        out_shape=jax.ShapeDtypeStruct((depth, 8, n), F32),
        compiler_params=_params(("arbitrary", "arbitrary")),
        name="mod_vectors",
    )(c_rows, w_mod, b_mod.reshape(depth, 1, n))


def _inproj_kernel(x_ref, g_ref, mod_ref, w_ref, o_ref, h_ref, *, transposed):
    @pl.when(pl.program_id(2) == 0)
    def _():
        for c in range(ROW_TILE // CHUNK):
            rows = slice(c * CHUNK, (c + 1) * CHUNK)
            h = _norm_mod(x_ref[0, rows, :], g_ref[...], mod_ref[0, c, 0:1, :], mod_ref[0, c, 1:2, :])
            h_ref[rows, :] = h.astype(BF16)

    if transposed:
        o_ref[0] = _dot_nt(w_ref[...], h_ref[...]).astype(BF16)
    else:
        for c in range(o_ref.shape[2] // INPROJ_COLS):
            cols = slice(c * INPROJ_COLS, (c + 1) * INPROJ_COLS)
            o_ref[0, :, cols] = _dot(h_ref[...], w_ref[:, cols]).astype(BF16)


def _inproj(x_all, g, modc, w, transposed):
    b, n_tok, d = x_all.shape
    nt = n_tok // ROW_TILE
    if transposed:
        cols = tn = w.shape[0]
        w_spec = pl.BlockSpec((tn, d), lambda bi, i, j: (j, 0))
        o_spec = pl.BlockSpec((1, tn, ROW_TILE), lambda bi, i, j: (bi, j, i))
        o_shape = (b, cols, n_tok)
    else:
        cols = tn = w.shape[1]
        w_spec = pl.BlockSpec((d, tn), lambda bi, i, j: (0, j))
        o_spec = pl.BlockSpec((1, ROW_TILE, tn), lambda bi, i, j: (bi, i, j))
        o_shape = (b, n_tok, cols)
    return pl.pallas_call(
        functools.partial(_inproj_kernel, transposed=transposed),
        grid=(b, nt, cols // tn),
        in_specs=[pl.BlockSpec((1, ROW_TILE, d), lambda bi, i, j: (bi, i, 0)),
                  pl.BlockSpec((1, d), lambda bi, i, j: (0, 0)),
                  pl.BlockSpec((1, ROW_TILE // CHUNK, 6, d), lambda bi, i, j: (bi, i, 0, 0)),
                  w_spec],
        out_specs=o_spec,
        out_shape=jax.ShapeDtypeStruct(o_shape, BF16),
        scratch_shapes=[pltpu.VMEM((ROW_TILE, d), BF16)],
        compiler_params=_params(("arbitrary", "arbitrary", "arbitrary")),
        name="inproj_t" if transposed else "inproj_n",
    )(x_all, g, modc, w)


def _rope_t(x, c, s):
    x1, x2 = x[:HALF], x[HALF:]
    return jnp.concatenate([x1 * c - x2 * s, x1 * s + x2 * c], axis=0)


def _rmsnorm_t(x, g):
    ms = jnp.mean(x * x, axis=0, keepdims=True)
    return x * lax.rsqrt(ms + EPS) * g


def _kprep_kernel(kb_ref, ka_ref, cos_ref, sin_ref, kn_ref, ob_ref, oa_ref):
    c, s = cos_ref[...], sin_ref[...]
    kn = kn_ref[...]
    outs = []
    for h in range(N_HEADS):
        x = kb_ref[0, HEAD_DIM * h:HEAD_DIM * (h + 1), :].astype(F32)
        outs.append(_rope_t(x, c, s))
    ob_ref[0] = jnp.concatenate(outs, axis=0).T.astype(BF16)
    outs = []
    for h in range(N_HEADS // A_GROUP):
        x = ka_ref[0, HEAD_DIM * h:HEAD_DIM * (h + 1), :].astype(F32)
        outs.append(_rope_t(_rmsnorm_t(x, kn), c, s))
    oa_ref[0] = jnp.concatenate(outs, axis=0).T.astype(BF16)


def _kprep(proj_t, cos_t, sin_t, kn):
    b, _, n_tok = proj_t.shape
    nt = n_tok // ROW_TILE
    return pl.pallas_call(
        _kprep_kernel,
        grid=(b, nt),
        in_specs=[pl.BlockSpec((1, 512, ROW_TILE), lambda bi, i: (bi, 3, i)),
                  pl.BlockSpec((1, 128, ROW_TILE), lambda bi, i: (bi, 17, i)),
                  pl.BlockSpec((HALF, ROW_TILE), lambda bi, i: (0, i)),
                  pl.BlockSpec((HALF, ROW_TILE), lambda bi, i: (0, i)),
                  pl.BlockSpec((HEAD_DIM, 1), lambda bi, i: (0, 0))],
        out_specs=[pl.BlockSpec((1, ROW_TILE, 512), lambda bi, i: (bi, i, 0)),
                   pl.BlockSpec((1, ROW_TILE, 128), lambda bi, i: (bi, i, 0))],
        out_shape=[jax.ShapeDtypeStruct((b, n_tok, 512), BF16),
                   jax.ShapeDtypeStruct((b, n_tok, 128), BF16)],
        compiler_params=_params(("arbitrary", "arbitrary")),
        name="kprep",
    )(proj_t, proj_t, cos_t, sin_t, kn)


def _attn_kernel(*refs, nkv, lam_init, aliased):
    qa_ref, ka_ref, va_ref, qb_ref, kb_ref, vb_ref, cos_ref, sin_ref, qn_ref, lamv_ref, sub_ref = refs[:11]
    rest = refs[13:] if aliased else refs[11:]
    oa_ref, ob_ref = rest[:2]
    scratch_a, scratch_b = rest[2:6], rest[6:10]
    _attn_body("A", qa_ref, ka_ref, va_ref, cos_ref, sin_ref, (qn_ref,), oa_ref, *scratch_a,
               nkv=nkv, lam_init=lam_init)
    _attn_body("B", qb_ref, kb_ref, vb_ref, cos_ref, sin_ref, (lamv_ref, sub_ref), ob_ref, *scratch_b,
               nkv=nkv, lam_init=lam_init)


def _attn_body(mode, q_ref, k_ref, v_ref, cos_ref, sin_ref, aux, o_ref, qz_ref, m_ref, acc_ref, redo_ref,
               *, nkv, lam_init):
    if mode == "A":
        qn_ref, = aux
    else:
        lamv_ref, sub_ref = aux
    kv = pl.program_id(2)
    dv = HEAD_DIM if mode == "A" else B_V_DIM

    def scores(h, rows=slice(None)):
        if mode == "A":
            kblk = k_ref[0, rows, :]
        else:
            kblk = k_ref[0, rows, LANES * (h // 2):LANES * (h // 2 + 1)]
        return _dot(kblk, qz_ref[h])

    @pl.when(kv == 0)
    def _init():
        c, s = cos_ref[...], sin_ref[...]
        for h in range(N_HEADS):
            x = q_ref[0, HEAD_DIM * h:HEAD_DIM * (h + 1), :].astype(F32)
            if mode == "A":
                x = _rmsnorm_t(x, qn_ref[...])
            xb = (_rope_t(x, c, s) * (SM_SCALE * LOG2E)).astype(BF16)
            z = jnp.zeros_like(xb)
            half = (h // A_GROUP) if mode == "A" else (h % 2)
            qz_ref[h] = jnp.concatenate([xb, z] if half == 0 else [z, xb], axis=0)
            m_ref[h:h + 1, :] = jnp.max(scores(h, slice(0, STAB_KEYS)), axis=0, keepdims=True)
        acc_ref[0] = jnp.zeros(acc_ref.shape[1:], F32)

    acc_in = acc_ref.at[kv % 2]
    acc_out = acc_ref.at[(kv + 1) % 2]
    ones = jnp.ones((SUM_ROWS, k_ref.shape[1]), BF16)

    def values(h):
        if mode == "A":
            g = h // A_GROUP
            vt = v_ref[0, HEAD_DIM * g:HEAD_DIM * (g + 1), :]
        else:
            hh = h % B_HEADS
            vt = v_ref[0, B_V_DIM * hh:B_V_DIM * (hh + 1), :]
        return jnp.concatenate([vt, ones], axis=0)

    excess = None
    for h in range(N_HEADS):
        s = scores(h)
        m = m_ref[h:h + 1, :]
        d = jnp.max(s, axis=0, keepdims=True) - m
        excess = d if excess is None else jnp.maximum(excess, d)
        p = jnp.exp2((s - m).astype(BF16))
        acc_out[h] = acc_in[h] + _dot(values(h), p)
    redo_ref[0] = (jnp.max(excess) > STAB_MARGIN).astype(jnp.int32)

    @pl.when(redo_ref[0] != 0)
    def _exact_pass():
        for h in range(N_HEADS):
            s = scores(h)
            m_prev = m_ref[h:h + 1, :]
            m_new = jnp.maximum(m_prev, jnp.max(s, axis=0, keepdims=True))
            alpha = jnp.exp2(m_prev - m_new)
            p = jnp.exp2((s - m_new).astype(BF16))
            acc_out[h] = alpha * acc_in[h] + _dot(values(h), p)
            m_ref[h:h + 1, :] = m_new

    @pl.when(kv == nkv - 1)
    def _fin():
        acc = acc_ref.at[nkv % 2]

        def normalized(h):
            return acc[h, 0:dv, :] * (1.0 / acc[h, dv:dv + 1, :])

        outs = []
        if mode == "A":
            for h in range(N_HEADS):
                outs.append(normalized(h))
        else:
            lv = lamv_ref[...]
            lam = (jnp.exp(jnp.sum(lv[0:1] * lv[1:2], axis=1, keepdims=True))
                   - jnp.exp(jnp.sum(lv[2:3] * lv[3:4], axis=1, keepdims=True)) + lam_init)
            for hh in range(B_HEADS):
                o = normalized(hh) - lam * normalized(hh + B_HEADS)
                outs.append(_rmsnorm_t(o, sub_ref[...]) * (1.0 - lam_init))
        o_ref[0] = jnp.concatenate(outs, axis=0).T.astype(BF16)


def _attention(proj_t, k_a, k_b, cos_t, sin_t, qn, lam_vecs, subln, lam_init, n_lat, ctx_into=None):
    b, _, n_tok = proj_t.shape
    is_ctx = ctx_into is not None
    if is_ctx:
        tq = tk = n_tok - n_lat
        nq, nkv = 1, 1
        q_off = k_off = n_lat // tq
    else:
        tq, tk = Q_TILE, KV_TILE
        nq, nkv = n_lat // tq, n_tok // tk
        q_off = k_off = 0
    q_spec = lambda blk: pl.BlockSpec((1, 512, tq), lambda bi, i, j: (bi, blk, i + q_off))
    k_spec = lambda w: pl.BlockSpec((1, tk, w), lambda bi, i, j: (bi, j + k_off, 0))
    v_spec = lambda w, blk: pl.BlockSpec((1, w, tk), lambda bi, i, j: (bi, blk, j + k_off))
    rope_spec = pl.BlockSpec((HALF, tq), lambda bi, i, j: (0, i + q_off))
    full = lambda a: pl.BlockSpec(a.shape, lambda bi, i, j: (0, 0))
    in_specs = [q_spec(0), k_spec(128), v_spec(128, 16), q_spec(1), k_spec(512), v_spec(512, 2),
                rope_spec, rope_spec, full(qn), full(lam_vecs), full(subln)]
    args = [proj_t, k_a, proj_t, proj_t, k_b, proj_t, cos_t, sin_t, qn, lam_vecs, subln]
    aliases = {}
    if is_ctx:
        in_specs += [pl.BlockSpec(memory_space=pl.ANY)] * 2
        args += list(ctx_into)
        aliases = {len(args) - 2: 0, len(args) - 1: 1}
    o_spec = pl.BlockSpec((1, tq, 512), lambda bi, i, j: (bi, i + q_off, 0))
    o_shape = jax.ShapeDtypeStruct((b, n_tok, 512), BF16)
    scratch = []
    for dv in (HEAD_DIM, B_V_DIM):
        scratch += [pltpu.VMEM((N_HEADS, LANES, tq), BF16),
                    pltpu.VMEM((N_HEADS, tq), F32),
                    pltpu.VMEM((2, N_HEADS, dv + SUM_ROWS, tq), F32),
                    pltpu.SMEM((1,), jnp.int32)]
    return pl.pallas_call(
        functools.partial(_attn_kernel, nkv=nkv, lam_init=lam_init, aliased=is_ctx),
        grid=(b, nq, nkv),
        in_specs=in_specs,
        out_specs=[o_spec, o_spec],
        out_shape=[o_shape, o_shape],
        scratch_shapes=scratch,
        input_output_aliases=aliases,
        compiler_params=_params(("arbitrary", "arbitrary", "arbitrary")),
        name="attn_ctx" if is_ctx else "attn_lat",
    )(*args)


def _nbr_tables(rows):
    nblk = rows // NBR_ROWS
    combos, index = {}, np.zeros((3, NBR_ROWS, BAND_ROWS // 2), np.int32)
    for v, blk in enumerate((0, 1, nblk - 1)):
        bs = min(max(NBR_ROWS * blk - NA_KH // 2, 0), rows - BAND_ROWS)
        for qr in range(NBR_ROWS):
            r = NBR_ROWS * blk + qr
            rs = min(max(r - NA_KH // 2, 0), rows - NA_KH)
            for kp in range(BAND_ROWS // 2):
                codes = []
                for kr in (bs + 2 * kp, bs + 2 * kp + 1):
                    codes.append(kr - r + NA_KH - 1 if rs <= kr < rs + NA_KH else 15)
                index[v, qr, kp] = combos.setdefault(tuple(codes), len(combos))
    lo = np.array([c[0] for c in combos], np.int32)
    hi = np.array([c[1] for c in combos], np.int32)
    return index.reshape(-1), lo, hi


def _nbr_bias_table(rpb, lo, hi):
    nh = rpb.shape[0]
    qc = np.arange(GRID_W)[:, None]
    kc = np.arange(GRID_W)[None, :]
    cs = np.clip(qc - NA_KW // 2, 0, GRID_W - NA_KW)
    inside = (kc >= cs) & (kc < cs + NA_KW)
    dc = np.clip(kc - qc + NA_KW - 1, 0, 2 * NA_KW - 2)
    onehot = (dc.reshape(-1)[None, :] == np.arange(2 * NA_KW - 1)[:, None]).astype(np.float32)
    toep = jnp.einsum("hrc,cq->hrq", rpb, jnp.asarray(onehot), precision=HIGHEST)
    toep = toep.reshape(nh, 2 * NA_KH - 1, GRID_W, GRID_W)
    toep = jnp.where(jnp.asarray(inside)[None, None], toep, NEG)
    toep = jnp.concatenate([toep, jnp.full((nh, 1, GRID_W, GRID_W), NEG, F32)], axis=1)
    return jnp.concatenate([toep[:, lo], toep[:, hi]], axis=-1)


def _pair_heads(q, lane):
    zero = jnp.zeros_like(q)
    return jnp.where(lane < HEAD_DIM, q, zero), jnp.where(lane >= HEAD_DIM, q, zero)


def _nbr_kernel(idx_ref, q_ref, k0, k1, k2, k3, v0, v1, v2, v3, kc_ref, vc_ref, tp_ref,
                o_ref, s_ref, *, nblk):
    blk = pl.program_id(1)
    variant = jnp.where(blk == 0, 0, jnp.where(blk == nblk - 1, 2, 1))
    lane = lax.broadcasted_iota(jnp.int32, (1, LANES), 1)
    outs = []
    for p in range(N_HEADS // 2):
        cols = slice(LANES * p, LANES * (p + 1))
        kband = jnp.concatenate([k0[0, :, cols], k1[0, :, cols], k2[0, :, cols], k3[0, :, cols]], axis=0)
        vband = jnp.concatenate([v0[0, :, cols], v1[0, :, cols], v2[0, :, cols], v3[0, :, cols]], axis=0)
        kctx, vctx = kc_ref[0, :, cols], vc_ref[0, :, cols]
        qp = q_ref[0, :, cols] * jnp.asarray(SM_SCALE, BF16)
        res = []
        for e, qm in enumerate(_pair_heads(qp, lane)):
            h = 2 * p + e
            s_raw = _dot_nt(qm, kband)
            sc = _dot_nt(qm, kctx)
            for qr in range(NBR_ROWS):
                rws = slice(GRID_W * qr, GRID_W * (qr + 1))
                for kp in range(BAND_ROWS // 2):
                    u = idx_ref[variant * (NBR_ROWS * BAND_ROWS // 2) + qr * (BAND_ROWS // 2) + kp]
                    cl = slice(LANES * kp, LANES * (kp + 1))
                    s_ref[rws, cl] = s_raw[rws, cl] + tp_ref[h, u]
            s = s_ref[...]
            m = jnp.maximum(jnp.max(s, axis=-1, keepdims=True), jnp.max(sc, axis=-1, keepdims=True))
            pn = jnp.exp(s - m)
            pc = jnp.exp(sc - m)
            l = jnp.sum(pn, axis=-1, keepdims=True) + jnp.sum(pc, axis=-1, keepdims=True)
            o = _dot(pn.astype(BF16), vband) + _dot(pc.astype(BF16), vctx)
            res.append(o * (1.0 / l))
        outs.append(jnp.where(lane < HEAD_DIM, res[0], res[1]))
    o_ref[0] = jnp.concatenate(outs, axis=1).astype(BF16)


def _nbr_attention(proj_n, idx, tp, n_lat):
    b, n_tok, _ = proj_n.shape
    rows = n_lat // GRID_W
    nblk = rows // NBR_ROWS
    tq = NBR_ROWS * GRID_W
    blk_rows = CHUNK // GRID_W
    n_band = BAND_ROWS // blk_rows
    max_start = rows // blk_rows - n_band
    ctx_blk = n_lat // CHUNK
    q_col, k_col, v_col = 6, 7, 8

    def band_spec(j, col):
        def imap(bi, i, idx_ref):
            start = jnp.clip(2 * i - 1, 0, max_start)
            return (bi, start + j, col)
        return pl.BlockSpec((1, CHUNK, 512), imap)

    in_specs = [pl.BlockSpec((1, tq, 512), lambda bi, i, idx_ref: (bi, i, q_col))]
    in_specs += [band_spec(j, k_col) for j in range(n_band)]
    in_specs += [band_spec(j, v_col) for j in range(n_band)]
    in_specs += [pl.BlockSpec((1, CHUNK, 512), lambda bi, i, idx_ref: (bi, ctx_blk, k_col)),
                 pl.BlockSpec((1, CHUNK, 512), lambda bi, i, idx_ref: (bi, ctx_blk, v_col)),
                 pl.BlockSpec(tp.shape, lambda bi, i, idx_ref: (0, 0, 0, 0))]
    grid_spec = pltpu.PrefetchScalarGridSpec(
        num_scalar_prefetch=1, grid=(b, nblk), in_specs=in_specs,
        out_specs=pl.BlockSpec((1, tq, 512), lambda bi, i, idx_ref: (bi, i, 0)),
        scratch_shapes=[pltpu.VMEM((tq, BAND_ROWS * GRID_W), F32)])
    return pl.pallas_call(
        functools.partial(_nbr_kernel, nblk=nblk),
        grid_spec=grid_spec,
        out_shape=jax.ShapeDtypeStruct((b, n_tok, 512), BF16),
        compiler_params=_params(("arbitrary", "arbitrary")),
        name="nbr_attn",
    )(idx, *([proj_n] * (1 + 2 * n_band + 2)), tp)


def _ctx_c_kernel(q_ref, k_ref, v_ref, prev_ref, o_ref):
    del prev_ref
    lane = lax.broadcasted_iota(jnp.int32, (1, LANES), 1)
    outs = []
    for p in range(N_HEADS // 2):
        cols = slice(LANES * p, LANES * (p + 1))
        kp_, vp = k_ref[0, :, cols], v_ref[0, :, cols]
        qp = q_ref[0, :, cols] * jnp.asarray(SM_SCALE, BF16)
        res = []
        for qm in _pair_heads(qp, lane):
            s = _dot_nt(qm, kp_)
            m = jnp.max(s, axis=-1, keepdims=True)
            pr = jnp.exp(s - m)
            l = jnp.sum(pr, axis=-1, keepdims=True)
            res.append(_dot(pr.astype(BF16), vp) * (1.0 / l))
        outs.append(jnp.where(lane < HEAD_DIM, res[0], res[1]))
    o_ref[0] = jnp.concatenate(outs, axis=1).astype(BF16)


def _ctx_c_attention(proj_n, o_c, n_lat):
    b, n_tok, _ = proj_n.shape
    n_ctx = n_tok - n_lat
    blk = n_lat // n_ctx
    return pl.pallas_call(
        _ctx_c_kernel,
        grid=(b,),
        in_specs=[pl.BlockSpec((1, n_ctx, 512), lambda bi: (bi, blk, 6)),
                  pl.BlockSpec((1, n_ctx, 512), lambda bi: (bi, blk, 7)),
                  pl.BlockSpec((1, n_ctx, 512), lambda bi: (bi, blk, 8)),
                  pl.BlockSpec(memory_space=pl.ANY)],
        out_specs=pl.BlockSpec((1, n_ctx, 512), lambda bi: (bi, blk, 0)),
        out_shape=jax.ShapeDtypeStruct(o_c.shape, BF16),
        input_output_aliases={3: 0},
        compiler_params=_params(("arbitrary",)),
        name="ctx_c_attn",
    )(proj_n, proj_n, proj_n, o_c)


def _merge_kernel(*refs, with_router):
    (oa_ref, ob_ref, oc_ref, gate_ref, x_ref, wa_ref, wb_ref, wc_ref, wo_ref,
     mod_ref, g_ref) = refs[:11]
    if with_router:
        r_ref, xo_ref, h_ref, route_ref = refs[11:]
    else:
        xo_ref, h_ref = refs[11:]
    d = D_MODEL
    gate = lambda k: _sigmoid(gate_ref[0, :, k * d:(k + 1) * d].astype(F32))
    m = (gate(0) * _dot(oa_ref[0], wa_ref[...])
         + gate(1) * _dot(ob_ref[0], wb_ref[...])
         + gate(2) * _dot(oc_ref[0], wc_ref[...]))
    y = _dot(m.astype(BF16), wo_ref[...])
    for c in range(ROW_TILE // CHUNK):
        rows = slice(c * CHUNK, (c + 1) * CHUNK)
        xn = x_ref[0, rows, :] + mod_ref[0, c, 2:3, :] * y[rows, :]
        xo_ref[0, rows, :] = xn
        h = _norm_mod(xn, g_ref[...], mod_ref[0, c, 3:4, :], mod_ref[0, c, 4:5, :])
        if not with_router:
            h_ref[0, rows, :] = h.astype(BF16)
        else:
            h_ref[0, rows, :] = h
            lane = lax.broadcasted_iota(jnp.int32, (CHUNK, LANES), 1)
            logits = jnp.where(lane < N_EXPERTS, _dot(h, r_ref[...], precision=HIGHEST), NEG)
            m1 = jnp.max(logits, axis=-1, keepdims=True)
            i1 = jnp.min(jnp.where(logits == m1, lane, LANES), axis=-1, keepdims=True)
            rest = jnp.where(lane == i1, NEG, logits)
            m2 = jnp.max(rest, axis=-1, keepdims=True)
            i2 = jnp.min(jnp.where(rest == m2, lane, LANES), axis=-1, keepdims=True)
            e = jnp.exp(m2 - m1)
            w1 = 1.0 / (1.0 + e)
            route_ref[0, rows, :] = jnp.where(
                lane == 0, i1.astype(F32), jnp.where(
                    lane == 1, i2.astype(F32), jnp.where(
                        lane == 2, w1, jnp.where(lane == 3, e * w1, 0.0))))


def _merge(o_a, o_b, o_c, proj_n, x_all, wa, wb, wc, wo, modc, g_ffn, router):
    b, n_tok, d = x_all.shape
    nt = n_tok // ROW_TILE
    with_router = router is not None
    tile = lambda w: pl.BlockSpec((1, ROW_TILE, w), lambda bi, i: (bi, i, 0))
    full = lambda a: pl.BlockSpec(a.shape, lambda bi, i: (0,) * a.ndim)
    in_specs = [tile(512), tile(512), tile(512), tile(3 * d), tile(d),
                full(wa), full(wb), full(wc), full(wo),
                pl.BlockSpec((1, ROW_TILE // CHUNK, 6, d), lambda bi, i: (bi, i, 0, 0)),
                full(g_ffn)]
    args = [o_a, o_b, o_c, proj_n, x_all, wa, wb, wc, wo, modc, g_ffn]
    out_specs = [tile(d), tile(d)]
    out_shape = [jax.ShapeDtypeStruct((b, n_tok, d), F32), jax.ShapeDtypeStruct((b, n_tok, d), BF16)]
    if with_router:
        in_specs.append(full(router))
        args.append(router)
        out_shape[1] = jax.ShapeDtypeStruct((b, n_tok, d), F32)
        out_specs.append(tile(LANES))
        out_shape.append(jax.ShapeDtypeStruct((b, n_tok, LANES), F32))
    return pl.pallas_call(
        functools.partial(_merge_kernel, with_router=with_router),
        grid=(b, nt),
        in_specs=in_specs,
        out_specs=out_specs,
        out_shape=out_shape,
        compiler_params=_params(("arbitrary", "arbitrary")),
        name="merge_router" if with_router else "merge",
    )(*args)


def _swiglu_partial(h, w1, w3, w2):
    a = _dot(h, w1)
    t = (a * _sigmoid(a)) * _dot(h, w3)
    return _dot(t.astype(BF16), w2)


def _ffn_kernel(h_ref, x_ref, w1_ref, w3_ref, w2_ref, mod_ref, o_ref, acc_ref, *, n_f):
    @pl.when(pl.program_id(2) == 0)
    def _():
        acc_ref[...] = jnp.zeros(acc_ref.shape, F32)

    acc_ref[...] += _swiglu_partial(h_ref[0], w1_ref[0], w3_ref[0], w2_ref[0])

    @pl.when(pl.program_id(2) == n_f - 1)
    def _():
        for c in range(ROW_TILE // CHUNK):
            rows = slice(c * CHUNK, (c + 1) * CHUNK)
            o_ref[0, rows, :] = x_ref[0, rows, :] + mod_ref[0, c, 5:6, :] * acc_ref[rows, :]


def _ff_tile(f, cap):
    return max(t for t in range(LANES, min(f, cap) + 1, LANES) if f % t == 0)


def _ffn(h2, x_all, w1, w3, w2, li, modc):
    b, n_tok, d = x_all.shape
    nt = n_tok // ROW_TILE
    f = w1.shape[-1]
    tf = _ff_tile(f, 1408)
    n_f = f // tf
    tile = lambda w: pl.BlockSpec((1, ROW_TILE, w), lambda bi, i, j: (bi, i, 0))
    w13 = pl.BlockSpec((1, d, tf), lambda bi, i, j: (li, 0, j))
    w2s = pl.BlockSpec((1, tf, d), lambda bi, i, j: (li, j, 0))
    mods = pl.BlockSpec((1, ROW_TILE // CHUNK, 6, d), lambda bi, i, j: (bi, i, 0, 0))
    return pl.pallas_call(
        functools.partial(_ffn_kernel, n_f=n_f),
        grid=(b, nt, n_f),
        in_specs=[tile(d), tile(d), w13, w13, w2s, mods],
        out_specs=tile(d),
        out_shape=jax.ShapeDtypeStruct((b, n_tok, d), F32),
        scratch_shapes=[pltpu.VMEM((ROW_TILE, d), F32)],
        compiler_params=_params(("arbitrary",) * 3),
        name="ffn",
    )(h2, x_all, w1, w3, w2, modc)


def _row_copy(src_ref, dst_ref, src_row, dst_row, sem):
    return pltpu.make_async_copy(src_ref.at[pl.ds(src_row, 1), :], dst_ref.at[pl.ds(dst_row, 1), :], sem)


def _gather_start(idx_ref, idx_base, n_rows, src_ref, dst_ref, sem):
    def issue(j, carry):
        for prio in range(DMA_PRIORITIES):
            i = DMA_PRIORITIES * j + prio
            _row_copy(src_ref, dst_ref, idx_ref[idx_base + i], i, sem).start(priority=prio)
        return carry

    lax.fori_loop(0, n_rows // DMA_PRIORITIES, issue, 0, unroll=4)


def _gather_wait(n_rows, src_ref, dst_ref, sem):
    def drain(i, carry):
        _row_copy(src_ref, dst_ref, 0, i, sem).wait()
        return carry

    lax.fori_loop(0, n_rows, drain, 0, unroll=8)


def _dispatch_kernel(idx_ref, src_ref, o_ref, sem):
    _gather_start(idx_ref, pl.program_id(0) * MOE_TILE, MOE_TILE, src_ref, o_ref, sem)
    _gather_wait(MOE_TILE, src_ref, o_ref, sem)


def _dispatch(h_rows, slot_token):
    n_slots = slot_token.shape[0]
    d = h_rows.shape[1]
    grid_spec = pltpu.PrefetchScalarGridSpec(
        num_scalar_prefetch=1, grid=(n_slots // MOE_TILE,),
        in_specs=[pl.BlockSpec(memory_space=pl.ANY)],
        out_specs=pl.BlockSpec((MOE_TILE, d), lambda t, idx: (t, 0)),
        scratch_shapes=[pltpu.SemaphoreType.DMA])
    return pl.pallas_call(
        _dispatch_kernel,
        grid_spec=grid_spec,
        out_shape=jax.ShapeDtypeStruct((n_slots, d), F32),
        compiler_params=_params(("arbitrary",)),
        name="moe_dispatch",
    )(slot_token, h_rows)


def _expert_kernel(te_ref, tv_ref, xg_ref, w1_ref, w3_ref, w2_ref, y_ref, h_ref, acc_ref, *, n_f):
    del te_ref
    t, f = pl.program_id(0), pl.program_id(1)

    @pl.when(tv_ref[t] != 0)
    def _():
        @pl.when(f == 0)
        def _():
            h_ref[...] = xg_ref[...].astype(BF16)
            acc_ref[...] = jnp.zeros(acc_ref.shape, F32)

        acc_ref[...] += _swiglu_partial(h_ref[...], w1_ref[0, 0], w3_ref[0, 0], w2_ref[0, 0])

        @pl.when(f == n_f - 1)
        def _():
            y_ref[...] = acc_ref[...]

    @pl.when((tv_ref[t] == 0) & (f == n_f - 1))
    def _():
        y_ref[...] = jnp.zeros(y_ref.shape, F32)


def _experts(xg, tile_expert, tile_valid, w1, w3, w2, li):
    n_slots = xg.shape[0]
    d, f = w1.shape[-2:]
    tf = _ff_tile(f, 896)
    n_f = f // tf
    grid_spec = pltpu.PrefetchScalarGridSpec(
        num_scalar_prefetch=2, grid=(n_slots // MOE_TILE, n_f),
        in_specs=[pl.BlockSpec((MOE_TILE, d), lambda t, j, te, tv: (t, 0)),
                  pl.BlockSpec((1, 1, d, tf), lambda t, j, te, tv: (li, te[t], 0, j)),
                  pl.BlockSpec((1, 1, d, tf), lambda t, j, te, tv: (li, te[t], 0, j)),
                  pl.BlockSpec((1, 1, tf, d), lambda t, j, te, tv: (li, te[t], j, 0))],
        out_specs=pl.BlockSpec((MOE_TILE, d), lambda t, j, te, tv: (t, 0)),
        scratch_shapes=[pltpu.VMEM((MOE_TILE, d), BF16), pltpu.VMEM((MOE_TILE, d), F32)])
    return pl.pallas_call(
        functools.partial(_expert_kernel, n_f=n_f),
        grid_spec=grid_spec,
        out_shape=jax.ShapeDtypeStruct((n_slots, d), F32),
        compiler_params=_params(("arbitrary", "arbitrary")),
        name="moe_experts",
    )(tile_expert, tile_valid, xg, w1, w3, w2)


def _combine_kernel(pos_ref, y_ref, x_ref, route_ref, mod_ref, o_ref, g1_ref, g2_ref, sems, *, n_rows):
    base = (pl.program_id(0) * pl.num_programs(1) + pl.program_id(1)) * ROW_TILE
    _gather_start(pos_ref, base, ROW_TILE, y_ref, g1_ref, sems.at[0])
    _gather_start(pos_ref, n_rows + base, ROW_TILE, y_ref, g2_ref, sems.at[1])
    _gather_wait(ROW_TILE, y_ref, g1_ref, sems.at[0])
    _gather_wait(ROW_TILE, y_ref, g2_ref, sems.at[1])
    for c in range(ROW_TILE // CHUNK):
        rows = slice(c * CHUNK, (c + 1) * CHUNK)
        w1 = route_ref[0, rows, 2:3]
        w2 = route_ref[0, rows, 3:4]
        y = w1 * g1_ref[rows, :] + w2 * g2_ref[rows, :]
        o_ref[0, rows, :] = x_ref[0, rows, :] + mod_ref[0, c, 5:6, :] * y


def _combine(y, pos, x_mid, route, modc):
    b, n_tok, d = x_mid.shape
    nt = n_tok // ROW_TILE
    tile = lambda w: pl.BlockSpec((1, ROW_TILE, w), lambda bi, i, p: (bi, i, 0))
    grid_spec = pltpu.PrefetchScalarGridSpec(
        num_scalar_prefetch=1, grid=(b, nt),
        in_specs=[pl.BlockSpec(memory_space=pl.ANY), tile(d), tile(LANES),
                  pl.BlockSpec((1, ROW_TILE // CHUNK, 6, d), lambda bi, i, p: (bi, i, 0, 0))],
        out_specs=tile(d),
        scratch_shapes=[pltpu.VMEM((ROW_TILE, d), F32),
                        pltpu.VMEM((ROW_TILE, d), F32),
                        pltpu.SemaphoreType.DMA((2,))])
    return pl.pallas_call(
        functools.partial(_combine_kernel, n_rows=b * n_tok),
        grid_spec=grid_spec,
        out_shape=jax.ShapeDtypeStruct((b, n_tok, d), F32),
        compiler_params=_params(("arbitrary", "arbitrary")),
        name="moe_combine",
    )(pos, y, x_mid, route, modc)


def _routing_tables(route):
    n_rows = route.shape[0] * route.shape[1]
    r2 = route.reshape(n_rows, LANES)
    e_flat = jnp.concatenate([r2[:, 0], r2[:, 1]]).astype(jnp.int32)
    onehot = (e_flat[:, None] == jnp.arange(N_EXPERTS, dtype=jnp.int32)[None, :]).astype(jnp.int32)
    csum = jnp.cumsum(onehot, axis=0)
    rank = jnp.take_along_axis(csum, e_flat[:, None], axis=1)[:, 0] - 1
    padded = ((csum[-1] + MOE_TILE - 1) // MOE_TILE) * MOE_TILE
    ends = jnp.cumsum(padded)
    pos = (ends - padded)[e_flat] + rank
    n_slots = 2 * n_rows + N_EXPERTS * MOE_TILE
    token = jnp.arange(2 * n_rows, dtype=jnp.int32) % n_rows
    slot_token = jnp.zeros((n_slots,), jnp.int32).at[pos].set(token)
    tile_start = jnp.arange(n_slots // MOE_TILE, dtype=jnp.int32) * MOE_TILE
    tile_expert = jnp.minimum(jnp.searchsorted(ends, tile_start, side="right"), N_EXPERTS - 1)
    tile_valid = (tile_start < ends[-1]).astype(jnp.int32)
    return slot_token, tile_expert.astype(jnp.int32), tile_valid, pos.astype(jnp.int32)


def _moe(h_rows, x_mid, route, w1, w3, w2, li, modc):
    b, n_tok = x_mid.shape[:2]
    slot_token, tile_expert, tile_valid, pos = _routing_tables(route)
    xg = _dispatch(h_rows.reshape(b * n_tok, -1), slot_token)
    y = _experts(xg, tile_expert, tile_valid, w1, w3, w2, li)
    return _combine(y, pos, x_mid, route, modc)


def _final_kernel(x_ref, g_ref, o_ref):
    x = x_ref[0]
    ms = jnp.mean(x * x, axis=-1, keepdims=True)
    o_ref[0] = x * lax.rsqrt(ms + EPS) * g_ref[...]


def _final_norm(x_all, g, n_lat):
    b, _, d = x_all.shape
    tm = 1024
    return pl.pallas_call(
        _final_kernel,
        grid=(b, n_lat // tm),
        in_specs=[pl.BlockSpec((1, tm, d), lambda bi, i: (bi, i, 0)),
                  pl.BlockSpec((1, d), lambda bi, i: (0, 0))],
        out_specs=pl.BlockSpec((1, tm, d), lambda bi, i: (bi, i, 0)),
        out_shape=jax.ShapeDtypeStruct((b, n_lat, d), F32),
        compiler_params=_params(("arbitrary", "arbitrary")),
        name="final_norm",
    )(x_all, g)


def _deinterleave(w, n_heads):
    lead = w.shape[:-1]
    w = w.reshape(lead + (n_heads, HALF, 2))
    return jnp.swapaxes(w, -1, -2).reshape(lead + (n_heads * HEAD_DIM,))


def _rope_tables(n_lat, n_tok):
    t = np.arange(n_lat)
    inv = ROPE_THETA ** (-np.arange(HEAD_DIM // 4, dtype=np.float64) / (HEAD_DIM // 4))
    ang = np.concatenate([(t // GRID_W)[:, None] * inv, (t % GRID_W)[:, None] * inv], axis=-1)
    ang = np.concatenate([ang, np.zeros((n_tok - n_lat, HALF))], axis=0)
    return jnp.asarray(np.cos(ang).T, F32), jnp.asarray(np.sin(ang).T, F32)


def kernel(x, c, ctx, c_ctx, w_mod, b_mod, g_mix, g_ffn, w_in, qn_a, kn_a, lam_q1, lam_k1, lam_q2,
           lam_k2, subln_b, rpb_c, w_br_a, w_br_b, w_br_c, w_out, ffn_w1, ffn_w3, ffn_w2, router,
           moe_w1, moe_w3, moe_w2, g_final):
    b, n_lat, d = x.shape
    n_ctx = ctx.shape[1]
    n_tok = n_lat + n_ctx
    depth = w_in.shape[0]
    assert d == D_MODEL and n_ctx == CHUNK and b + 1 <= 8
    assert n_tok % ROW_TILE == 0 and n_tok % KV_TILE == 0 and n_lat % Q_TILE == 0
    assert n_lat % (NBR_ROWS * GRID_W) == 0 and n_lat // GRID_W >= 24
    n_chunks = n_tok // CHUNK

    x_all = jnp.concatenate([x, ctx], axis=1)
    cos_t, sin_t = _rope_tables(n_lat, n_tok)

    c_rows = jnp.concatenate([c, c_ctx[None], jnp.zeros((7 - b, d), F32)], axis=0)
    mods = _mod_vectors(c_rows, w_mod, b_mod).reshape(depth, 8, 6, d)
    mod_lat = jnp.broadcast_to(mods[:, :b, None], (depth, b, n_chunks - 1, 6, d))
    mod_ctx = jnp.broadcast_to(mods[:, b:b + 1, None], (depth, b, 1, 6, d))
    mod_chunks = jnp.concatenate([mod_lat, mod_ctx], axis=2)

    sl = lambda name: w_in[:, :, _OFF[name][0]:_OFF[name][1]]
    w_n = jnp.concatenate([sl("gate"), sl("cq"), sl("ck"), sl("cv")], axis=-1).astype(BF16)
    w_t = jnp.concatenate([_deinterleave(sl("aq"), 8), _deinterleave(sl("bq"), 8), sl("bv"),
                           _deinterleave(sl("bk"), 8), sl("av"), _deinterleave(sl("ak"), 2)], axis=-1)
    w_t = jnp.swapaxes(w_t, 1, 2).astype(BF16)
    qn_t = _deinterleave(qn_a, 1)[:, :, None]
    kn_t = _deinterleave(kn_a, 1)[:, :, None]
    lam_vecs = jnp.pad(jnp.stack([lam_q1, lam_k1, lam_q2, lam_k2], axis=1),
                       ((0, 0), (0, 4), (0, LANES - HEAD_DIM)))
    wa, wb, wc, wo = (w.astype(BF16) for w in (w_br_a, w_br_b, w_br_c, w_out))
    f1, f3, f2 = (w.astype(BF16) for w in (ffn_w1, ffn_w3, ffn_w2))
    m1, m3, m2 = (w.astype(BF16) for w in (moe_w1, moe_w3, moe_w2))
    router_p = jnp.pad(router, ((0, 0), (0, 0), (0, LANES - N_EXPERTS)))
    nbr_idx, nbr_lo, nbr_hi = _nbr_tables(n_lat // GRID_W)
    nbr_idx = jnp.asarray(nbr_idx)

    for layer in range(depth):
        lam_init = 0.8 - 0.6 * math.exp(-0.3 * layer)
        modc = mod_chunks[layer]
        g_mix_l = g_mix[layer][None]
        proj_n = _inproj(x_all, g_mix_l, modc, w_n[layer], transposed=False)
        proj_t = _inproj(x_all, g_mix_l, modc, w_t[layer], transposed=True)
        k_b, k_a = _kprep(proj_t, cos_t, sin_t, kn_t[layer])

        attn_args = (proj_t, k_a, k_b, cos_t, sin_t, qn_t[layer], lam_vecs[layer], subln_b[layer][:, None],
                     lam_init, n_lat)
        o_a, o_b = _attention(*attn_args, ctx_into=_attention(*attn_args))
        tp = _nbr_bias_table(rpb_c[layer], nbr_lo, nbr_hi)
        o_c = _nbr_attention(proj_n, nbr_idx, tp, n_lat)
        o_c = _ctx_c_attention(proj_n, o_c, n_lat)

        is_moe = layer % 2 == 1
        li = layer // 2
        merged = _merge(o_a, o_b, o_c, proj_n, x_all, wa[layer], wb[layer], wc[layer], wo[layer],
                        modc, g_ffn[layer][None], router_p[li] if is_moe else None)
        if is_moe:
            x_mid, h_rows, route = merged
            x_all = _moe(h_rows, x_mid, route, m1, m3, m2, li, modc)
        else:
            x_mid, h2 = merged
            x_all = _ffn(h2, x_mid, f1, f3, f2, li, modc)

    return _final_norm(x_all, g_final[None], n_lat)
```

```python
import functools
import math

import numpy as np
import jax
import jax.numpy as jnp
from jax import lax
from jax.experimental import pallas as pl
from jax.experimental.pallas import tpu as pltpu

F32 = jnp.float32
BF16 = jnp.bfloat16
HIGHEST = lax.Precision.HIGHEST

D_MODEL = 1024
HEAD_DIM = 64
HALF = HEAD_DIM // 2
GRID_W = 64
ROPE_THETA = 10000.0
EPS = 1e-6
N_HEADS = 8
A_GROUP = 4
B_HEADS = 4
B_V_DIM = 128
NA_KH = 8
NA_KW = 16
N_EXPERTS = 8
LANES = 128
CHUNK = 256
ROW_TILE = 768
INPROJ_COLS = 2304
Q_TILE = 1024
KV_TILE = 768
MOE_TILE = 512
DMA_PRIORITIES = 2
NBR_ROWS = 8
BAND_ROWS = 16
NEG = -1e30
SM_SCALE = HEAD_DIM ** -0.5
LOG2E = math.log2(math.e)
SUM_ROWS = 16
STAB_MARGIN = 64.0
STAB_KEYS = 128
VMEM_LIMIT = 56 * 1024 * 1024

_OFF = {}
_o = 0
for _name, _size in (("aq", 512), ("ak", 128), ("av", 128), ("bq", 512), ("bk", 512),
                     ("bv", 512), ("cq", 512), ("ck", 512), ("cv", 512), ("gate", 3072)):
    _OFF[_name] = (_o, _o + _size)
    _o += _size
D_IN = _o
N_COLS = 4608
T_COLS = 2304


def _params(sem):
    return pltpu.CompilerParams(dimension_semantics=sem, vmem_limit_bytes=VMEM_LIMIT)


def _sigmoid(x):
    return 1.0 / (1.0 + jnp.exp(-x))


def _dot(a, b, **kw):
    return jnp.dot(a, b, preferred_element_type=F32, **kw)


def _dot_nt(a, b):
    return lax.dot_general(a, b, (((1,), (1,)), ((), ())), preferred_element_type=F32)


def _norm_mod(x, g, shift, scale):
    ms = jnp.mean(x * x, axis=-1, keepdims=True)
    y = x * lax.rsqrt(ms + EPS) * g
    return y * (1.0 + scale) + shift


def _mod_kernel(c_ref, w_ref, b_ref, o_ref):
    c = c_ref[...]
    s = c * _sigmoid(c)
    o_ref[0] = _dot(s, w_ref[0], precision=HIGHEST) + b_ref[0]


def _mod_vectors(c_rows, w_mod, b_mod):
    depth, d, n = w_mod.shape
    tn = 1536
    return pl.pallas_call(
        _mod_kernel,
        grid=(depth, n // tn),
        in_specs=[pl.BlockSpec((8, d), lambda l, j: (0, 0)),
                  pl.BlockSpec((1, d, tn), lambda l, j: (l, 0, j)),
                  pl.BlockSpec((1, 1, tn), lambda l, j: (l, 0, j))],
        out_specs=pl.BlockSpec((1, 8, tn), lambda l, j: (l, 0, j)),
        out_shape=jax.ShapeDtypeStruct((depth, 8, n), F32),
        compiler_params=_params(("arbitrary", "arbitrary")),
        name="mod_vectors",
    )(c_rows, w_mod, b_mod.reshape(depth, 1, n))


def _inproj_kernel(x_ref, g_ref, mod_ref, w_ref, o_ref, h_ref, *, transposed):
    @pl.when(pl.program_id(2) == 0)
    def _():
        for c in range(ROW_TILE // CHUNK):
            rows = slice(c * CHUNK, (c + 1) * CHUNK)
            h = _norm_mod(x_ref[0, rows, :], g_ref[...], mod_ref[0, c, 0:1, :], mod_ref[0, c, 1:2, :])
            h_ref[rows, :] = h.astype(BF16)

    if transposed:
        o_ref[0] = _dot_nt(w_ref[...], h_ref[...]).astype(BF16)
    else:
        for c in range(o_ref.shape[2] // INPROJ_COLS):
            cols = slice(c * INPROJ_COLS, (c + 1) * INPROJ_COLS)
            o_ref[0, :, cols] = _dot(h_ref[...], w_ref[:, cols]).astype(BF16)


def _inproj(x_all, g, modc, w, transposed):
    b, n_tok, d = x_all.shape
    nt = n_tok // ROW_TILE
    if transposed:
        cols = tn = w.shape[0]
        w_spec = pl.BlockSpec((tn, d), lambda bi, i, j: (j, 0))
        o_spec = pl.BlockSpec((1, tn, ROW_TILE), lambda bi, i, j: (bi, j, i))
        o_shape = (b, cols, n_tok)
    else:
        cols = tn = w.shape[1]
        w_spec = pl.BlockSpec((d, tn), lambda bi, i, j: (0, j))
        o_spec = pl.BlockSpec((1, ROW_TILE, tn), lambda bi, i, j: (bi, i, j))
        o_shape = (b, n_tok, cols)
    return pl.pallas_call(
        functools.partial(_inproj_kernel, transposed=transposed),
        grid=(b, nt, cols // tn),
        in_specs=[pl.BlockSpec((1, ROW_TILE, d), lambda bi, i, j: (bi, i, 0)),
                  pl.BlockSpec((1, d), lambda bi, i, j: (0, 0)),
                  pl.BlockSpec((1, ROW_TILE // CHUNK, 6, d), lambda bi, i, j: (bi, i, 0, 0)),
                  w_spec],
        out_specs=o_spec,
        out_shape=jax.ShapeDtypeStruct(o_shape, BF16),
        scratch_shapes=[pltpu.VMEM((ROW_TILE, d), BF16)],
        compiler_params=_params(("arbitrary", "arbitrary", "arbitrary")),
        name="inproj_t" if transposed else "inproj_n",
    )(x_all, g, modc, w)


def _rope_t(x, c, s):
    x1, x2 = x[:HALF], x[HALF:]
    return jnp.concatenate([x1 * c - x2 * s, x1 * s + x2 * c], axis=0)


def _rmsnorm_t(x, g):
    ms = jnp.mean(x * x, axis=0, keepdims=True)
    return x * lax.rsqrt(ms + EPS) * g


def _kprep_kernel(kb_ref, ka_ref, cos_ref, sin_ref, kn_ref, ob_ref, oa_ref):
    c, s = cos_ref[...], sin_ref[...]
    kn = kn_ref[...]
    outs = []
    for h in range(N_HEADS):
        x = kb_ref[0, HEAD_DIM * h:HEAD_DIM * (h + 1), :].astype(F32)
        outs.append(_rope_t(x, c, s))
    ob_ref[0] = jnp.concatenate(outs, axis=0).T.astype(BF16)
    outs = []
    for h in range(N_HEADS // A_GROUP):
        x = ka_ref[0, HEAD_DIM * h:HEAD_DIM * (h + 1), :].astype(F32)
        outs.append(_rope_t(_rmsnorm_t(x, kn), c, s))
    oa_ref[0] = jnp.concatenate(outs, axis=0).T.astype(BF16)


def _kprep(proj_t, cos_t, sin_t, kn):
    b, _, n_tok = proj_t.shape
    nt = n_tok // ROW_TILE
    return pl.pallas_call(
        _kprep_kernel,
        grid=(b, nt),
        in_specs=[pl.BlockSpec((1, 512, ROW_TILE), lambda bi, i: (bi, 3, i)),
                  pl.BlockSpec((1, 128, ROW_TILE), lambda bi, i: (bi, 17, i)),
                  pl.BlockSpec((HALF, ROW_TILE), lambda bi, i: (0, i)),
                  pl.BlockSpec((HALF, ROW_TILE), lambda bi, i: (0, i)),
                  pl.BlockSpec((HEAD_DIM, 1), lambda bi, i: (0, 0))],
        out_specs=[pl.BlockSpec((1, ROW_TILE, 512), lambda bi, i: (bi, i, 0)),
                   pl.BlockSpec((1, ROW_TILE, 128), lambda bi, i: (bi, i, 0))],
        out_shape=[jax.ShapeDtypeStruct((b, n_tok, 512), BF16),
                   jax.ShapeDtypeStruct((b, n_tok, 128), BF16)],
        compiler_params=_params(("arbitrary", "arbitrary")),
        name="kprep",
    )(proj_t, proj_t, cos_t, sin_t, kn)


def _attn_kernel(*refs, nkv, lam_init, aliased):
    qa_ref, ka_ref, va_ref, qb_ref, kb_ref, vb_ref, cos_ref, sin_ref, qn_ref, lamv_ref, sub_ref = refs[:11]
    rest = refs[13:] if aliased else refs[11:]
    oa_ref, ob_ref = rest[:2]
    scratch_a, scratch_b = rest[2:6], rest[6:10]
    _attn_body("A", qa_ref, ka_ref, va_ref, cos_ref, sin_ref, (qn_ref,), oa_ref, *scratch_a,
               nkv=nkv, lam_init=lam_init)
    _attn_body("B", qb_ref, kb_ref, vb_ref, cos_ref, sin_ref, (lamv_ref, sub_ref), ob_ref, *scratch_b,
               nkv=nkv, lam_init=lam_init)


def _attn_body(mode, q_ref, k_ref, v_ref, cos_ref, sin_ref, aux, o_ref, qz_ref, m_ref, acc_ref, redo_ref,
               *, nkv, lam_init):
    if mode == "A":
        qn_ref, = aux
    else:
        lamv_ref, sub_ref = aux
    kv = pl.program_id(2)
    dv = HEAD_DIM if mode == "A" else B_V_DIM

    def scores(h, rows=slice(None)):
        if mode == "A":
            kblk = k_ref[0, rows, :]
        else:
            kblk = k_ref[0, rows, LANES * (h // 2):LANES * (h // 2 + 1)]
        return _dot(kblk, qz_ref[h])

    @pl.when(kv == 0)
    def _init():
        c, s = cos_ref[...], sin_ref[...]
        for h in range(N_HEADS):
            x = q_ref[0, HEAD_DIM * h:HEAD_DIM * (h + 1), :].astype(F32)
            if mode == "A":
                x = _rmsnorm_t(x, qn_ref[...])
            xb = (_rope_t(x, c, s) * (SM_SCALE * LOG2E)).astype(BF16)
            z = jnp.zeros_like(xb)
            half = (h // A_GROUP) if mode == "A" else (h % 2)
            qz_ref[h] = jnp.concatenate([xb, z] if half == 0 else [z, xb], axis=0)
            m_ref[h:h + 1, :] = jnp.max(scores(h, slice(0, STAB_KEYS)), axis=0, keepdims=True)
        acc_ref[0] = jnp.zeros(acc_ref.shape[1:], F32)

    acc_in = acc_ref.at[kv % 2]
    acc_out = acc_ref.at[(kv + 1) % 2]
    ones = jnp.ones((SUM_ROWS, k_ref.shape[1]), BF16)

    def values(h):
        if mode == "A":
            g = h // A_GROUP
            vt = v_ref[0, HEAD_DIM * g:HEAD_DIM * (g + 1), :]
        else:
            hh = h % B_HEADS
            vt = v_ref[0, B_V_DIM * hh:B_V_DIM * (hh + 1), :]
        return jnp.concatenate([vt, ones], axis=0)

    excess = None
    for h in range(N_HEADS):
        s = scores(h)
        m = m_ref[h:h + 1, :]
        d = jnp.max(s, axis=0, keepdims=True) - m
        excess = d if excess is None else jnp.maximum(excess, d)
        p = jnp.exp2(s - m).astype(BF16)
        acc_out[h] = acc_in[h] + _dot(values(h), p)
    redo_ref[0] = (jnp.max(excess) > STAB_MARGIN).astype(jnp.int32)

    @pl.when(redo_ref[0] != 0)
    def _exact_pass():
        for h in range(N_HEADS):
            s = scores(h)
            m_prev = m_ref[h:h + 1, :]
            m_new = jnp.maximum(m_prev, jnp.max(s, axis=0, keepdims=True))
            alpha = jnp.exp2(m_prev - m_new)
            p = jnp.exp2((s - m_new).astype(BF16))
            acc_out[h] = alpha * acc_in[h] + _dot(values(h), p)
            m_ref[h:h + 1, :] = m_new

    @pl.when(kv == nkv - 1)
    def _fin():
        acc = acc_ref.at[nkv % 2]

        def normalized(h):
            return acc[h, 0:dv, :] * (1.0 / acc[h, dv:dv + 1, :])

        outs = []
        if mode == "A":
            for h in range(N_HEADS):
                outs.append(normalized(h))
        else:
            lv = lamv_ref[...]
            lam = (jnp.exp(jnp.sum(lv[0:1] * lv[1:2], axis=1, keepdims=True))
                   - jnp.exp(jnp.sum(lv[2:3] * lv[3:4], axis=1, keepdims=True)) + lam_init)
            for hh in range(B_HEADS):
                o = normalized(hh) - lam * normalized(hh + B_HEADS)
                outs.append(_rmsnorm_t(o, sub_ref[...]) * (1.0 - lam_init))
        o_ref[0] = jnp.concatenate(outs, axis=0).T.astype(BF16)


def _attention(proj_t, k_a, k_b, cos_t, sin_t, qn, lam_vecs, subln, lam_init, n_lat, ctx_into=None):
    b, _, n_tok = proj_t.shape
    is_ctx = ctx_into is not None
    if is_ctx:
        tq = tk = n_tok - n_lat
        nq, nkv = 1, 1
        q_off = k_off = n_lat // tq
    else:
        tq, tk = Q_TILE, KV_TILE
        nq, nkv = n_lat // tq, n_tok // tk
        q_off = k_off = 0
    q_spec = lambda blk: pl.BlockSpec((1, 512, tq), lambda bi, i, j: (bi, blk, i + q_off))
    k_spec = lambda w: pl.BlockSpec((1, tk, w), lambda bi, i, j: (bi, j + k_off, 0))
    v_spec = lambda w, blk: pl.BlockSpec((1, w, tk), lambda bi, i, j: (bi, blk, j + k_off))
    rope_spec = pl.BlockSpec((HALF, tq), lambda bi, i, j: (0, i + q_off))
    full = lambda a: pl.BlockSpec(a.shape, lambda bi, i, j: (0, 0))
    in_specs = [q_spec(0), k_spec(128), v_spec(128, 16), q_spec(1), k_spec(512), v_spec(512, 2),
                rope_spec, rope_spec, full(qn), full(lam_vecs), full(subln)]
    args = [proj_t, k_a, proj_t, proj_t, k_b, proj_t, cos_t, sin_t, qn, lam_vecs, subln]
    aliases = {}
    if is_ctx:
        in_specs += [pl.BlockSpec(memory_space=pl.ANY)] * 2
        args += list(ctx_into)
        aliases = {len(args) - 2: 0, len(args) - 1: 1}
    o_spec = pl.BlockSpec((1, tq, 512), lambda bi, i, j: (bi, i + q_off, 0))
    o_shape = jax.ShapeDtypeStruct((b, n_tok, 512), BF16)
    scratch = []
    for dv in (HEAD_DIM, B_V_DIM):
        scratch += [pltpu.VMEM((N_HEADS, LANES, tq), BF16),
                    pltpu.VMEM((N_HEADS, tq), F32),
                    pltpu.VMEM((2, N_HEADS, dv + SUM_ROWS, tq), F32),
                    pltpu.SMEM((1,), jnp.int32)]
    return pl.pallas_call(
        functools.partial(_attn_kernel, nkv=nkv, lam_init=lam_init, aliased=is_ctx),
        grid=(b, nq, nkv),
        in_specs=in_specs,
        out_specs=[o_spec, o_spec],
        out_shape=[o_shape, o_shape],
        scratch_shapes=scratch,
        input_output_aliases=aliases,
        compiler_params=_params(("arbitrary", "arbitrary", "arbitrary")),
        name="attn_ctx" if is_ctx else "attn_lat",
    )(*args)


def _nbr_tables(rows):
    nblk = rows // NBR_ROWS
    combos, index = {}, np.zeros((3, NBR_ROWS, BAND_ROWS // 2), np.int32)
    for v, blk in enumerate((0, 1, nblk - 1)):
        bs = min(max(NBR_ROWS * blk - NA_KH // 2, 0), rows - BAND_ROWS)
        for qr in range(NBR_ROWS):
            r = NBR_ROWS * blk + qr
            rs = min(max(r - NA_KH // 2, 0), rows - NA_KH)
            for kp in range(BAND_ROWS // 2):
                codes = []
                for kr in (bs + 2 * kp, bs + 2 * kp + 1):
                    codes.append(kr - r + NA_KH - 1 if rs <= kr < rs + NA_KH else 15)
                index[v, qr, kp] = combos.setdefault(tuple(codes), len(combos))
    lo = np.array([c[0] for c in combos], np.int32)
    hi = np.array([c[1] for c in combos], np.int32)
    return index.reshape(-1), lo, hi


def _nbr_bias_table(rpb, lo, hi):
    nh = rpb.shape[0]
    qc = np.arange(GRID_W)[:, None]
    kc = np.arange(GRID_W)[None, :]
    cs = np.clip(qc - NA_KW // 2, 0, GRID_W - NA_KW)
    inside = (kc >= cs) & (kc < cs + NA_KW)
    dc = np.clip(kc - qc + NA_KW - 1, 0, 2 * NA_KW - 2)
    onehot = (dc.reshape(-1)[None, :] == np.arange(2 * NA_KW - 1)[:, None]).astype(np.float32)
    toep = jnp.einsum("hrc,cq->hrq", rpb, jnp.asarray(onehot), precision=HIGHEST)
    toep = toep.reshape(nh, 2 * NA_KH - 1, GRID_W, GRID_W)
    toep = jnp.where(jnp.asarray(inside)[None, None], toep, NEG)
    toep = jnp.concatenate([toep, jnp.full((nh, 1, GRID_W, GRID_W), NEG, F32)], axis=1)
    return jnp.concatenate([toep[:, lo], toep[:, hi]], axis=-1)


def _pair_heads(q, lane):
    zero = jnp.zeros_like(q)
    return jnp.where(lane < HEAD_DIM, q, zero), jnp.where(lane >= HEAD_DIM, q, zero)


def _nbr_kernel(idx_ref, q_ref, k0, k1, k2, k3, v0, v1, v2, v3, kc_ref, vc_ref, tp_ref,
                o_ref, s_ref, *, nblk):
    blk = pl.program_id(1)
    variant = jnp.where(blk == 0, 0, jnp.where(blk == nblk - 1, 2, 1))
    lane = lax.broadcasted_iota(jnp.int32, (1, LANES), 1)
    outs = []
    for p in range(N_HEADS // 2):
        cols = slice(LANES * p, LANES * (p + 1))
        kband = jnp.concatenate([k0[0, :, cols], k1[0, :, cols], k2[0, :, cols], k3[0, :, cols]], axis=0)
        vband = jnp.concatenate([v0[0, :, cols], v1[0, :, cols], v2[0, :, cols], v3[0, :, cols]], axis=0)
        kctx, vctx = kc_ref[0, :, cols], vc_ref[0, :, cols]
        qp = q_ref[0, :, cols] * jnp.asarray(SM_SCALE, BF16)
        res = []
        for e, qm in enumerate(_pair_heads(qp, lane)):
            h = 2 * p + e
            s_raw = _dot_nt(qm, kband)
            sc = _dot_nt(qm, kctx)
            for qr in range(NBR_ROWS):
                rws = slice(GRID_W * qr, GRID_W * (qr + 1))
                for kp in range(BAND_ROWS // 2):
                    u = idx_ref[variant * (NBR_ROWS * BAND_ROWS // 2) + qr * (BAND_ROWS // 2) + kp]
                    cl = slice(LANES * kp, LANES * (kp + 1))
                    s_ref[rws, cl] = s_raw[rws, cl] + tp_ref[h, u]
            s = s_ref[...]
            m = jnp.maximum(jnp.max(s, axis=-1, keepdims=True), jnp.max(sc, axis=-1, keepdims=True))
            pn = jnp.exp(s - m)
            pc = jnp.exp(sc - m)
            l = jnp.sum(pn, axis=-1, keepdims=True) + jnp.sum(pc, axis=-1, keepdims=True)
            o = _dot(pn.astype(BF16), vband) + _dot(pc.astype(BF16), vctx)
            res.append(o * (1.0 / l))
        outs.append(jnp.where(lane < HEAD_DIM, res[0], res[1]))
    o_ref[0] = jnp.concatenate(outs, axis=1).astype(BF16)


def _nbr_attention(proj_n, idx, tp, n_lat):
    b, n_tok, _ = proj_n.shape
    rows = n_lat // GRID_W
    nblk = rows // NBR_ROWS
    tq = NBR_ROWS * GRID_W
    blk_rows = CHUNK // GRID_W
    n_band = BAND_ROWS // blk_rows
    max_start = rows // blk_rows - n_band
    ctx_blk = n_lat // CHUNK
    q_col, k_col, v_col = 6, 7, 8

    def band_spec(j, col):
        def imap(bi, i, idx_ref):
            start = jnp.clip(2 * i - 1, 0, max_start)
            return (bi, start + j, col)
        return pl.BlockSpec((1, CHUNK, 512), imap)

    in_specs = [pl.BlockSpec((1, tq, 512), lambda bi, i, idx_ref: (bi, i, q_col))]
    in_specs += [band_spec(j, k_col) for j in range(n_band)]
    in_specs += [band_spec(j, v_col) for j in range(n_band)]
    in_specs += [pl.BlockSpec((1, CHUNK, 512), lambda bi, i, idx_ref: (bi, ctx_blk, k_col)),
                 pl.BlockSpec((1, CHUNK, 512), lambda bi, i, idx_ref: (bi, ctx_blk, v_col)),
                 pl.BlockSpec(tp.shape, lambda bi, i, idx_ref: (0, 0, 0, 0))]
    grid_spec = pltpu.PrefetchScalarGridSpec(
        num_scalar_prefetch=1, grid=(b, nblk), in_specs=in_specs,
        out_specs=pl.BlockSpec((1, tq, 512), lambda bi, i, idx_ref: (bi, i, 0)),
        scratch_shapes=[pltpu.VMEM((tq, BAND_ROWS * GRID_W), F32)])
    return pl.pallas_call(
        functools.partial(_nbr_kernel, nblk=nblk),
        grid_spec=grid_spec,
        out_shape=jax.ShapeDtypeStruct((b, n_tok, 512), BF16),
        compiler_params=_params(("arbitrary", "arbitrary")),
        name="nbr_attn",
    )(idx, *([proj_n] * (1 + 2 * n_band + 2)), tp)


def _ctx_c_kernel(q_ref, k_ref, v_ref, prev_ref, o_ref):
    del prev_ref
    lane = lax.broadcasted_iota(jnp.int32, (1, LANES), 1)
    outs = []
    for p in range(N_HEADS // 2):
        cols = slice(LANES * p, LANES * (p + 1))
        kp_, vp = k_ref[0, :, cols], v_ref[0, :, cols]
        qp = q_ref[0, :, cols] * jnp.asarray(SM_SCALE, BF16)
        res = []
        for qm in _pair_heads(qp, lane):
            s = _dot_nt(qm, kp_)
            m = jnp.max(s, axis=-1, keepdims=True)
            pr = jnp.exp(s - m)
            l = jnp.sum(pr, axis=-1, keepdims=True)
            res.append(_dot(pr.astype(BF16), vp) * (1.0 / l))
        outs.append(jnp.where(lane < HEAD_DIM, res[0], res[1]))
    o_ref[0] = jnp.concatenate(outs, axis=1).astype(BF16)


def _ctx_c_attention(proj_n, o_c, n_lat):
    b, n_tok, _ = proj_n.shape
    n_ctx = n_tok - n_lat
    blk = n_lat // n_ctx
    return pl.pallas_call(
        _ctx_c_kernel,
        grid=(b,),
        in_specs=[pl.BlockSpec((1, n_ctx, 512), lambda bi: (bi, blk, 6)),
                  pl.BlockSpec((1, n_ctx, 512), lambda bi: (bi, blk, 7)),
                  pl.BlockSpec((1, n_ctx, 512), lambda bi: (bi, blk, 8)),
                  pl.BlockSpec(memory_space=pl.ANY)],
        out_specs=pl.BlockSpec((1, n_ctx, 512), lambda bi: (bi, blk, 0)),
        out_shape=jax.ShapeDtypeStruct(o_c.shape, BF16),
        input_output_aliases={3: 0},
        compiler_params=_params(("arbitrary",)),
        name="ctx_c_attn",
    )(proj_n, proj_n, proj_n, o_c)


def _merge_kernel(*refs, with_router):
    (oa_ref, ob_ref, oc_ref, gate_ref, x_ref, wa_ref, wb_ref, wc_ref, wo_ref,
     mod_ref, g_ref) = refs[:11]
    if with_router:
        r_ref, xo_ref, h_ref, route_ref = refs[11:]
    else:
        xo_ref, h_ref = refs[11:]
    d = D_MODEL
    gate = lambda k: _sigmoid(gate_ref[0, :, k * d:(k + 1) * d].astype(F32))
    m = (gate(0) * _dot(oa_ref[0], wa_ref[...])
         + gate(1) * _dot(ob_ref[0], wb_ref[...])
         + gate(2) * _dot(oc_ref[0], wc_ref[...]))
    y = _dot(m.astype(BF16), wo_ref[...])
    for c in range(ROW_TILE // CHUNK):
        rows = slice(c * CHUNK, (c + 1) * CHUNK)
        xn = x_ref[0, rows, :] + mod_ref[0, c, 2:3, :] * y[rows, :]
        xo_ref[0, rows, :] = xn
        h = _norm_mod(xn, g_ref[...], mod_ref[0, c, 3:4, :], mod_ref[0, c, 4:5, :])
        if not with_router:
            h_ref[0, rows, :] = h.astype(BF16)
        else:
            h_ref[0, rows, :] = h
            lane = lax.broadcasted_iota(jnp.int32, (CHUNK, LANES), 1)
            logits = jnp.where(lane < N_EXPERTS, _dot(h, r_ref[...], precision=HIGHEST), NEG)
            m1 = jnp.max(logits, axis=-1, keepdims=True)
            i1 = jnp.min(jnp.where(logits == m1, lane, LANES), axis=-1, keepdims=True)
            rest = jnp.where(lane == i1, NEG, logits)
            m2 = jnp.max(rest, axis=-1, keepdims=True)
            i2 = jnp.min(jnp.where(rest == m2, lane, LANES), axis=-1, keepdims=True)
            e = jnp.exp(m2 - m1)
            w1 = 1.0 / (1.0 + e)
            route_ref[0, rows, :] = jnp.where(
                lane == 0, i1.astype(F32), jnp.where(
                    lane == 1, i2.astype(F32), jnp.where(
                        lane == 2, w1, jnp.where(lane == 3, e * w1, 0.0))))


def _merge(o_a, o_b, o_c, proj_n, x_all, wa, wb, wc, wo, modc, g_ffn, router):
    b, n_tok, d = x_all.shape
    nt = n_tok // ROW_TILE
    with_router = router is not None
    tile = lambda w: pl.BlockSpec((1, ROW_TILE, w), lambda bi, i: (bi, i, 0))
    full = lambda a: pl.BlockSpec(a.shape, lambda bi, i: (0,) * a.ndim)
    in_specs = [tile(512), tile(512), tile(512), tile(3 * d), tile(d),
                full(wa), full(wb), full(wc), full(wo),
                pl.BlockSpec((1, ROW_TILE // CHUNK, 6, d), lambda bi, i: (bi, i, 0, 0)),
                full(g_ffn)]
    args = [o_a, o_b, o_c, proj_n, x_all, wa, wb, wc, wo, modc, g_ffn]
    out_specs = [tile(d), tile(d)]
    out_shape = [jax.ShapeDtypeStruct((b, n_tok, d), F32), jax.ShapeDtypeStruct((b, n_tok, d), BF16)]
    if with_router:
        in_specs.append(full(router))
        args.append(router)
        out_shape[1] = jax.ShapeDtypeStruct((b, n_tok, d), F32)
        out_specs.append(tile(LANES))
        out_shape.append(jax.ShapeDtypeStruct((b, n_tok, LANES), F32))
    return pl.pallas_call(
        functools.partial(_merge_kernel, with_router=with_router),
        grid=(b, nt),
        in_specs=in_specs,
        out_specs=out_specs,
        out_shape=out_shape,
        compiler_params=_params(("arbitrary", "arbitrary")),
        name="merge_router" if with_router else "merge",
    )(*args)


def _swiglu_partial(h, w1, w3, w2):
    a = _dot(h, w1)
    t = (a * _sigmoid(a)) * _dot(h, w3)
    return _dot(t.astype(BF16), w2)


def _ffn_kernel(h_ref, x_ref, w1_ref, w3_ref, w2_ref, mod_ref, o_ref, acc_ref, *, n_f):
    @pl.when(pl.program_id(2) == 0)
    def _():
        acc_ref[...] = jnp.zeros(acc_ref.shape, F32)

    acc_ref[...] += _swiglu_partial(h_ref[0], w1_ref[0], w3_ref[0], w2_ref[0])

    @pl.when(pl.program_id(2) == n_f - 1)
    def _():
        for c in range(ROW_TILE // CHUNK):
            rows = slice(c * CHUNK, (c + 1) * CHUNK)
            o_ref[0, rows, :] = x_ref[0, rows, :] + mod_ref[0, c, 5:6, :] * acc_ref[rows, :]


def _ff_tile(f, cap):
    return max(t for t in range(LANES, min(f, cap) + 1, LANES) if f % t == 0)


def _ffn(h2, x_all, w1, w3, w2, li, modc):
    b, n_tok, d = x_all.shape
    nt = n_tok // ROW_TILE
    f = w1.shape[-1]
    tf = _ff_tile(f, 1408)
    n_f = f // tf
    tile = lambda w: pl.BlockSpec((1, ROW_TILE, w), lambda bi, i, j: (bi, i, 0))
    w13 = pl.BlockSpec((1, d, tf), lambda bi, i, j: (li, 0, j))
    w2s = pl.BlockSpec((1, tf, d), lambda bi, i, j: (li, j, 0))
    mods = pl.BlockSpec((1, ROW_TILE // CHUNK, 6, d), lambda bi, i, j: (bi, i, 0, 0))
    return pl.pallas_call(
        functools.partial(_ffn_kernel, n_f=n_f),
        grid=(b, nt, n_f),
        in_specs=[tile(d), tile(d), w13, w13, w2s, mods],
        out_specs=tile(d),
        out_shape=jax.ShapeDtypeStruct((b, n_tok, d), F32),
        scratch_shapes=[pltpu.VMEM((ROW_TILE, d), F32)],
        compiler_params=_params(("arbitrary",) * 3),
        name="ffn",
    )(h2, x_all, w1, w3, w2, modc)


def _row_copy(src_ref, dst_ref, src_row, dst_row, sem):
    return pltpu.make_async_copy(src_ref.at[pl.ds(src_row, 1), :], dst_ref.at[pl.ds(dst_row, 1), :], sem)


def _gather_start(idx_ref, idx_base, n_rows, src_ref, dst_ref, sem):
    def issue(j, carry):
        for prio in range(DMA_PRIORITIES):
            i = DMA_PRIORITIES * j + prio
            _row_copy(src_ref, dst_ref, idx_ref[idx_base + i], i, sem).start(priority=prio)
        return carry

    lax.fori_loop(0, n_rows // DMA_PRIORITIES, issue, 0, unroll=4)


def _gather_wait(n_rows, src_ref, dst_ref, sem):
    def drain(i, carry):
        _row_copy(src_ref, dst_ref, 0, i, sem).wait()
        return carry

    lax.fori_loop(0, n_rows, drain, 0, unroll=8)


def _dispatch_kernel(idx_ref, src_ref, o_ref, sem):
    _gather_start(idx_ref, pl.program_id(0) * MOE_TILE, MOE_TILE, src_ref, o_ref, sem)
    _gather_wait(MOE_TILE, src_ref, o_ref, sem)


def _dispatch(h_rows, slot_token):
    n_slots = slot_token.shape[0]
    d = h_rows.shape[1]
    grid_spec = pltpu.PrefetchScalarGridSpec(
        num_scalar_prefetch=1, grid=(n_slots // MOE_TILE,),
        in_specs=[pl.BlockSpec(memory_space=pl.ANY)],
        out_specs=pl.BlockSpec((MOE_TILE, d), lambda t, idx: (t, 0)),
        scratch_shapes=[pltpu.SemaphoreType.DMA])
    return pl.pallas_call(
        _dispatch_kernel,
        grid_spec=grid_spec,
        out_shape=jax.ShapeDtypeStruct((n_slots, d), F32),
        compiler_params=_params(("arbitrary",)),
        name="moe_dispatch",
    )(slot_token, h_rows)


def _expert_kernel(te_ref, tv_ref, xg_ref, w1_ref, w3_ref, w2_ref, y_ref, h_ref, acc_ref, *, n_f):
    del te_ref
    t, f = pl.program_id(0), pl.program_id(1)

    @pl.when(tv_ref[t] != 0)
    def _():
        @pl.when(f == 0)
        def _():
            h_ref[...] = xg_ref[...].astype(BF16)
            acc_ref[...] = jnp.zeros(acc_ref.shape, F32)

        acc_ref[...] += _swiglu_partial(h_ref[...], w1_ref[0, 0], w3_ref[0, 0], w2_ref[0, 0])

        @pl.when(f == n_f - 1)
        def _():
            y_ref[...] = acc_ref[...]

    @pl.when((tv_ref[t] == 0) & (f == n_f - 1))
    def _():
        y_ref[...] = jnp.zeros(y_ref.shape, F32)


def _experts(xg, tile_expert, tile_valid, w1, w3, w2, li):
    n_slots = xg.shape[0]
    d, f = w1.shape[-2:]
    tf = _ff_tile(f, 896)
    n_f = f // tf
    grid_spec = pltpu.PrefetchScalarGridSpec(
        num_scalar_prefetch=2, grid=(n_slots // MOE_TILE, n_f),
        in_specs=[pl.BlockSpec((MOE_TILE, d), lambda t, j, te, tv: (t, 0)),
                  pl.BlockSpec((1, 1, d, tf), lambda t, j, te, tv: (li, te[t], 0, j)),
                  pl.BlockSpec((1, 1, d, tf), lambda t, j, te, tv: (li, te[t], 0, j)),
                  pl.BlockSpec((1, 1, tf, d), lambda t, j, te, tv: (li, te[t], j, 0))],
        out_specs=pl.BlockSpec((MOE_TILE, d), lambda t, j, te, tv: (t, 0)),
        scratch_shapes=[pltpu.VMEM((MOE_TILE, d), BF16), pltpu.VMEM((MOE_TILE, d), F32)])
    return pl.pallas_call(
        functools.partial(_expert_kernel, n_f=n_f),
        grid_spec=grid_spec,
        out_shape=jax.ShapeDtypeStruct((n_slots, d), F32),
        compiler_params=_params(("arbitrary", "arbitrary")),
        name="moe_experts",
    )(tile_expert, tile_valid, xg, w1, w3, w2)


def _combine_kernel(pos_ref, y_ref, x_ref, route_ref, mod_ref, o_ref, g1_ref, g2_ref, sems, *, n_rows):
    base = (pl.program_id(0) * pl.num_programs(1) + pl.program_id(1)) * ROW_TILE
    _gather_start(pos_ref, base, ROW_TILE, y_ref, g1_ref, sems.at[0])
    _gather_start(pos_ref, n_rows + base, ROW_TILE, y_ref, g2_ref, sems.at[1])
    _gather_wait(ROW_TILE, y_ref, g1_ref, sems.at[0])
    _gather_wait(ROW_TILE, y_ref, g2_ref, sems.at[1])
    for c in range(ROW_TILE // CHUNK):
        rows = slice(c * CHUNK, (c + 1) * CHUNK)
        w1 = route_ref[0, rows, 2:3]
        w2 = route_ref[0, rows, 3:4]
        y = w1 * g1_ref[rows, :] + w2 * g2_ref[rows, :]
        o_ref[0, rows, :] = x_ref[0, rows, :] + mod_ref[0, c, 5:6, :] * y


def _combine(y, pos, x_mid, route, modc):
    b, n_tok, d = x_mid.shape
    nt = n_tok // ROW_TILE
    tile = lambda w: pl.BlockSpec((1, ROW_TILE, w), lambda bi, i, p: (bi, i, 0))
    grid_spec = pltpu.PrefetchScalarGridSpec(
        num_scalar_prefetch=1, grid=(b, nt),
        in_specs=[pl.BlockSpec(memory_space=pl.ANY), tile(d), tile(LANES),
                  pl.BlockSpec((1, ROW_TILE // CHUNK, 6, d), lambda bi, i, p: (bi, i, 0, 0))],
        out_specs=tile(d),
        scratch_shapes=[pltpu.VMEM((ROW_TILE, d), F32),
                        pltpu.VMEM((ROW_TILE, d), F32),
                        pltpu.SemaphoreType.DMA((2,))])
    return pl.pallas_call(
        functools.partial(_combine_kernel, n_rows=b * n_tok),
        grid_spec=grid_spec,
        out_shape=jax.ShapeDtypeStruct((b, n_tok, d), F32),
        compiler_params=_params(("arbitrary", "arbitrary")),
        name="moe_combine",
    )(pos, y, x_mid, route, modc)


def _routing_tables(route):
    n_rows = route.shape[0] * route.shape[1]
    r2 = route.reshape(n_rows, LANES)
    e_flat = jnp.concatenate([r2[:, 0], r2[:, 1]]).astype(jnp.int32)
    onehot = (e_flat[:, None] == jnp.arange(N_EXPERTS, dtype=jnp.int32)[None, :]).astype(jnp.int32)
    csum = jnp.cumsum(onehot, axis=0)
    rank = jnp.take_along_axis(csum, e_flat[:, None], axis=1)[:, 0] - 1
    padded = ((csum[-1] + MOE_TILE - 1) // MOE_TILE) * MOE_TILE
    ends = jnp.cumsum(padded)
    pos = (ends - padded)[e_flat] + rank
    n_slots = 2 * n_rows + N_EXPERTS * MOE_TILE
    token = jnp.arange(2 * n_rows, dtype=jnp.int32) % n_rows
    slot_token = jnp.zeros((n_slots,), jnp.int32).at[pos].set(token)
    tile_start = jnp.arange(n_slots // MOE_TILE, dtype=jnp.int32) * MOE_TILE
    tile_expert = jnp.minimum(jnp.searchsorted(ends, tile_start, side="right"), N_EXPERTS - 1)
    tile_valid = (tile_start < ends[-1]).astype(jnp.int32)
    return slot_token, tile_expert.astype(jnp.int32), tile_valid, pos.astype(jnp.int32)


def _moe(h_rows, x_mid, route, w1, w3, w2, li, modc):
    b, n_tok = x_mid.shape[:2]
    slot_token, tile_expert, tile_valid, pos = _routing_tables(route)
    xg = _dispatch(h_rows.reshape(b * n_tok, -1), slot_token)
    y = _experts(xg, tile_expert, tile_valid, w1, w3, w2, li)
    return _combine(y, pos, x_mid, route, modc)


def _final_kernel(x_ref, g_ref, o_ref):
    x = x_ref[0]
    ms = jnp.mean(x * x, axis=-1, keepdims=True)
    o_ref[0] = x * lax.rsqrt(ms + EPS) * g_ref[...]


def _final_norm(x_all, g, n_lat):
    b, _, d = x_all.shape
    tm = 1024
    return pl.pallas_call(
        _final_kernel,
        grid=(b, n_lat // tm),
        in_specs=[pl.BlockSpec((1, tm, d), lambda bi, i: (bi, i, 0)),
                  pl.BlockSpec((1, d), lambda bi, i: (0, 0))],
        out_specs=pl.BlockSpec((1, tm, d), lambda bi, i: (bi, i, 0)),
        out_shape=jax.ShapeDtypeStruct((b, n_lat, d), F32),
        compiler_params=_params(("arbitrary", "arbitrary")),
        name="final_norm",
    )(x_all, g)


def _deinterleave(w, n_heads):
    lead = w.shape[:-1]
    w = w.reshape(lead + (n_heads, HALF, 2))
    return jnp.swapaxes(w, -1, -2).reshape(lead + (n_heads * HEAD_DIM,))


def _rope_tables(n_lat, n_tok):
    t = np.arange(n_lat)
    inv = ROPE_THETA ** (-np.arange(HEAD_DIM // 4, dtype=np.float64) / (HEAD_DIM // 4))
    ang = np.concatenate([(t // GRID_W)[:, None] * inv, (t % GRID_W)[:, None] * inv], axis=-1)
    ang = np.concatenate([ang, np.zeros((n_tok - n_lat, HALF))], axis=0)
    return jnp.asarray(np.cos(ang).T, F32), jnp.asarray(np.sin(ang).T, F32)


def kernel(x, c, ctx, c_ctx, w_mod, b_mod, g_mix, g_ffn, w_in, qn_a, kn_a, lam_q1, lam_k1, lam_q2,
           lam_k2, subln_b, rpb_c, w_br_a, w_br_b, w_br_c, w_out, ffn_w1, ffn_w3, ffn_w2, router,
           moe_w1, moe_w3, moe_w2, g_final):
    b, n_lat, d = x.shape
    n_ctx = ctx.shape[1]
    n_tok = n_lat + n_ctx
    depth = w_in.shape[0]
    assert d == D_MODEL and n_ctx == CHUNK and b + 1 <= 8
    assert n_tok % ROW_TILE == 0 and n_tok % KV_TILE == 0 and n_lat % Q_TILE == 0
    assert n_lat % (NBR_ROWS * GRID_W) == 0 and n_lat // GRID_W >= 24
    n_chunks = n_tok // CHUNK

    x_all = jnp.concatenate([x, ctx], axis=1)
    cos_t, sin_t = _rope_tables(n_lat, n_tok)

    c_rows = jnp.concatenate([c, c_ctx[None], jnp.zeros((7 - b, d), F32)], axis=0)
    mods = _mod_vectors(c_rows, w_mod, b_mod).reshape(depth, 8, 6, d)
    mod_lat = jnp.broadcast_to(mods[:, :b, None], (depth, b, n_chunks - 1, 6, d))
    mod_ctx = jnp.broadcast_to(mods[:, b:b + 1, None], (depth, b, 1, 6, d))
    mod_chunks = jnp.concatenate([mod_lat, mod_ctx], axis=2)

    sl = lambda name: w_in[:, :, _OFF[name][0]:_OFF[name][1]]
    w_n = jnp.concatenate([sl("gate"), sl("cq"), sl("ck"), sl("cv")], axis=-1).astype(BF16)
    w_t = jnp.concatenate([_deinterleave(sl("aq"), 8), _deinterleave(sl("bq"), 8), sl("bv"),
                           _deinterleave(sl("bk"), 8), sl("av"), _deinterleave(sl("ak"), 2)], axis=-1)
    w_t = jnp.swapaxes(w_t, 1, 2).astype(BF16)
    qn_t = _deinterleave(qn_a, 1)[:, :, None]
    kn_t = _deinterleave(kn_a, 1)[:, :, None]
    lam_vecs = jnp.pad(jnp.stack([lam_q1, lam_k1, lam_q2, lam_k2], axis=1),
                       ((0, 0), (0, 4), (0, LANES - HEAD_DIM)))
    wa, wb, wc, wo = (w.astype(BF16) for w in (w_br_a, w_br_b, w_br_c, w_out))
    f1, f3, f2 = (w.astype(BF16) for w in (ffn_w1, ffn_w3, ffn_w2))
    m1, m3, m2 = (w.astype(BF16) for w in (moe_w1, moe_w3, moe_w2))
    router_p = jnp.pad(router, ((0, 0), (0, 0), (0, LANES - N_EXPERTS)))
    nbr_idx, nbr_lo, nbr_hi = _nbr_tables(n_lat // GRID_W)
    nbr_idx = jnp.asarray(nbr_idx)

    for layer in range(depth):
        lam_init = 0.8 - 0.6 * math.exp(-0.3 * layer)
        modc = mod_chunks[layer]
        g_mix_l = g_mix[layer][None]
        proj_n = _inproj(x_all, g_mix_l, modc, w_n[layer], transposed=False)
        proj_t = _inproj(x_all, g_mix_l, modc, w_t[layer], transposed=True)
        k_b, k_a = _kprep(proj_t, cos_t, sin_t, kn_t[layer])

        attn_args = (proj_t, k_a, k_b, cos_t, sin_t, qn_t[layer], lam_vecs[layer], subln_b[layer][:, None],
                     lam_init, n_lat)
        o_a, o_b = _attention(*attn_args, ctx_into=_attention(*attn_args))
        tp = _nbr_bias_table(rpb_c[layer], nbr_lo, nbr_hi)
        o_c = _nbr_attention(proj_n, nbr_idx, tp, n_lat)
        o_c = _ctx_c_attention(proj_n, o_c, n_lat)

        is_moe = layer % 2 == 1
        li = layer // 2
        merged = _merge(o_a, o_b, o_c, proj_n, x_all, wa[layer], wb[layer], wc[layer], wo[layer],
                        modc, g_ffn[layer][None], router_p[li] if is_moe else None)
        if is_moe:
            x_mid, h_rows, route = merged
            x_all = _moe(h_rows, x_mid, route, m1, m3, m2, li, modc)
        else:
            x_mid, h2 = merged
            x_all = _ffn(h2, x_mid, f1, f3, f2, li, modc)

    return _final_norm(x_all, g_final[None], n_lat)
```

```python
import functools
import math

import numpy as np
import jax
import jax.numpy as jnp
from jax import lax
from jax.experimental import pallas as pl
from jax.experimental.pallas import tpu as pltpu

F32 = jnp.float32
BF16 = jnp.bfloat16
HIGHEST = lax.Precision.HIGHEST

D_MODEL = 1024
HEAD_DIM = 64
HALF = HEAD_DIM // 2
GRID_W = 64
ROPE_THETA = 10000.0
EPS = 1e-6
N_HEADS = 8
A_GROUP = 4
B_HEADS = 4
B_V_DIM = 128
NA_KH = 8
NA_KW = 16
N_EXPERTS = 8
LANES = 128
CHUNK = 256
ROW_TILE = 768
INPROJ_COLS = 2304
Q_TILE = 1024
KV_TILE = 768
MOE_TILE = 512
DMA_PRIORITIES = 2
NBR_ROWS = 8
BAND_ROWS = 16
NEG = -1e30
SM_SCALE = HEAD_DIM ** -0.5
LOG2E = math.log2(math.e)
SUM_ROWS = 16
STAB_MARGIN = 64.0
STAB_KEYS = 128
VMEM_LIMIT = 56 * 1024 * 1024

_OFF = {}
_o = 0
for _name, _size in (("aq", 512), ("ak", 128), ("av", 128), ("bq", 512), ("bk", 512),
                     ("bv", 512), ("cq", 512), ("ck", 512), ("cv", 512), ("gate", 3072)):
    _OFF[_name] = (_o, _o + _size)
    _o += _size
D_IN = _o
N_COLS = 4608
T_COLS = 2304
KB_ROW0, KA_ROW0 = 1536, 2176


def _params(sem):
    return pltpu.CompilerParams(dimension_semantics=sem, vmem_limit_bytes=VMEM_LIMIT)


def _sigmoid(x):
    return 1.0 / (1.0 + jnp.exp(-x))


def _dot(a, b, **kw):
    return jnp.dot(a, b, preferred_element_type=F32, **kw)


def _dot_nt(a, b):
    return lax.dot_general(a, b, (((1,), (1,)), ((), ())), preferred_element_type=F32)


def _norm_mod(x, g, shift, scale):
    ms = jnp.mean(x * x, axis=-1, keepdims=True)
    y = x * lax.rsqrt(ms + EPS) * g
    return y * (1.0 + scale) + shift


def _mod_kernel(c_ref, w_ref, b_ref, o_ref):
    c = c_ref[...]
    s = c * _sigmoid(c)
    o_ref[0] = _dot(s, w_ref[0], precision=HIGHEST) + b_ref[0]


def _mod_vectors(c_rows, w_mod, b_mod):
    depth, d, n = w_mod.shape
    tn = 1536
    return pl.pallas_call(
        _mod_kernel,
        grid=(depth, n // tn),
        in_specs=[pl.BlockSpec((8, d), lambda l, j: (0, 0)),
                  pl.BlockSpec((1, d, tn), lambda l, j: (l, 0, j)),
                  pl.BlockSpec((1, 1, tn), lambda l, j: (l, 0, j))],
        out_specs=pl.BlockSpec((1, 8, tn), lambda l, j: (l, 0, j)),
        out_shape=jax.ShapeDtypeStruct((depth, 8, n), F32),
        compiler_params=_params(("arbitrary", "arbitrary")),
        name="mod_vectors",
    )(c_rows, w_mod, b_mod.reshape(depth, 1, n))


def _rope_t(x, c, s):
    x1, x2 = x[:HALF], x[HALF:]
    return jnp.concatenate([x1 * c - x2 * s, x1 * s + x2 * c], axis=0)


def _rmsnorm_t(x, g):
    ms = jnp.mean(x * x, axis=0, keepdims=True)
    return x * lax.rsqrt(ms + EPS) * g


def _modulated_tile(x_ref, g_ref, mod_ref):
    chunks = []
    for c in range(ROW_TILE // CHUNK):
        rows = slice(c * CHUNK, (c + 1) * CHUNK)
        h = _norm_mod(x_ref[0, rows, :], g_ref[...], mod_ref[0, c, 0:1, :], mod_ref[0, c, 1:2, :])
        chunks.append(h.astype(BF16))
    return jnp.concatenate(chunks, axis=0)


def _inproj_n_kernel(x_ref, g_ref, mod_ref, w_ref, o_ref):
    h = _modulated_tile(x_ref, g_ref, mod_ref)
    for c in range(o_ref.shape[2] // INPROJ_COLS):
        cols = slice(c * INPROJ_COLS, (c + 1) * INPROJ_COLS)
        o_ref[0, :, cols] = _dot(h, w_ref[:, cols]).astype(BF16)


def _inproj_t_kernel(x_ref, g_ref, mod_ref, w_ref, cos_ref, sin_ref, kn_ref, o_ref, kb_ref, ka_ref):
    r = _dot_nt(w_ref[...], _modulated_tile(x_ref, g_ref, mod_ref))
    o_ref[0] = r.astype(BF16)
    c, s = cos_ref[...], sin_ref[...]
    heads_b = [_rope_t(r[KB_ROW0 + HEAD_DIM * h:KB_ROW0 + HEAD_DIM * (h + 1)], c, s) for h in range(N_HEADS)]
    kb_ref[0] = jnp.concatenate(heads_b, axis=0).T.astype(BF16)
    heads_a = [_rope_t(_rmsnorm_t(r[KA_ROW0 + HEAD_DIM * h:KA_ROW0 + HEAD_DIM * (h + 1)], kn_ref[...]), c, s)
               for h in range(N_HEADS // A_GROUP)]
    ka_ref[0] = jnp.concatenate(heads_a, axis=0).T.astype(BF16)


def _inproj(x_all, g, modc, w, rope=None):
    b, n_tok, d = x_all.shape
    nt = n_tok // ROW_TILE
    in_specs = [pl.BlockSpec((1, ROW_TILE, d), lambda bi, i: (bi, i, 0)),
                pl.BlockSpec((1, d), lambda bi, i: (0, 0)),
                pl.BlockSpec((1, ROW_TILE // CHUNK, 6, d), lambda bi, i: (bi, i, 0, 0)),
                pl.BlockSpec(w.shape, lambda bi, i: (0, 0))]
    args = [x_all, g, modc, w]
    tok = lambda width: pl.BlockSpec((1, ROW_TILE, width), lambda bi, i: (bi, i, 0))
    if rope is None:
        kern, name = _inproj_n_kernel, "inproj_n"
        out_specs = tok(w.shape[1])
        out_shape = jax.ShapeDtypeStruct((b, n_tok, w.shape[1]), BF16)
    else:
        kern, name = _inproj_t_kernel, "inproj_t"
        cos_t, sin_t, kn = rope
        in_specs += [pl.BlockSpec((HALF, ROW_TILE), lambda bi, i: (0, i)),
                     pl.BlockSpec((HALF, ROW_TILE), lambda bi, i: (0, i)),
                     pl.BlockSpec((HEAD_DIM, 1), lambda bi, i: (0, 0))]
        args += [cos_t, sin_t, kn]
        out_specs = [pl.BlockSpec((1, w.shape[0], ROW_TILE), lambda bi, i: (bi, 0, i)), tok(512), tok(128)]
        out_shape = [jax.ShapeDtypeStruct((b, w.shape[0], n_tok), BF16),
                     jax.ShapeDtypeStruct((b, n_tok, 512), BF16),
                     jax.ShapeDtypeStruct((b, n_tok, 128), BF16)]
    return pl.pallas_call(
        kern,
        grid=(b, nt),
        in_specs=in_specs,
        out_specs=out_specs,
        out_shape=out_shape,
        compiler_params=_params(("arbitrary", "arbitrary")),
        name=name,
    )(*args)


def _attn_kernel(*refs, nkv, lam_init, aliased):
    qa_ref, ka_ref, va_ref, qb_ref, kb_ref, vb_ref, cos_ref, sin_ref, qn_ref, lamv_ref, sub_ref = refs[:11]
    rest = refs[13:] if aliased else refs[11:]
    oa_ref, ob_ref = rest[:2]
    scratch_a, scratch_b = rest[2:6], rest[6:10]
    _attn_body("A", qa_ref, ka_ref, va_ref, cos_ref, sin_ref, (qn_ref,), oa_ref, *scratch_a,
               nkv=nkv, lam_init=lam_init)
    _attn_body("B", qb_ref, kb_ref, vb_ref, cos_ref, sin_ref, (lamv_ref, sub_ref), ob_ref, *scratch_b,
               nkv=nkv, lam_init=lam_init)


def _attn_body(mode, q_ref, k_ref, v_ref, cos_ref, sin_ref, aux, o_ref, qz_ref, m_ref, acc_ref, redo_ref,
               *, nkv, lam_init):
    if mode == "A":
        qn_ref, = aux
    else:
        lamv_ref, sub_ref = aux
    kv = pl.program_id(2)
    dv = HEAD_DIM if mode == "A" else B_V_DIM

    def scores(h, rows=slice(None)):
        if mode == "A":
            kblk = k_ref[0, rows, :]
        else:
            kblk = k_ref[0, rows, LANES * (h // 2):LANES * (h // 2 + 1)]
        return _dot(kblk, qz_ref[h])

    @pl.when(kv == 0)
    def _init():
        c, s = cos_ref[...], sin_ref[...]
        for h in range(N_HEADS):
            x = q_ref[0, HEAD_DIM * h:HEAD_DIM * (h + 1), :].astype(F32)
            if mode == "A":
                x = _rmsnorm_t(x, qn_ref[...])
            xb = (_rope_t(x, c, s) * (SM_SCALE * LOG2E)).astype(BF16)
            z = jnp.zeros_like(xb)
            half = (h // A_GROUP) if mode == "A" else (h % 2)
            qz_ref[h] = jnp.concatenate([xb, z] if half == 0 else [z, xb], axis=0)
            m_ref[h:h + 1, :] = jnp.max(scores(h, slice(0, STAB_KEYS)), axis=0, keepdims=True)
        acc_ref[0] = jnp.zeros(acc_ref.shape[1:], F32)

    acc_in = acc_ref.at[kv % 2]
    acc_out = acc_ref.at[(kv + 1) % 2]
    ones = jnp.ones((SUM_ROWS, k_ref.shape[1]), BF16)

    def values(h):
        if mode == "A":
            g = h // A_GROUP
            vt = v_ref[0, HEAD_DIM * g:HEAD_DIM * (g + 1), :]
        else:
            hh = h % B_HEADS
            vt = v_ref[0, B_V_DIM * hh:B_V_DIM * (hh + 1), :]
        return jnp.concatenate([vt, ones], axis=0)

    excess = None
    for h in range(N_HEADS):
        s = scores(h)
        m = m_ref[h:h + 1, :]
        d = jnp.max(s, axis=0, keepdims=True) - m
        excess = d if excess is None else jnp.maximum(excess, d)
        p = jnp.exp2(s - m).astype(BF16)
        acc_out[h] = acc_in[h] + _dot(values(h), p)
    redo_ref[0] = (jnp.max(excess) > STAB_MARGIN).astype(jnp.int32)

    @pl.when(redo_ref[0] != 0)
    def _exact_pass():
        for h in range(N_HEADS):
            s = scores(h)
            m_prev = m_ref[h:h + 1, :]
            m_new = jnp.maximum(m_prev, jnp.max(s, axis=0, keepdims=True))
            alpha = jnp.exp2(m_prev - m_new)
            p = jnp.exp2((s - m_new).astype(BF16))
            acc_out[h] = alpha * acc_in[h] + _dot(values(h), p)
            m_ref[h:h + 1, :] = m_new

    @pl.when(kv == nkv - 1)
    def _fin():
        acc = acc_ref.at[nkv % 2]

        def normalized(h):
            return acc[h, 0:dv, :] * (1.0 / acc[h, dv:dv + 1, :])

        outs = []
        if mode == "A":
            for h in range(N_HEADS):
                outs.append(normalized(h))
        else:
            lv = lamv_ref[...]
            lam = (jnp.exp(jnp.sum(lv[0:1] * lv[1:2], axis=1, keepdims=True))
                   - jnp.exp(jnp.sum(lv[2:3] * lv[3:4], axis=1, keepdims=True)) + lam_init)
            for hh in range(B_HEADS):
                o = normalized(hh) - lam * normalized(hh + B_HEADS)
                outs.append(_rmsnorm_t(o, sub_ref[...]) * (1.0 - lam_init))
        o_ref[0] = jnp.concatenate(outs, axis=0).T.astype(BF16)


def _attention(proj_t, k_a, k_b, cos_t, sin_t, qn, lam_vecs, subln, lam_init, n_lat, ctx_into=None):
    b, _, n_tok = proj_t.shape
    is_ctx = ctx_into is not None
    if is_ctx:
        tq = tk = n_tok - n_lat
        nq, nkv = 1, 1
        q_off = k_off = n_lat // tq
    else:
        tq, tk = Q_TILE, KV_TILE
        nq, nkv = n_lat // tq, n_tok // tk
        q_off = k_off = 0
    q_spec = lambda blk: pl.BlockSpec((1, 512, tq), lambda bi, i, j: (bi, blk, i + q_off))
    k_spec = lambda w: pl.BlockSpec((1, tk, w), lambda bi, i, j: (bi, j + k_off, 0))
    v_spec = lambda w, blk: pl.BlockSpec((1, w, tk), lambda bi, i, j: (bi, blk, j + k_off))
    rope_spec = pl.BlockSpec((HALF, tq), lambda bi, i, j: (0, i + q_off))
    full = lambda a: pl.BlockSpec(a.shape, lambda bi, i, j: (0, 0))
    in_specs = [q_spec(0), k_spec(128), v_spec(128, 16), q_spec(1), k_spec(512), v_spec(512, 2),
                rope_spec, rope_spec, full(qn), full(lam_vecs), full(subln)]
    args = [proj_t, k_a, proj_t, proj_t, k_b, proj_t, cos_t, sin_t, qn, lam_vecs, subln]
    aliases = {}
    if is_ctx:
        in_specs += [pl.BlockSpec(memory_space=pl.ANY)] * 2
        args += list(ctx_into)
        aliases = {len(args) - 2: 0, len(args) - 1: 1}
    o_spec = pl.BlockSpec((1, tq, 512), lambda bi, i, j: (bi, i + q_off, 0))
    o_shape = jax.ShapeDtypeStruct((b, n_tok, 512), BF16)
    scratch = []
    for dv in (HEAD_DIM, B_V_DIM):
        scratch += [pltpu.VMEM((N_HEADS, LANES, tq), BF16),
                    pltpu.VMEM((N_HEADS, tq), F32),
                    pltpu.VMEM((2, N_HEADS, dv + SUM_ROWS, tq), F32),
                    pltpu.SMEM((1,), jnp.int32)]
    return pl.pallas_call(
        functools.partial(_attn_kernel, nkv=nkv, lam_init=lam_init, aliased=is_ctx),
        grid=(b, nq, nkv),
        in_specs=in_specs,
        out_specs=[o_spec, o_spec],
        out_shape=[o_shape, o_shape],
        scratch_shapes=scratch,
        input_output_aliases=aliases,
        compiler_params=_params(("arbitrary", "arbitrary", "arbitrary")),
        name="attn_ctx" if is_ctx else "attn_lat",
    )(*args)


def _nbr_tables(rows):
    nblk = rows // NBR_ROWS
    combos, index = {}, np.zeros((3, NBR_ROWS, BAND_ROWS // 2), np.int32)
    for v, blk in enumerate((0, 1, nblk - 1)):
        bs = min(max(NBR_ROWS * blk - NA_KH // 2, 0), rows - BAND_ROWS)
        for qr in range(NBR_ROWS):
            r = NBR_ROWS * blk + qr
            rs = min(max(r - NA_KH // 2, 0), rows - NA_KH)
            for kp in range(BAND_ROWS // 2):
                codes = []
                for kr in (bs + 2 * kp, bs + 2 * kp + 1):
                    codes.append(kr - r + NA_KH - 1 if rs <= kr < rs + NA_KH else 15)
                index[v, qr, kp] = combos.setdefault(tuple(codes), len(combos))
    lo = np.array([c[0] for c in combos], np.int32)
    hi = np.array([c[1] for c in combos], np.int32)
    return index.reshape(-1), lo, hi


def _nbr_bias_table(rpb, lo, hi):
    nh = rpb.shape[0]
    qc = np.arange(GRID_W)[:, None]
    kc = np.arange(GRID_W)[None, :]
    cs = np.clip(qc - NA_KW // 2, 0, GRID_W - NA_KW)
    inside = (kc >= cs) & (kc < cs + NA_KW)
    dc = np.clip(kc - qc + NA_KW - 1, 0, 2 * NA_KW - 2)
    onehot = (dc.reshape(-1)[None, :] == np.arange(2 * NA_KW - 1)[:, None]).astype(np.float32)
    toep = jnp.einsum("hrc,cq->hrq", rpb, jnp.asarray(onehot), precision=HIGHEST)
    toep = toep.reshape(nh, 2 * NA_KH - 1, GRID_W, GRID_W)
    toep = jnp.where(jnp.asarray(inside)[None, None], toep, NEG)
    toep = jnp.concatenate([toep, jnp.full((nh, 1, GRID_W, GRID_W), NEG, F32)], axis=1)
    return jnp.concatenate([toep[:, lo], toep[:, hi]], axis=-1)


def _pair_heads(q, lane):
    zero = jnp.zeros_like(q)
    return jnp.where(lane < HEAD_DIM, q, zero), jnp.where(lane >= HEAD_DIM, q, zero)


def _nbr_kernel(idx_ref, q_ref, k0, k1, k2, k3, v0, v1, v2, v3, kc_ref, vc_ref, tp_ref,
                o_ref, s_ref, *, nblk):
    blk = pl.program_id(1)
    variant = jnp.where(blk == 0, 0, jnp.where(blk == nblk - 1, 2, 1))
    lane = lax.broadcasted_iota(jnp.int32, (1, LANES), 1)
    outs = []
    for p in range(N_HEADS // 2):
        cols = slice(LANES * p, LANES * (p + 1))
        kband = jnp.concatenate([k0[0, :, cols], k1[0, :, cols], k2[0, :, cols], k3[0, :, cols]], axis=0)
        vband = jnp.concatenate([v0[0, :, cols], v1[0, :, cols], v2[0, :, cols], v3[0, :, cols]], axis=0)
        kctx, vctx = kc_ref[0, :, cols], vc_ref[0, :, cols]
        qp = q_ref[0, :, cols] * jnp.asarray(SM_SCALE, BF16)
        res = []
        for e, qm in enumerate(_pair_heads(qp, lane)):
            h = 2 * p + e
            s_raw = _dot_nt(qm, kband)
            sc = _dot_nt(qm, kctx)
            for qr in range(NBR_ROWS):
                rws = slice(GRID_W * qr, GRID_W * (qr + 1))
                for kp in range(BAND_ROWS // 2):
                    u = idx_ref[variant * (NBR_ROWS * BAND_ROWS // 2) + qr * (BAND_ROWS // 2) + kp]
                    cl = slice(LANES * kp, LANES * (kp + 1))
                    s_ref[rws, cl] = s_raw[rws, cl] + tp_ref[h, u]
            s = s_ref[...]
            m = jnp.maximum(jnp.max(s, axis=-1, keepdims=True), jnp.max(sc, axis=-1, keepdims=True))
            pn = jnp.exp(s - m)
            pc = jnp.exp(sc - m)
            l = jnp.sum(pn, axis=-1, keepdims=True) + jnp.sum(pc, axis=-1, keepdims=True)
            o = _dot(pn.astype(BF16), vband) + _dot(pc.astype(BF16), vctx)
            res.append(o * (1.0 / l))
        outs.append(jnp.where(lane < HEAD_DIM, res[0], res[1]))
    o_ref[0] = jnp.concatenate(outs, axis=1).astype(BF16)


def _nbr_attention(proj_n, idx, tp, n_lat):
    b, n_tok, _ = proj_n.shape
    rows = n_lat // GRID_W
    nblk = rows // NBR_ROWS
    tq = NBR_ROWS * GRID_W
    blk_rows = CHUNK // GRID_W
    n_band = BAND_ROWS // blk_rows
    max_start = rows // blk_rows - n_band
    ctx_blk = n_lat // CHUNK
    q_col, k_col, v_col = 6, 7, 8

    def band_spec(j, col):
        def imap(bi, i, idx_ref):
            start = jnp.clip(2 * i - 1, 0, max_start)
            return (bi, start + j, col)
        return pl.BlockSpec((1, CHUNK, 512), imap)

    in_specs = [pl.BlockSpec((1, tq, 512), lambda bi, i, idx_ref: (bi, i, q_col))]
    in_specs += [band_spec(j, k_col) for j in range(n_band)]
    in_specs += [band_spec(j, v_col) for j in range(n_band)]
    in_specs += [pl.BlockSpec((1, CHUNK, 512), lambda bi, i, idx_ref: (bi, ctx_blk, k_col)),
                 pl.BlockSpec((1, CHUNK, 512), lambda bi, i, idx_ref: (bi, ctx_blk, v_col)),
                 pl.BlockSpec(tp.shape, lambda bi, i, idx_ref: (0, 0, 0, 0))]
    grid_spec = pltpu.PrefetchScalarGridSpec(
        num_scalar_prefetch=1, grid=(b, nblk), in_specs=in_specs,
        out_specs=pl.BlockSpec((1, tq, 512), lambda bi, i, idx_ref: (bi, i, 0)),
        scratch_shapes=[pltpu.VMEM((tq, BAND_ROWS * GRID_W), F32)])
    return pl.pallas_call(
        functools.partial(_nbr_kernel, nblk=nblk),
        grid_spec=grid_spec,
        out_shape=jax.ShapeDtypeStruct((b, n_tok, 512), BF16),
        compiler_params=_params(("arbitrary", "arbitrary")),
        name="nbr_attn",
    )(idx, *([proj_n] * (1 + 2 * n_band + 2)), tp)


def _ctx_c_kernel(q_ref, k_ref, v_ref, prev_ref, o_ref):
    del prev_ref
    lane = lax.broadcasted_iota(jnp.int32, (1, LANES), 1)
    outs = []
    for p in range(N_HEADS // 2):
        cols = slice(LANES * p, LANES * (p + 1))
        kp_, vp = k_ref[0, :, cols], v_ref[0, :, cols]
        qp = q_ref[0, :, cols] * jnp.asarray(SM_SCALE, BF16)
        res = []
        for qm in _pair_heads(qp, lane):
            s = _dot_nt(qm, kp_)
            m = jnp.max(s, axis=-1, keepdims=True)
            pr = jnp.exp(s - m)
            l = jnp.sum(pr, axis=-1, keepdims=True)
            res.append(_dot(pr.astype(BF16), vp) * (1.0 / l))
        outs.append(jnp.where(lane < HEAD_DIM, res[0], res[1]))
    o_ref[0] = jnp.concatenate(outs, axis=1).astype(BF16)


def _ctx_c_attention(proj_n, o_c, n_lat):
    b, n_tok, _ = proj_n.shape
    n_ctx = n_tok - n_lat
    blk = n_lat // n_ctx
    return pl.pallas_call(
        _ctx_c_kernel,
        grid=(b,),
        in_specs=[pl.BlockSpec((1, n_ctx, 512), lambda bi: (bi, blk, 6)),
                  pl.BlockSpec((1, n_ctx, 512), lambda bi: (bi, blk, 7)),
                  pl.BlockSpec((1, n_ctx, 512), lambda bi: (bi, blk, 8)),
                  pl.BlockSpec(memory_space=pl.ANY)],
        out_specs=pl.BlockSpec((1, n_ctx, 512), lambda bi: (bi, blk, 0)),
        out_shape=jax.ShapeDtypeStruct(o_c.shape, BF16),
        input_output_aliases={3: 0},
        compiler_params=_params(("arbitrary",)),
        name="ctx_c_attn",
    )(proj_n, proj_n, proj_n, o_c)


def _merge_kernel(*refs, with_router):
    (oa_ref, ob_ref, oc_ref, gate_ref, x_ref, wa_ref, wb_ref, wc_ref, wo_ref,
     mod_ref, g_ref) = refs[:11]
    if with_router:
        r_ref, xo_ref, h_ref, route_ref = refs[11:]
    else:
        xo_ref, h_ref = refs[11:]
    d = D_MODEL
    gate = lambda k: _sigmoid(gate_ref[0, :, k * d:(k + 1) * d].astype(F32))
    m = (gate(0) * _dot(oa_ref[0], wa_ref[...])
         + gate(1) * _dot(ob_ref[0], wb_ref[...])
         + gate(2) * _dot(oc_ref[0], wc_ref[...]))
    y = _dot(m.astype(BF16), wo_ref[...])
    for c in range(ROW_TILE // CHUNK):
        rows = slice(c * CHUNK, (c + 1) * CHUNK)
        xn = x_ref[0, rows, :] + mod_ref[0, c, 2:3, :] * y[rows, :]
        xo_ref[0, rows, :] = xn
        h = _norm_mod(xn, g_ref[...], mod_ref[0, c, 3:4, :], mod_ref[0, c, 4:5, :])
        if not with_router:
            h_ref[0, rows, :] = h.astype(BF16)
        else:
            h_ref[0, rows, :] = h
            lane = lax.broadcasted_iota(jnp.int32, (CHUNK, LANES), 1)
            logits = jnp.where(lane < N_EXPERTS, _dot(h, r_ref[...], precision=HIGHEST), NEG)
            m1 = jnp.max(logits, axis=-1, keepdims=True)
            i1 = jnp.min(jnp.where(logits == m1, lane, LANES), axis=-1, keepdims=True)
            rest = jnp.where(lane == i1, NEG, logits)
            m2 = jnp.max(rest, axis=-1, keepdims=True)
            i2 = jnp.min(jnp.where(rest == m2, lane, LANES), axis=-1, keepdims=True)
            e = jnp.exp(m2 - m1)
            w1 = 1.0 / (1.0 + e)
            route_ref[0, rows, :] = jnp.where(
                lane == 0, i1.astype(F32), jnp.where(
                    lane == 1, i2.astype(F32), jnp.where(
                        lane == 2, w1, jnp.where(lane == 3, e * w1, 0.0))))


def _merge(o_a, o_b, o_c, proj_n, x_all, wa, wb, wc, wo, modc, g_ffn, router):
    b, n_tok, d = x_all.shape
    nt = n_tok // ROW_TILE
    with_router = router is not None
    tile = lambda w: pl.BlockSpec((1, ROW_TILE, w), lambda bi, i: (bi, i, 0))
    full = lambda a: pl.BlockSpec(a.shape, lambda bi, i: (0,) * a.ndim)
    in_specs = [tile(512), tile(512), tile(512), tile(3 * d), tile(d),
                full(wa), full(wb), full(wc), full(wo),
                pl.BlockSpec((1, ROW_TILE // CHUNK, 6, d), lambda bi, i: (bi, i, 0, 0)),
                full(g_ffn)]
    args = [o_a, o_b, o_c, proj_n, x_all, wa, wb, wc, wo, modc, g_ffn]
    out_specs = [tile(d), tile(d)]
    out_shape = [jax.ShapeDtypeStruct((b, n_tok, d), F32), jax.ShapeDtypeStruct((b, n_tok, d), BF16)]
    if with_router:
        in_specs.append(full(router))
        args.append(router)
        out_shape[1] = jax.ShapeDtypeStruct((b, n_tok, d), F32)
        out_specs.append(tile(LANES))
        out_shape.append(jax.ShapeDtypeStruct((b, n_tok, LANES), F32))
    return pl.pallas_call(
        functools.partial(_merge_kernel, with_router=with_router),
        grid=(b, nt),
        in_specs=in_specs,
        out_specs=out_specs,
        out_shape=out_shape,
        compiler_params=_params(("arbitrary", "arbitrary")),
        name="merge_router" if with_router else "merge",
    )(*args)


def _swiglu_partial(h, w1, w3, w2):
    a = _dot(h, w1)
    t = (a * _sigmoid(a)) * _dot(h, w3)
    return _dot(t.astype(BF16), w2)


def _ffn_kernel(h_ref, x_ref, w1_ref, w3_ref, w2_ref, mod_ref, o_ref, acc_ref, *, n_f):
    @pl.when(pl.program_id(2) == 0)
    def _():
        acc_ref[...] = jnp.zeros(acc_ref.shape, F32)

    acc_ref[...] += _swiglu_partial(h_ref[0], w1_ref[0], w3_ref[0], w2_ref[0])

    @pl.when(pl.program_id(2) == n_f - 1)
    def _():
        for c in range(ROW_TILE // CHUNK):
            rows = slice(c * CHUNK, (c + 1) * CHUNK)
            o_ref[0, rows, :] = x_ref[0, rows, :] + mod_ref[0, c, 5:6, :] * acc_ref[rows, :]


def _ff_tile(f, cap):
    return max(t for t in range(LANES, min(f, cap) + 1, LANES) if f % t == 0)


def _ffn(h2, x_all, w1, w3, w2, li, modc):
    b, n_tok, d = x_all.shape
    nt = n_tok // ROW_TILE
    f = w1.shape[-1]
    tf = _ff_tile(f, 1408)
    n_f = f // tf
    tile = lambda w: pl.BlockSpec((1, ROW_TILE, w), lambda bi, i, j: (bi, i, 0))
    w13 = pl.BlockSpec((1, d, tf), lambda bi, i, j: (li, 0, j))
    w2s = pl.BlockSpec((1, tf, d), lambda bi, i, j: (li, j, 0))
    mods = pl.BlockSpec((1, ROW_TILE // CHUNK, 6, d), lambda bi, i, j: (bi, i, 0, 0))
    return pl.pallas_call(
        functools.partial(_ffn_kernel, n_f=n_f),
        grid=(b, nt, n_f),
        in_specs=[tile(d), tile(d), w13, w13, w2s, mods],
        out_specs=tile(d),
        out_shape=jax.ShapeDtypeStruct((b, n_tok, d), F32),
        scratch_shapes=[pltpu.VMEM((ROW_TILE, d), F32)],
        compiler_params=_params(("arbitrary",) * 3),
        name="ffn",
    )(h2, x_all, w1, w3, w2, modc)


def _row_copy(src_ref, dst_ref, src_row, dst_row, sem):
    return pltpu.make_async_copy(src_ref.at[pl.ds(src_row, 1), :], dst_ref.at[pl.ds(dst_row, 1), :], sem)


def _gather_start(idx_ref, idx_base, n_rows, src_ref, dst_ref, sem):
    def issue(j, carry):
        for prio in range(DMA_PRIORITIES):
            i = DMA_PRIORITIES * j + prio
            _row_copy(src_ref, dst_ref, idx_ref[idx_base + i], i, sem).start(priority=prio)
        return carry

    lax.fori_loop(0, n_rows // DMA_PRIORITIES, issue, 0, unroll=4)


def _gather_wait(n_rows, src_ref, dst_ref, sem):
    def drain(i, carry):
        _row_copy(src_ref, dst_ref, 0, i, sem).wait()
        return carry

    lax.fori_loop(0, n_rows, drain, 0, unroll=8)


def _dispatch_kernel(idx_ref, src_ref, o_ref, sem):
    _gather_start(idx_ref, pl.program_id(0) * MOE_TILE, MOE_TILE, src_ref, o_ref, sem)
    _gather_wait(MOE_TILE, src_ref, o_ref, sem)


def _dispatch(h_rows, slot_token):
    n_slots = slot_token.shape[0]
    d = h_rows.shape[1]
    grid_spec = pltpu.PrefetchScalarGridSpec(
        num_scalar_prefetch=1, grid=(n_slots // MOE_TILE,),
        in_specs=[pl.BlockSpec(memory_space=pl.ANY)],
        out_specs=pl.BlockSpec((MOE_TILE, d), lambda t, idx: (t, 0)),
        scratch_shapes=[pltpu.SemaphoreType.DMA])
    return pl.pallas_call(
        _dispatch_kernel,
        grid_spec=grid_spec,
        out_shape=jax.ShapeDtypeStruct((n_slots, d), F32),
        compiler_params=_params(("arbitrary",)),
        name="moe_dispatch",
    )(slot_token, h_rows)


def _expert_kernel(te_ref, tv_ref, xg_ref, w1_ref, w3_ref, w2_ref, y_ref, h_ref, acc_ref, *, n_f):
    del te_ref
    t, f = pl.program_id(0), pl.program_id(1)

    @pl.when(tv_ref[t] != 0)
    def _():
        @pl.when(f == 0)
        def _():
            h_ref[...] = xg_ref[...].astype(BF16)
            acc_ref[...] = jnp.zeros(acc_ref.shape, F32)

        acc_ref[...] += _swiglu_partial(h_ref[...], w1_ref[0, 0], w3_ref[0, 0], w2_ref[0, 0])

        @pl.when(f == n_f - 1)
        def _():
            y_ref[...] = acc_ref[...]

    @pl.when((tv_ref[t] == 0) & (f == n_f - 1))
    def _():
        y_ref[...] = jnp.zeros(y_ref.shape, F32)


def _experts(xg, tile_expert, tile_valid, w1, w3, w2, li):
    n_slots = xg.shape[0]
    d, f = w1.shape[-2:]
    tf = _ff_tile(f, 896)
    n_f = f // tf
    grid_spec = pltpu.PrefetchScalarGridSpec(
        num_scalar_prefetch=2, grid=(n_slots // MOE_TILE, n_f),
        in_specs=[pl.BlockSpec((MOE_TILE, d), lambda t, j, te, tv: (t, 0)),
                  pl.BlockSpec((1, 1, d, tf), lambda t, j, te, tv: (li, te[t], 0, j)),
                  pl.BlockSpec((1, 1, d, tf), lambda t, j, te, tv: (li, te[t], 0, j)),
                  pl.BlockSpec((1, 1, tf, d), lambda t, j, te, tv: (li, te[t], j, 0))],
        out_specs=pl.BlockSpec((MOE_TILE, d), lambda t, j, te, tv: (t, 0)),
        scratch_shapes=[pltpu.VMEM((MOE_TILE, d), BF16), pltpu.VMEM((MOE_TILE, d), F32)])
    return pl.pallas_call(
        functools.partial(_expert_kernel, n_f=n_f),
        grid_spec=grid_spec,
        out_shape=jax.ShapeDtypeStruct((n_slots, d), F32),
        compiler_params=_params(("arbitrary", "arbitrary")),
        name="moe_experts",
    )(tile_expert, tile_valid, xg, w1, w3, w2)


def _combine_kernel(pos_ref, y_ref, x_ref, route_ref, mod_ref, o_ref, g1_ref, g2_ref, sems, *, n_rows):
    base = (pl.program_id(0) * pl.num_programs(1) + pl.program_id(1)) * ROW_TILE
    _gather_start(pos_ref, base, ROW_TILE, y_ref, g1_ref, sems.at[0])
    _gather_start(pos_ref, n_rows + base, ROW_TILE, y_ref, g2_ref, sems.at[1])
    _gather_wait(ROW_TILE, y_ref, g1_ref, sems.at[0])
    _gather_wait(ROW_TILE, y_ref, g2_ref, sems.at[1])
    for c in range(ROW_TILE // CHUNK):
        rows = slice(c * CHUNK, (c + 1) * CHUNK)
        w1 = route_ref[0, rows, 2:3]
        w2 = route_ref[0, rows, 3:4]
        y = w1 * g1_ref[rows, :] + w2 * g2_ref[rows, :]
        o_ref[0, rows, :] = x_ref[0, rows, :] + mod_ref[0, c, 5:6, :] * y


def _combine(y, pos, x_mid, route, modc):
    b, n_tok, d = x_mid.shape
    nt = n_tok // ROW_TILE
    tile = lambda w: pl.BlockSpec((1, ROW_TILE, w), lambda bi, i, p: (bi, i, 0))
    grid_spec = pltpu.PrefetchScalarGridSpec(
        num_scalar_prefetch=1, grid=(b, nt),
        in_specs=[pl.BlockSpec(memory_space=pl.ANY), tile(d), tile(LANES),
                  pl.BlockSpec((1, ROW_TILE // CHUNK, 6, d), lambda bi, i, p: (bi, i, 0, 0))],
        out_specs=tile(d),
        scratch_shapes=[pltpu.VMEM((ROW_TILE, d), F32),
                        pltpu.VMEM((ROW_TILE, d), F32),
                        pltpu.SemaphoreType.DMA((2,))])
    return pl.pallas_call(
        functools.partial(_combine_kernel, n_rows=b * n_tok),
        grid_spec=grid_spec,
        out_shape=jax.ShapeDtypeStruct((b, n_tok, d), F32),
        compiler_params=_params(("arbitrary", "arbitrary")),
        name="moe_combine",
    )(pos, y, x_mid, route, modc)


def _routing_tables(route):
    n_rows = route.shape[0] * route.shape[1]
    r2 = route.reshape(n_rows, LANES)
    e_flat = jnp.concatenate([r2[:, 0], r2[:, 1]]).astype(jnp.int32)
    onehot = (e_flat[:, None] == jnp.arange(N_EXPERTS, dtype=jnp.int32)[None, :]).astype(jnp.int32)
    csum = jnp.cumsum(onehot, axis=0)
    rank = jnp.take_along_axis(csum, e_flat[:, None], axis=1)[:, 0] - 1
    padded = ((csum[-1] + MOE_TILE - 1) // MOE_TILE) * MOE_TILE
    ends = jnp.cumsum(padded)
    pos = (ends - padded)[e_flat] + rank
    n_slots = 2 * n_rows + N_EXPERTS * MOE_TILE
    token = jnp.arange(2 * n_rows, dtype=jnp.int32) % n_rows
    slot_token = jnp.zeros((n_slots,), jnp.int32).at[pos].set(token, unique_indices=True, mode="promise_in_bounds")
    tile_start = jnp.arange(n_slots // MOE_TILE, dtype=jnp.int32) * MOE_TILE
    tile_expert = jnp.minimum(jnp.searchsorted(ends, tile_start, side="right"), N_EXPERTS - 1)
    tile_valid = (tile_start < ends[-1]).astype(jnp.int32)
    return slot_token, tile_expert.astype(jnp.int32), tile_valid, pos.astype(jnp.int32)


def _moe(h_rows, x_mid, route, w1, w3, w2, li, modc):
    b, n_tok = x_mid.shape[:2]
    slot_token, tile_expert, tile_valid, pos = _routing_tables(route)
    xg = _dispatch(h_rows.reshape(b * n_tok, -1), slot_token)
    y = _experts(xg, tile_expert, tile_valid, w1, w3, w2, li)
    return _combine(y, pos, x_mid, route, modc)


def _final_kernel(x_ref, g_ref, o_ref):
    x = x_ref[0]
    ms = jnp.mean(x * x, axis=-1, keepdims=True)
    o_ref[0] = x * lax.rsqrt(ms + EPS) * g_ref[...]


def _final_norm(x_all, g, n_lat):
    b, _, d = x_all.shape
    tm = 1024
    return pl.pallas_call(
        _final_kernel,
        grid=(b, n_lat // tm),
        in_specs=[pl.BlockSpec((1, tm, d), lambda bi, i: (bi, i, 0)),
                  pl.BlockSpec((1, d), lambda bi, i: (0, 0))],
        out_specs=pl.BlockSpec((1, tm, d), lambda bi, i: (bi, i, 0)),
        out_shape=jax.ShapeDtypeStruct((b, n_lat, d), F32),
        compiler_params=_params(("arbitrary", "arbitrary")),
        name="final_norm",
    )(x_all, g)


def _deinterleave(w, n_heads):
    lead = w.shape[:-1]
    w = w.reshape(lead + (n_heads, HALF, 2))
    return jnp.swapaxes(w, -1, -2).reshape(lead + (n_heads * HEAD_DIM,))


def _rope_tables(n_lat, n_tok):
    t = np.arange(n_lat)
    inv = ROPE_THETA ** (-np.arange(HEAD_DIM // 4, dtype=np.float64) / (HEAD_DIM // 4))
    ang = np.concatenate([(t // GRID_W)[:, None] * inv, (t % GRID_W)[:, None] * inv], axis=-1)
    ang = np.concatenate([ang, np.zeros((n_tok - n_lat, HALF))], axis=0)
    return jnp.asarray(np.cos(ang).T, F32), jnp.asarray(np.sin(ang).T, F32)


def kernel(x, c, ctx, c_ctx, w_mod, b_mod, g_mix, g_ffn, w_in, qn_a, kn_a, lam_q1, lam_k1, lam_q2,
           lam_k2, subln_b, rpb_c, w_br_a, w_br_b, w_br_c, w_out, ffn_w1, ffn_w3, ffn_w2, router,
           moe_w1, moe_w3, moe_w2, g_final):
    b, n_lat, d = x.shape
    n_ctx = ctx.shape[1]
    n_tok = n_lat + n_ctx
    depth = w_in.shape[0]
    assert d == D_MODEL and n_ctx == CHUNK and b + 1 <= 8
    assert n_tok % ROW_TILE == 0 and n_tok % KV_TILE == 0 and n_lat % Q_TILE == 0
    assert n_lat % (NBR_ROWS * GRID_W) == 0 and n_lat // GRID_W >= 24
    n_chunks = n_tok // CHUNK

    x_all = jnp.concatenate([x, ctx], axis=1)
    cos_t, sin_t = _rope_tables(n_lat, n_tok)

    c_rows = jnp.concatenate([c, c_ctx[None], jnp.zeros((7 - b, d), F32)], axis=0)
    mods = _mod_vectors(c_rows, w_mod, b_mod).reshape(depth, 8, 6, d)
    mod_lat = jnp.broadcast_to(mods[:, :b, None], (depth, b, n_chunks - 1, 6, d))
    mod_ctx = jnp.broadcast_to(mods[:, b:b + 1, None], (depth, b, 1, 6, d))
    mod_chunks = jnp.concatenate([mod_lat, mod_ctx], axis=2)

    sl = lambda name: w_in[:, :, _OFF[name][0]:_OFF[name][1]]
    w_n = jnp.concatenate([sl("gate"), sl("cq"), sl("ck"), sl("cv")], axis=-1).astype(BF16)
    w_t = jnp.concatenate([_deinterleave(sl("aq"), 8), _deinterleave(sl("bq"), 8), sl("bv"),
                           _deinterleave(sl("bk"), 8), sl("av"), _deinterleave(sl("ak"), 2)], axis=-1)
    w_t = jnp.swapaxes(w_t, 1, 2).astype(BF16)
    qn_t = _deinterleave(qn_a, 1)[:, :, None]
    kn_t = _deinterleave(kn_a, 1)[:, :, None]
    lam_vecs = jnp.pad(jnp.stack([lam_q1, lam_k1, lam_q2, lam_k2], axis=1),
                       ((0, 0), (0, 4), (0, LANES - HEAD_DIM)))
    wa, wb, wc, wo = (w.astype(BF16) for w in (w_br_a, w_br_b, w_br_c, w_out))
    f1, f3, f2 = (w.astype(BF16) for w in (ffn_w1, ffn_w3, ffn_w2))
    m1, m3, m2 = (w.astype(BF16) for w in (moe_w1, moe_w3, moe_w2))
    router_p = jnp.pad(router, ((0, 0), (0, 0), (0, LANES - N_EXPERTS)))
    nbr_idx, nbr_lo, nbr_hi = _nbr_tables(n_lat // GRID_W)
    nbr_idx = jnp.asarray(nbr_idx)

    for layer in range(depth):
        lam_init = 0.8 - 0.6 * math.exp(-0.3 * layer)
        modc = mod_chunks[layer]
        g_mix_l = g_mix[layer][None]
        proj_n = _inproj(x_all, g_mix_l, modc, w_n[layer])
        proj_t, k_b, k_a = _inproj(x_all, g_mix_l, modc, w_t[layer], rope=(cos_t, sin_t, kn_t[layer]))

        attn_args = (proj_t, k_a, k_b, cos_t, sin_t, qn_t[layer], lam_vecs[layer], subln_b[layer][:, None],
                     lam_init, n_lat)
        o_a, o_b = _attention(*attn_args, ctx_into=_attention(*attn_args))
        tp = _nbr_bias_table(rpb_c[layer], nbr_lo, nbr_hi)
        o_c = _nbr_attention(proj_n, nbr_idx, tp, n_lat)
        o_c = _ctx_c_attention(proj_n, o_c, n_lat)

        is_moe = layer % 2 == 1
        li = layer // 2
        merged = _merge(o_a, o_b, o_c, proj_n, x_all, wa[layer], wb[layer], wc[layer], wo[layer],
                        modc, g_ffn[layer][None], router_p[li] if is_moe else None)
        if is_moe:
            x_mid, h_rows, route = merged
            x_all = _moe(h_rows, x_mid, route, m1, m3, m2, li, modc)
        else:
            x_mid, h2 = merged
            x_all = _ffn(h2, x_mid, f1, f3, f2, li, modc)

    return _final_norm(x_all, g_final[None], n_lat)
```

```python
import functools
import math

import numpy as np
import jax
import jax.numpy as jnp
from jax import lax
from jax.experimental import pallas as pl
from jax.experimental.pallas import tpu as pltpu

F32 = jnp.float32
BF16 = jnp.bfloat16
HIGHEST = lax.Precision.HIGHEST

D_MODEL = 1024
HEAD_DIM = 64
HALF = HEAD_DIM // 2
GRID_W = 64
ROPE_THETA = 10000.0
EPS = 1e-6
N_HEADS = 8
A_GROUP = 4
B_HEADS = 4
B_V_DIM = 128
NA_KH = 8
NA_KW = 16
N_EXPERTS = 8
LANES = 128
CHUNK = 256
ROW_TILE = 768
INPROJ_COLS = 2304
Q_TILE = 1024
KV_TILE = 768
MOE_TILE = 512
DMA_PRIORITIES = 2
NBR_ROWS = 8
BAND_ROWS = 16
NEG = -1e30
SM_SCALE = HEAD_DIM ** -0.5
LOG2E = math.log2(math.e)
SUM_ROWS = 16
STAB_MARGIN = 64.0
STAB_KEYS = 128
VMEM_LIMIT = 56 * 1024 * 1024

_OFF = {}
_o = 0
for _name, _size in (("aq", 512), ("ak", 128), ("av", 128), ("bq", 512), ("bk", 512),
                     ("bv", 512), ("cq", 512), ("ck", 512), ("cv", 512), ("gate", 3072)):
    _OFF[_name] = (_o, _o + _size)
    _o += _size
D_IN = _o
N_COLS = 4608
T_COLS = 2304
KB_ROW0, KA_ROW0 = 1536, 2176


def _params(sem):
    return pltpu.CompilerParams(dimension_semantics=sem, vmem_limit_bytes=VMEM_LIMIT)


def _sigmoid(x):
    return 1.0 / (1.0 + jnp.exp(-x))


def _dot(a, b, **kw):
    return jnp.dot(a, b, preferred_element_type=F32, **kw)


def _dot_nt(a, b):
    return lax.dot_general(a, b, (((1,), (1,)), ((), ())), preferred_element_type=F32)


def _norm_mod(x, g, shift, scale):
    ms = jnp.mean(x * x, axis=-1, keepdims=True)
    y = x * lax.rsqrt(ms + EPS) * g
    return y * (1.0 + scale) + shift


def _mod_kernel(c_ref, w_ref, b_ref, o_ref):
    c = c_ref[...]
    s = c * _sigmoid(c)
    o_ref[0] = _dot(s, w_ref[0], precision=HIGHEST) + b_ref[0]


def _mod_vectors(c_rows, w_mod, b_mod):
    depth, d, n = w_mod.shape
    tn = 1536
    return pl.pallas_call(
        _mod_kernel,
        grid=(depth, n // tn),
        in_specs=[pl.BlockSpec((8, d), lambda l, j: (0, 0)),
                  pl.BlockSpec((1, d, tn), lambda l, j: (l, 0, j)),
                  pl.BlockSpec((1, 1, tn), lambda l, j: (l, 0, j))],
        out_specs=pl.BlockSpec((1, 8, tn), lambda l, j: (l, 0, j)),
        out_shape=jax.ShapeDtypeStruct((depth, 8, n), F32),
        compiler_params=_params(("arbitrary", "arbitrary")),
        name="mod_vectors",
    )(c_rows, w_mod, b_mod.reshape(depth, 1, n))


def _rope_t(x, c, s):
    x1, x2 = x[:HALF], x[HALF:]
    return jnp.concatenate([x1 * c - x2 * s, x1 * s + x2 * c], axis=0)


def _rmsnorm_t(x, g):
    ms = jnp.mean(x * x, axis=0, keepdims=True)
    return x * lax.rsqrt(ms + EPS) * g


def _modulated_tile(x_ref, g_ref, mod_ref):
    chunks = []
    for c in range(ROW_TILE // CHUNK):
        rows = slice(c * CHUNK, (c + 1) * CHUNK)
        h = _norm_mod(x_ref[0, rows, :], g_ref[...], mod_ref[0, c, 0:1, :], mod_ref[0, c, 1:2, :])
        chunks.append(h.astype(BF16))
    return jnp.concatenate(chunks, axis=0)


def _inproj_n_kernel(x_ref, g_ref, mod_ref, w_ref, o_ref):
    h = _modulated_tile(x_ref, g_ref, mod_ref)
    for c in range(o_ref.shape[2] // INPROJ_COLS):
        cols = slice(c * INPROJ_COLS, (c + 1) * INPROJ_COLS)
        o_ref[0, :, cols] = _dot(h, w_ref[:, cols]).astype(BF16)


def _inproj_t_kernel(x_ref, g_ref, mod_ref, w_ref, cos_ref, sin_ref, kn_ref, o_ref, kb_ref, ka_ref):
    r = _dot_nt(w_ref[...], _modulated_tile(x_ref, g_ref, mod_ref))
    o_ref[0] = r.astype(BF16)
    c, s = cos_ref[...], sin_ref[...]
    heads_b = [_rope_t(r[KB_ROW0 + HEAD_DIM * h:KB_ROW0 + HEAD_DIM * (h + 1)], c, s) for h in range(N_HEADS)]
    kb_ref[0] = jnp.concatenate(heads_b, axis=0).T.astype(BF16)
    heads_a = [_rope_t(_rmsnorm_t(r[KA_ROW0 + HEAD_DIM * h:KA_ROW0 + HEAD_DIM * (h + 1)], kn_ref[...]), c, s)
               for h in range(N_HEADS // A_GROUP)]
    ka_ref[0] = jnp.concatenate(heads_a, axis=0).T.astype(BF16)


def _inproj(x_all, g, modc, w, rope=None):
    b, n_tok, d = x_all.shape
    nt = n_tok // ROW_TILE
    in_specs = [pl.BlockSpec((1, ROW_TILE, d), lambda bi, i: (bi, i, 0)),
                pl.BlockSpec((1, d), lambda bi, i: (0, 0)),
                pl.BlockSpec((1, ROW_TILE // CHUNK, 6, d), lambda bi, i: (bi, i, 0, 0)),
                pl.BlockSpec(w.shape, lambda bi, i: (0, 0))]
    args = [x_all, g, modc, w]
    tok = lambda width: pl.BlockSpec((1, ROW_TILE, width), lambda bi, i: (bi, i, 0))
    if rope is None:
        kern, name = _inproj_n_kernel, "inproj_n"
        out_specs = tok(w.shape[1])
        out_shape = jax.ShapeDtypeStruct((b, n_tok, w.shape[1]), BF16)
    else:
        kern, name = _inproj_t_kernel, "inproj_t"
        cos_t, sin_t, kn = rope
        in_specs += [pl.BlockSpec((HALF, ROW_TILE), lambda bi, i: (0, i)),
                     pl.BlockSpec((HALF, ROW_TILE), lambda bi, i: (0, i)),
                     pl.BlockSpec((HEAD_DIM, 1), lambda bi, i: (0, 0))]
        args += [cos_t, sin_t, kn]
        out_specs = [pl.BlockSpec((1, w.shape[0], ROW_TILE), lambda bi, i: (bi, 0, i)), tok(512), tok(128)]
        out_shape = [jax.ShapeDtypeStruct((b, w.shape[0], n_tok), BF16),
                     jax.ShapeDtypeStruct((b, n_tok, 512), BF16),
                     jax.ShapeDtypeStruct((b, n_tok, 128), BF16)]
    return pl.pallas_call(
        kern,
        grid=(b, nt),
        in_specs=in_specs,
        out_specs=out_specs,
        out_shape=out_shape,
        compiler_params=_params(("arbitrary", "arbitrary")),
        name=name,
    )(*args)


def _attn_kernel(*refs, nkv, lam_init, aliased):
    qa_ref, ka_ref, va_ref, qb_ref, kb_ref, vb_ref, cos_ref, sin_ref, qn_ref, lamv_ref, sub_ref = refs[:11]
    rest = refs[13:] if aliased else refs[11:]
    oa_ref, ob_ref = rest[:2]
    scratch_a, scratch_b = rest[2:6], rest[6:10]
    mixers = [_attn_phases("A", qa_ref, ka_ref, va_ref, cos_ref, sin_ref, (qn_ref,), oa_ref, *scratch_a,
                           nkv=nkv, lam_init=lam_init),
              _attn_phases("B", qb_ref, kb_ref, vb_ref, cos_ref, sin_ref, (lamv_ref, sub_ref), ob_ref, *scratch_b,
                           nkv=nkv, lam_init=lam_init)]
    for phase in range(3):
        for mixer in mixers:
            mixer[phase]()


def _attn_phases(mode, q_ref, k_ref, v_ref, cos_ref, sin_ref, aux, o_ref, qz_ref, m_ref, acc_ref, redo_ref,
                 *, nkv, lam_init):
    if mode == "A":
        qn_ref, = aux
    else:
        lamv_ref, sub_ref = aux
    kv = pl.program_id(2)
    dv = HEAD_DIM if mode == "A" else B_V_DIM

    def scores(h, rows=slice(None)):
        if mode == "A":
            kblk = k_ref[0, rows, :]
        else:
            kblk = k_ref[0, rows, LANES * (h // 2):LANES * (h // 2 + 1)]
        return _dot(kblk, qz_ref[h])

    def init():
        @pl.when(kv == 0)
        def _():
            c, s = cos_ref[...], sin_ref[...]
            for h in range(N_HEADS):
                x = q_ref[0, HEAD_DIM * h:HEAD_DIM * (h + 1), :].astype(F32)
                if mode == "A":
                    x = _rmsnorm_t(x, qn_ref[...])
                xb = (_rope_t(x, c, s) * (SM_SCALE * LOG2E)).astype(BF16)
                z = jnp.zeros_like(xb)
                half = (h // A_GROUP) if mode == "A" else (h % 2)
                qz_ref[h] = jnp.concatenate([xb, z] if half == 0 else [z, xb], axis=0)
                m_ref[h:h + 1, :] = jnp.max(scores(h, slice(0, STAB_KEYS)), axis=0, keepdims=True)
            acc_ref[0] = jnp.zeros(acc_ref.shape[1:], F32)

    acc_in = acc_ref.at[kv % 2]
    acc_out = acc_ref.at[(kv + 1) % 2]
    ones = jnp.ones((SUM_ROWS, k_ref.shape[1]), BF16)

    def values(h):
        if mode == "A":
            g = h // A_GROUP
            vt = v_ref[0, HEAD_DIM * g:HEAD_DIM * (g + 1), :]
        else:
            hh = h % B_HEADS
            vt = v_ref[0, B_V_DIM * hh:B_V_DIM * (hh + 1), :]
        return jnp.concatenate([vt, ones], axis=0)

    def single_pass():
        excess = None
        for h in range(N_HEADS):
            s = scores(h)
            m = m_ref[h:h + 1, :]
            d = jnp.max(s, axis=0, keepdims=True) - m
            excess = d if excess is None else jnp.maximum(excess, d)
            p = jnp.exp2(s - m).astype(BF16)
            acc_out[h] = acc_in[h] + _dot(values(h), p)
        redo_ref[0] = (jnp.max(excess) > STAB_MARGIN).astype(jnp.int32)

    def post():
        @pl.when(redo_ref[0] != 0)
        def _exact_pass():
            for h in range(N_HEADS):
                s = scores(h)
                m_prev = m_ref[h:h + 1, :]
                m_new = jnp.maximum(m_prev, jnp.max(s, axis=0, keepdims=True))
                alpha = jnp.exp2(m_prev - m_new)
                p = jnp.exp2(s - m_new).astype(BF16)
                acc_out[h] = alpha * acc_in[h] + _dot(values(h), p)
                m_ref[h:h + 1, :] = m_new

        @pl.when(kv == nkv - 1)
        def _fin():
            acc = acc_ref.at[nkv % 2]

            def normalized(h):
                return acc[h, 0:dv, :] * (1.0 / acc[h, dv:dv + 1, :])

            outs = []
            if mode == "A":
                for h in range(N_HEADS):
                    outs.append(normalized(h))
            else:
                lv = lamv_ref[...]
                lam = (jnp.exp(jnp.sum(lv[0:1] * lv[1:2], axis=1, keepdims=True))
                       - jnp.exp(jnp.sum(lv[2:3] * lv[3:4], axis=1, keepdims=True)) + lam_init)
                for hh in range(B_HEADS):
                    o = normalized(hh) - lam * normalized(hh + B_HEADS)
                    outs.append(_rmsnorm_t(o, sub_ref[...]) * (1.0 - lam_init))
            o_ref[0] = jnp.concatenate(outs, axis=0).T.astype(BF16)

    return init, single_pass, post


def _attention(proj_t, k_a, k_b, cos_t, sin_t, qn, lam_vecs, subln, lam_init, n_lat, ctx_into=None):
    b, _, n_tok = proj_t.shape
    is_ctx = ctx_into is not None
    if is_ctx:
        tq = tk = n_tok - n_lat
        nq, nkv = 1, 1
        q_off = k_off = n_lat // tq
    else:
        tq, tk = Q_TILE, KV_TILE
        nq, nkv = n_lat // tq, n_tok // tk
        q_off = k_off = 0
    q_spec = lambda blk: pl.BlockSpec((1, 512, tq), lambda bi, i, j: (bi, blk, i + q_off))
    k_spec = lambda w: pl.BlockSpec((1, tk, w), lambda bi, i, j: (bi, j + k_off, 0))
    v_spec = lambda w, blk: pl.BlockSpec((1, w, tk), lambda bi, i, j: (bi, blk, j + k_off))
    rope_spec = pl.BlockSpec((HALF, tq), lambda bi, i, j: (0, i + q_off))
    full = lambda a: pl.BlockSpec(a.shape, lambda bi, i, j: (0, 0))
    in_specs = [q_spec(0), k_spec(128), v_spec(128, 16), q_spec(1), k_spec(512), v_spec(512, 2),
                rope_spec, rope_spec, full(qn), full(lam_vecs), full(subln)]
    args = [proj_t, k_a, proj_t, proj_t, k_b, proj_t, cos_t, sin_t, qn, lam_vecs, subln]
    aliases = {}
    if is_ctx:
        in_specs += [pl.BlockSpec(memory_space=pl.ANY)] * 2
        args += list(ctx_into)
        aliases = {len(args) - 2: 0, len(args) - 1: 1}
    o_spec = pl.BlockSpec((1, tq, 512), lambda bi, i, j: (bi, i + q_off, 0))
    o_shape = jax.ShapeDtypeStruct((b, n_tok, 512), BF16)
    scratch = []
    for dv in (HEAD_DIM, B_V_DIM):
        scratch += [pltpu.VMEM((N_HEADS, LANES, tq), BF16),
                    pltpu.VMEM((N_HEADS, tq), F32),
                    pltpu.VMEM((2, N_HEADS, dv + SUM_ROWS, tq), F32),
                    pltpu.SMEM((1,), jnp.int32)]
    return pl.pallas_call(
        functools.partial(_attn_kernel, nkv=nkv, lam_init=lam_init, aliased=is_ctx),
        grid=(b, nq, nkv),
        in_specs=in_specs,
        out_specs=[o_spec, o_spec],
        out_shape=[o_shape, o_shape],
        scratch_shapes=scratch,
        input_output_aliases=aliases,
        compiler_params=_params(("arbitrary", "arbitrary", "arbitrary")),
        name="attn_ctx" if is_ctx else "attn_lat",
    )(*args)


def _nbr_tables(rows):
    nblk = rows // NBR_ROWS
    combos, index = {}, np.zeros((3, NBR_ROWS, BAND_ROWS // 2), np.int32)
    for v, blk in enumerate((0, 1, nblk - 1)):
        bs = min(max(NBR_ROWS * blk - NA_KH // 2, 0), rows - BAND_ROWS)
        for qr in range(NBR_ROWS):
            r = NBR_ROWS * blk + qr
            rs = min(max(r - NA_KH // 2, 0), rows - NA_KH)
            for kp in range(BAND_ROWS // 2):
                codes = []
                for kr in (bs + 2 * kp, bs + 2 * kp + 1):
                    codes.append(kr - r + NA_KH - 1 if rs <= kr < rs + NA_KH else 15)
                index[v, qr, kp] = combos.setdefault(tuple(codes), len(combos))
    lo = np.array([c[0] for c in combos], np.int32)
    hi = np.array([c[1] for c in combos], np.int32)
    return index.reshape(-1), lo, hi


def _nbr_bias_table(rpb, lo, hi):
    nh = rpb.shape[0]
    qc = np.arange(GRID_W)[:, None]
    kc = np.arange(GRID_W)[None, :]
    cs = np.clip(qc - NA_KW // 2, 0, GRID_W - NA_KW)
    inside = (kc >= cs) & (kc < cs + NA_KW)
    dc = np.clip(kc - qc + NA_KW - 1, 0, 2 * NA_KW - 2)
    onehot = (dc.reshape(-1)[None, :] == np.arange(2 * NA_KW - 1)[:, None]).astype(np.float32)
    toep = jnp.einsum("hrc,cq->hrq", rpb, jnp.asarray(onehot), precision=HIGHEST)
    toep = toep.reshape(nh, 2 * NA_KH - 1, GRID_W, GRID_W)
    toep = jnp.where(jnp.asarray(inside)[None, None], toep, NEG)
    toep = jnp.concatenate([toep, jnp.full((nh, 1, GRID_W, GRID_W), NEG, F32)], axis=1)
    return jnp.concatenate([toep[:, lo], toep[:, hi]], axis=-1)


def _pair_heads(q, lane):
    zero = jnp.zeros_like(q)
    return jnp.where(lane < HEAD_DIM, q, zero), jnp.where(lane >= HEAD_DIM, q, zero)


def _nbr_kernel(idx_ref, q_ref, k0, k1, k2, k3, v0, v1, v2, v3, kc_ref, vc_ref, tp_ref,
                o_ref, s_ref, *, nblk):
    blk = pl.program_id(1)
    variant = jnp.where(blk == 0, 0, jnp.where(blk == nblk - 1, 2, 1))
    lane = lax.broadcasted_iota(jnp.int32, (1, LANES), 1)
    outs = []
    for p in range(N_HEADS // 2):
        cols = slice(LANES * p, LANES * (p + 1))
        kband = jnp.concatenate([k0[0, :, cols], k1[0, :, cols], k2[0, :, cols], k3[0, :, cols]], axis=0)
        vband = jnp.concatenate([v0[0, :, cols], v1[0, :, cols], v2[0, :, cols], v3[0, :, cols]], axis=0)
        kctx, vctx = kc_ref[0, :, cols], vc_ref[0, :, cols]
        qp = q_ref[0, :, cols] * jnp.asarray(SM_SCALE, BF16)
        res = []
        for e, qm in enumerate(_pair_heads(qp, lane)):
            h = 2 * p + e
            s_raw = _dot_nt(qm, kband)
            sc = _dot_nt(qm, kctx)
            for qr in range(NBR_ROWS):
                rws = slice(GRID_W * qr, GRID_W * (qr + 1))
                for kp in range(BAND_ROWS // 2):
                    u = idx_ref[variant * (NBR_ROWS * BAND_ROWS // 2) + qr * (BAND_ROWS // 2) + kp]
                    cl = slice(LANES * kp, LANES * (kp + 1))
                    s_ref[rws, cl] = s_raw[rws, cl] + tp_ref[h, u]
            s = s_ref[...]
            m = jnp.maximum(jnp.max(s, axis=-1, keepdims=True), jnp.max(sc, axis=-1, keepdims=True))
            pn = jnp.exp(s - m)
            pc = jnp.exp(sc - m)
            l = jnp.sum(pn, axis=-1, keepdims=True) + jnp.sum(pc, axis=-1, keepdims=True)
            o = _dot(pn.astype(BF16), vband) + _dot(pc.astype(BF16), vctx)
            res.append(o * (1.0 / l))
        outs.append(jnp.where(lane < HEAD_DIM, res[0], res[1]))
    o_ref[0] = jnp.concatenate(outs, axis=1).astype(BF16)


def _nbr_attention(proj_n, idx, tp, n_lat):
    b, n_tok, _ = proj_n.shape
    rows = n_lat // GRID_W
    nblk = rows // NBR_ROWS
    tq = NBR_ROWS * GRID_W
    blk_rows = CHUNK // GRID_W
    n_band = BAND_ROWS // blk_rows
    max_start = rows // blk_rows - n_band
    ctx_blk = n_lat // CHUNK
    q_col, k_col, v_col = 6, 7, 8

    def band_spec(j, col):
        def imap(bi, i, idx_ref):
            start = jnp.clip(2 * i - 1, 0, max_start)
            return (bi, start + j, col)
        return pl.BlockSpec((1, CHUNK, 512), imap)

    in_specs = [pl.BlockSpec((1, tq, 512), lambda bi, i, idx_ref: (bi, i, q_col))]
    in_specs += [band_spec(j, k_col) for j in range(n_band)]
    in_specs += [band_spec(j, v_col) for j in range(n_band)]
    in_specs += [pl.BlockSpec((1, CHUNK, 512), lambda bi, i, idx_ref: (bi, ctx_blk, k_col)),
                 pl.BlockSpec((1, CHUNK, 512), lambda bi, i, idx_ref: (bi, ctx_blk, v_col)),
                 pl.BlockSpec(tp.shape, lambda bi, i, idx_ref: (0, 0, 0, 0))]
    grid_spec = pltpu.PrefetchScalarGridSpec(
        num_scalar_prefetch=1, grid=(b, nblk), in_specs=in_specs,
        out_specs=pl.BlockSpec((1, tq, 512), lambda bi, i, idx_ref: (bi, i, 0)),
        scratch_shapes=[pltpu.VMEM((tq, BAND_ROWS * GRID_W), F32)])
    return pl.pallas_call(
        functools.partial(_nbr_kernel, nblk=nblk),
        grid_spec=grid_spec,
        out_shape=jax.ShapeDtypeStruct((b, n_tok, 512), BF16),
        compiler_params=_params(("arbitrary", "arbitrary")),
        name="nbr_attn",
    )(idx, *([proj_n] * (1 + 2 * n_band + 2)), tp)


def _ctx_c_kernel(q_ref, k_ref, v_ref, prev_ref, o_ref):
    del prev_ref
    lane = lax.broadcasted_iota(jnp.int32, (1, LANES), 1)
    outs = []
    for p in range(N_HEADS // 2):
        cols = slice(LANES * p, LANES * (p + 1))
        kp_, vp = k_ref[0, :, cols], v_ref[0, :, cols]
        qp = q_ref[0, :, cols] * jnp.asarray(SM_SCALE, BF16)
        res = []
        for qm in _pair_heads(qp, lane):
            s = _dot_nt(qm, kp_)
            m = jnp.max(s, axis=-1, keepdims=True)
            pr = jnp.exp(s - m)
            l = jnp.sum(pr, axis=-1, keepdims=True)
            res.append(_dot(pr.astype(BF16), vp) * (1.0 / l))
        outs.append(jnp.where(lane < HEAD_DIM, res[0], res[1]))
    o_ref[0] = jnp.concatenate(outs, axis=1).astype(BF16)


def _ctx_c_attention(proj_n, o_c, n_lat):
    b, n_tok, _ = proj_n.shape
    n_ctx = n_tok - n_lat
    blk = n_lat // n_ctx
    return pl.pallas_call(
        _ctx_c_kernel,
        grid=(b,),
        in_specs=[pl.BlockSpec((1, n_ctx, 512), lambda bi: (bi, blk, 6)),
                  pl.BlockSpec((1, n_ctx, 512), lambda bi: (bi, blk, 7)),
                  pl.BlockSpec((1, n_ctx, 512), lambda bi: (bi, blk, 8)),
                  pl.BlockSpec(memory_space=pl.ANY)],
        out_specs=pl.BlockSpec((1, n_ctx, 512), lambda bi: (bi, blk, 0)),
        out_shape=jax.ShapeDtypeStruct(o_c.shape, BF16),
        input_output_aliases={3: 0},
        compiler_params=_params(("arbitrary",)),
        name="ctx_c_attn",
    )(proj_n, proj_n, proj_n, o_c)


def _merge_kernel(*refs, with_router):
    (oa_ref, ob_ref, oc_ref, gate_ref, x_ref, wa_ref, wb_ref, wc_ref, wo_ref,
     mod_ref, g_ref) = refs[:11]
    if with_router:
        r_ref, xo_ref, h_ref, route_ref = refs[11:]
    else:
        xo_ref, h_ref = refs[11:]
    d = D_MODEL
    gate = lambda k: _sigmoid(gate_ref[0, :, k * d:(k + 1) * d].astype(F32))
    m = (gate(0) * _dot(oa_ref[0], wa_ref[...])
         + gate(1) * _dot(ob_ref[0], wb_ref[...])
         + gate(2) * _dot(oc_ref[0], wc_ref[...]))
    y = _dot(m.astype(BF16), wo_ref[...])
    for c in range(ROW_TILE // CHUNK):
        rows = slice(c * CHUNK, (c + 1) * CHUNK)
        xn = x_ref[0, rows, :] + mod_ref[0, c, 2:3, :] * y[rows, :]
        xo_ref[0, rows, :] = xn
        h = _norm_mod(xn, g_ref[...], mod_ref[0, c, 3:4, :], mod_ref[0, c, 4:5, :])
        if not with_router:
            h_ref[0, rows, :] = h.astype(BF16)
        else:
            h_ref[0, rows, :] = h
            lane = lax.broadcasted_iota(jnp.int32, (CHUNK, LANES), 1)
            logits = jnp.where(lane < N_EXPERTS, _dot(h, r_ref[...], precision=HIGHEST), NEG)
            m1 = jnp.max(logits, axis=-1, keepdims=True)
            i1 = jnp.min(jnp.where(logits == m1, lane, LANES), axis=-1, keepdims=True)
            rest = jnp.where(lane == i1, NEG, logits)
            m2 = jnp.max(rest, axis=-1, keepdims=True)
            i2 = jnp.min(jnp.where(rest == m2, lane, LANES), axis=-1, keepdims=True)
            e = jnp.exp(m2 - m1)
            w1 = 1.0 / (1.0 + e)
            route_ref[0, rows, :] = jnp.where(
                lane == 0, i1.astype(F32), jnp.where(
                    lane == 1, i2.astype(F32), jnp.where(
                        lane == 2, w1, jnp.where(lane == 3, e * w1, 0.0))))


def _merge(o_a, o_b, o_c, proj_n, x_all, wa, wb, wc, wo, modc, g_ffn, router):
    b, n_tok, d = x_all.shape
    nt = n_tok // ROW_TILE
    with_router = router is not None
    tile = lambda w: pl.BlockSpec((1, ROW_TILE, w), lambda bi, i: (bi, i, 0))
    full = lambda a: pl.BlockSpec(a.shape, lambda bi, i: (0,) * a.ndim)
    in_specs = [tile(512), tile(512), tile(512), tile(3 * d), tile(d),
                full(wa), full(wb), full(wc), full(wo),
                pl.BlockSpec((1, ROW_TILE // CHUNK, 6, d), lambda bi, i: (bi, i, 0, 0)),
                full(g_ffn)]
    args = [o_a, o_b, o_c, proj_n, x_all, wa, wb, wc, wo, modc, g_ffn]
    out_specs = [tile(d), tile(d)]
    out_shape = [jax.ShapeDtypeStruct((b, n_tok, d), F32), jax.ShapeDtypeStruct((b, n_tok, d), BF16)]
    if with_router:
        in_specs.append(full(router))
        args.append(router)
        out_shape[1] = jax.ShapeDtypeStruct((b, n_tok, d), F32)
        out_specs.append(tile(LANES))
        out_shape.append(jax.ShapeDtypeStruct((b, n_tok, LANES), F32))
    return pl.pallas_call(
        functools.partial(_merge_kernel, with_router=with_router),
        grid=(b, nt),
        in_specs=in_specs,
        out_specs=out_specs,
        out_shape=out_shape,
        compiler_params=_params(("arbitrary", "arbitrary")),
        name="merge_router" if with_router else "merge",
    )(*args)


def _swiglu_partial(h, w1, w3, w2):
    a = _dot(h, w1)
    t = (a * _sigmoid(a)) * _dot(h, w3)
    return _dot(t.astype(BF16), w2)


def _ffn_kernel(h_ref, x_ref, w1_ref, w3_ref, w2_ref, mod_ref, o_ref, acc_ref, *, n_f):
    @pl.when(pl.program_id(2) == 0)
    def _():
        acc_ref[...] = jnp.zeros(acc_ref.shape, F32)

    acc_ref[...] += _swiglu_partial(h_ref[0], w1_ref[0], w3_ref[0], w2_ref[0])

    @pl.when(pl.program_id(2) == n_f - 1)
    def _():
        for c in range(ROW_TILE // CHUNK):
            rows = slice(c * CHUNK, (c + 1) * CHUNK)
            o_ref[0, rows, :] = x_ref[0, rows, :] + mod_ref[0, c, 5:6, :] * acc_ref[rows, :]


def _ff_tile(f, cap):
    return max(t for t in range(LANES, min(f, cap) + 1, LANES) if f % t == 0)


def _ffn(h2, x_all, w1, w3, w2, li, modc):
    b, n_tok, d = x_all.shape
    nt = n_tok // ROW_TILE
    f = w1.shape[-1]
    tf = _ff_tile(f, 1408)
    n_f = f // tf
    tile = lambda w: pl.BlockSpec((1, ROW_TILE, w), lambda bi, i, j: (bi, i, 0))
    w13 = pl.BlockSpec((1, d, tf), lambda bi, i, j: (li, 0, j))
    w2s = pl.BlockSpec((1, tf, d), lambda bi, i, j: (li, j, 0))
    mods = pl.BlockSpec((1, ROW_TILE // CHUNK, 6, d), lambda bi, i, j: (bi, i, 0, 0))
    return pl.pallas_call(
        functools.partial(_ffn_kernel, n_f=n_f),
        grid=(b, nt, n_f),
        in_specs=[tile(d), tile(d), w13, w13, w2s, mods],
        out_specs=tile(d),
        out_shape=jax.ShapeDtypeStruct((b, n_tok, d), F32),
        scratch_shapes=[pltpu.VMEM((ROW_TILE, d), F32)],
        compiler_params=_params(("arbitrary",) * 3),
        name="ffn",
    )(h2, x_all, w1, w3, w2, modc)


def _row_copy(src_ref, dst_ref, src_row, dst_row, sem):
    return pltpu.make_async_copy(src_ref.at[pl.ds(src_row, 1), :], dst_ref.at[pl.ds(dst_row, 1), :], sem)


def _gather_start(idx_ref, idx_base, n_rows, src_ref, dst_ref, sem):
    def issue(j, carry):
        for prio in range(DMA_PRIORITIES):
            i = DMA_PRIORITIES * j + prio
            _row_copy(src_ref, dst_ref, idx_ref[idx_base + i], i, sem).start(priority=prio)
        return carry

    lax.fori_loop(0, n_rows // DMA_PRIORITIES, issue, 0, unroll=4)


def _gather_wait(n_rows, src_ref, dst_ref, sem):
    def drain(i, carry):
        _row_copy(src_ref, dst_ref, 0, i, sem).wait()
        return carry

    lax.fori_loop(0, n_rows, drain, 0, unroll=8)


def _dispatch_kernel(idx_ref, src_ref, o_ref, sem):
    _gather_start(idx_ref, pl.program_id(0) * MOE_TILE, MOE_TILE, src_ref, o_ref, sem)
    _gather_wait(MOE_TILE, src_ref, o_ref, sem)


def _dispatch(h_rows, slot_token):
    n_slots = slot_token.shape[0]
    d = h_rows.shape[1]
    grid_spec = pltpu.PrefetchScalarGridSpec(
        num_scalar_prefetch=1, grid=(n_slots // MOE_TILE,),
        in_specs=[pl.BlockSpec(memory_space=pl.ANY)],
        out_specs=pl.BlockSpec((MOE_TILE, d), lambda t, idx: (t, 0)),
        scratch_shapes=[pltpu.SemaphoreType.DMA])
    return pl.pallas_call(
        _dispatch_kernel,
        grid_spec=grid_spec,
        out_shape=jax.ShapeDtypeStruct((n_slots, d), F32),
        compiler_params=_params(("arbitrary",)),
        name="moe_dispatch",
    )(slot_token, h_rows)


def _expert_kernel(te_ref, tv_ref, xg_ref, w1_ref, w3_ref, w2_ref, y_ref, h_ref, acc_ref, *, n_f):
    del te_ref
    t, f = pl.program_id(0), pl.program_id(1)

    @pl.when(tv_ref[t] != 0)
    def _():
        @pl.when(f == 0)
        def _():
            h_ref[...] = xg_ref[...].astype(BF16)
            acc_ref[...] = jnp.zeros(acc_ref.shape, F32)

        acc_ref[...] += _swiglu_partial(h_ref[...], w1_ref[0, 0], w3_ref[0, 0], w2_ref[0, 0])

        @pl.when(f == n_f - 1)
        def _():
            y_ref[...] = acc_ref[...]

    @pl.when((tv_ref[t] == 0) & (f == n_f - 1))
    def _():
        y_ref[...] = jnp.zeros(y_ref.shape, F32)


def _experts(xg, tile_expert, tile_valid, w1, w3, w2, li):
    n_slots = xg.shape[0]
    d, f = w1.shape[-2:]
    tf = _ff_tile(f, 896)
    n_f = f // tf
    grid_spec = pltpu.PrefetchScalarGridSpec(
        num_scalar_prefetch=2, grid=(n_slots // MOE_TILE, n_f),
        in_specs=[pl.BlockSpec((MOE_TILE, d), lambda t, j, te, tv: (t, 0)),
                  pl.BlockSpec((1, 1, d, tf), lambda t, j, te, tv: (li, te[t], 0, j)),
                  pl.BlockSpec((1, 1, d, tf), lambda t, j, te, tv: (li, te[t], 0, j)),
                  pl.BlockSpec((1, 1, tf, d), lambda t, j, te, tv: (li, te[t], j, 0))],
        out_specs=pl.BlockSpec((MOE_TILE, d), lambda t, j, te, tv: (t, 0)),
        scratch_shapes=[pltpu.VMEM((MOE_TILE, d), BF16), pltpu.VMEM((MOE_TILE, d), F32)])
    return pl.pallas_call(
        functools.partial(_expert_kernel, n_f=n_f),
        grid_spec=grid_spec,
        out_shape=jax.ShapeDtypeStruct((n_slots, d), F32),
        compiler_params=_params(("arbitrary", "arbitrary")),
        name="moe_experts",
    )(tile_expert, tile_valid, xg, w1, w3, w2)


def _combine_kernel(pos_ref, y_ref, x_ref, route_ref, mod_ref, o_ref, g1_ref, g2_ref, sems, *, n_rows):
    base = (pl.program_id(0) * pl.num_programs(1) + pl.program_id(1)) * ROW_TILE
    _gather_start(pos_ref, base, ROW_TILE, y_ref, g1_ref, sems.at[0])
    _gather_start(pos_ref, n_rows + base, ROW_TILE, y_ref, g2_ref, sems.at[1])
    _gather_wait(ROW_TILE, y_ref, g1_ref, sems.at[0])
    _gather_wait(ROW_TILE, y_ref, g2_ref, sems.at[1])
    for c in range(ROW_TILE // CHUNK):
        rows = slice(c * CHUNK, (c + 1) * CHUNK)
        w1 = route_ref[0, rows, 2:3]
        w2 = route_ref[0, rows, 3:4]
        y = w1 * g1_ref[rows, :] + w2 * g2_ref[rows, :]
        o_ref[0, rows, :] = x_ref[0, rows, :] + mod_ref[0, c, 5:6, :] * y


def _combine(y, pos, x_mid, route, modc):
    b, n_tok, d = x_mid.shape
    nt = n_tok // ROW_TILE
    tile = lambda w: pl.BlockSpec((1, ROW_TILE, w), lambda bi, i, p: (bi, i, 0))
    grid_spec = pltpu.PrefetchScalarGridSpec(
        num_scalar_prefetch=1, grid=(b, nt),
        in_specs=[pl.BlockSpec(memory_space=pl.ANY), tile(d), tile(LANES),
                  pl.BlockSpec((1, ROW_TILE // CHUNK, 6, d), lambda bi, i, p: (bi, i, 0, 0))],
        out_specs=tile(d),
        scratch_shapes=[pltpu.VMEM((ROW_TILE, d), F32),
                        pltpu.VMEM((ROW_TILE, d), F32),
                        pltpu.SemaphoreType.DMA((2,))])
    return pl.pallas_call(
        functools.partial(_combine_kernel, n_rows=b * n_tok),
        grid_spec=grid_spec,
        out_shape=jax.ShapeDtypeStruct((b, n_tok, d), F32),
        compiler_params=_params(("arbitrary", "arbitrary")),
        name="moe_combine",
    )(pos, y, x_mid, route, modc)


def _routing_tables(route):
    n_rows = route.shape[0] * route.shape[1]
    r2 = route.reshape(n_rows, LANES)
    e_flat = jnp.concatenate([r2[:, 0], r2[:, 1]]).astype(jnp.int32)
    onehot = (e_flat[:, None] == jnp.arange(N_EXPERTS, dtype=jnp.int32)[None, :]).astype(jnp.int32)
    csum = jnp.cumsum(onehot, axis=0)
    rank = jnp.take_along_axis(csum, e_flat[:, None], axis=1)[:, 0] - 1
    padded = ((csum[-1] + MOE_TILE - 1) // MOE_TILE) * MOE_TILE
    ends = jnp.cumsum(padded)
    pos = (ends - padded)[e_flat] + rank
    n_slots = 2 * n_rows + N_EXPERTS * MOE_TILE
    token = jnp.arange(2 * n_rows, dtype=jnp.int32) % n_rows
    slot_token = jnp.zeros((n_slots,), jnp.int32).at[pos].set(token, unique_indices=True, mode="promise_in_bounds")
    tile_start = jnp.arange(n_slots // MOE_TILE, dtype=jnp.int32) * MOE_TILE
    tile_expert = jnp.minimum(jnp.searchsorted(ends, tile_start, side="right"), N_EXPERTS - 1)
    tile_valid = (tile_start < ends[-1]).astype(jnp.int32)
    return slot_token, tile_expert.astype(jnp.int32), tile_valid, pos.astype(jnp.int32)


def _moe(h_rows, x_mid, route, w1, w3, w2, li, modc):
    b, n_tok = x_mid.shape[:2]
    slot_token, tile_expert, tile_valid, pos = _routing_tables(route)
    xg = _dispatch(h_rows.reshape(b * n_tok, -1), slot_token)
    y = _experts(xg, tile_expert, tile_valid, w1, w3, w2, li)
    return _combine(y, pos, x_mid, route, modc)


def _final_kernel(x_ref, g_ref, o_ref):
    x = x_ref[0]
    ms = jnp.mean(x * x, axis=-1, keepdims=True)
    o_ref[0] = x * lax.rsqrt(ms + EPS) * g_ref[...]


def _final_norm(x_all, g, n_lat):
    b, _, d = x_all.shape
    tm = 1024
    return pl.pallas_call(
        _final_kernel,
        grid=(b, n_lat // tm),
        in_specs=[pl.BlockSpec((1, tm, d), lambda bi, i: (bi, i, 0)),
                  pl.BlockSpec((1, d), lambda bi, i: (0, 0))],
        out_specs=pl.BlockSpec((1, tm, d), lambda bi, i: (bi, i, 0)),
        out_shape=jax.ShapeDtypeStruct((b, n_lat, d), F32),
        compiler_params=_params(("arbitrary", "arbitrary")),
        name="final_norm",
    )(x_all, g)


def _deinterleave(w, n_heads):
    lead = w.shape[:-1]
    w = w.reshape(lead + (n_heads, HALF, 2))
    return jnp.swapaxes(w, -1, -2).reshape(lead + (n_heads * HEAD_DIM,))


def _rope_tables(n_lat, n_tok):
    t = np.arange(n_lat)
    inv = ROPE_THETA ** (-np.arange(HEAD_DIM // 4, dtype=np.float64) / (HEAD_DIM // 4))
    ang = np.concatenate([(t // GRID_W)[:, None] * inv, (t % GRID_W)[:, None] * inv], axis=-1)
    ang = np.concatenate([ang, np.zeros((n_tok - n_lat, HALF))], axis=0)
    return jnp.asarray(np.cos(ang).T, F32), jnp.asarray(np.sin(ang).T, F32)


def kernel(x, c, ctx, c_ctx, w_mod, b_mod, g_mix, g_ffn, w_in, qn_a, kn_a, lam_q1, lam_k1, lam_q2,
           lam_k2, subln_b, rpb_c, w_br_a, w_br_b, w_br_c, w_out, ffn_w1, ffn_w3, ffn_w2, router,
           moe_w1, moe_w3, moe_w2, g_final):
    b, n_lat, d = x.shape
    n_ctx = ctx.shape[1]
    n_tok = n_lat + n_ctx
    depth = w_in.shape[0]
    assert d == D_MODEL and n_ctx == CHUNK and b + 1 <= 8
    assert n_tok % ROW_TILE == 0 and n_tok % KV_TILE == 0 and n_lat % Q_TILE == 0
    assert n_lat % (NBR_ROWS * GRID_W) == 0 and n_lat // GRID_W >= 24
    n_chunks = n_tok // CHUNK

    x_all = jnp.concatenate([x, ctx], axis=1)
    cos_t, sin_t = _rope_tables(n_lat, n_tok)

    c_rows = jnp.concatenate([c, c_ctx[None], jnp.zeros((7 - b, d), F32)], axis=0)
    mods = _mod_vectors(c_rows, w_mod, b_mod).reshape(depth, 8, 6, d)
    mod_lat = jnp.broadcast_to(mods[:, :b, None], (depth, b, n_chunks - 1, 6, d))
    mod_ctx = jnp.broadcast_to(mods[:, b:b + 1, None], (depth, b, 1, 6, d))
    mod_chunks = jnp.concatenate([mod_lat, mod_ctx], axis=2)

    sl = lambda name: w_in[:, :, _OFF[name][0]:_OFF[name][1]]
    w_n = jnp.concatenate([sl("gate"), sl("cq"), sl("ck"), sl("cv")], axis=-1).astype(BF16)
    w_t = jnp.concatenate([_deinterleave(sl("aq"), 8), _deinterleave(sl("bq"), 8), sl("bv"),
                           _deinterleave(sl("bk"), 8), sl("av"), _deinterleave(sl("ak"), 2)], axis=-1)
    w_t = jnp.swapaxes(w_t, 1, 2).astype(BF16)
    qn_t = _deinterleave(qn_a, 1)[:, :, None]
    kn_t = _deinterleave(kn_a, 1)[:, :, None]
    lam_vecs = jnp.pad(jnp.stack([lam_q1, lam_k1, lam_q2, lam_k2], axis=1),
                       ((0, 0), (0, 4), (0, LANES - HEAD_DIM)))
    wa, wb, wc, wo = (w.astype(BF16) for w in (w_br_a, w_br_b, w_br_c, w_out))
    f1, f3, f2 = (w.astype(BF16) for w in (ffn_w1, ffn_w3, ffn_w2))
    m1, m3, m2 = (w.astype(BF16) for w in (moe_w1, moe_w3, moe_w2))
    router_p = jnp.pad(router, ((0, 0), (0, 0), (0, LANES - N_EXPERTS)))
    nbr_idx, nbr_lo, nbr_hi = _nbr_tables(n_lat // GRID_W)
    nbr_idx = jnp.asarray(nbr_idx)

    for layer in range(depth):
        lam_init = 0.8 - 0.6 * math.exp(-0.3 * layer)
        modc = mod_chunks[layer]
        g_mix_l = g_mix[layer][None]
        proj_n = _inproj(x_all, g_mix_l, modc, w_n[layer])
        proj_t, k_b, k_a = _inproj(x_all, g_mix_l, modc, w_t[layer], rope=(cos_t, sin_t, kn_t[layer]))

        attn_args = (proj_t, k_a, k_b, cos_t, sin_t, qn_t[layer], lam_vecs[layer], subln_b[layer][:, None],
                     lam_init, n_lat)
        o_a, o_b = _attention(*attn_args, ctx_into=_attention(*attn_args))
        tp = _nbr_bias_table(rpb_c[layer], nbr_lo, nbr_hi)
        o_c = _nbr_attention(proj_n, nbr_idx, tp, n_lat)
        o_c = _ctx_c_attention(proj_n, o_c, n_lat)

        is_moe = layer % 2 == 1
        li = layer // 2
        merged = _merge(o_a, o_b, o_c, proj_n, x_all, wa[layer], wb[layer], wc[layer], wo[layer],
                        modc, g_ffn[layer][None], router_p[li] if is_moe else None)
        if is_moe:
            x_mid, h_rows, route = merged
            x_all = _moe(h_rows, x_mid, route, m1, m3, m2, li, modc)
        else:
            x_mid, h2 = merged
            x_all = _ffn(h2, x_mid, f1, f3, f2, li, modc)

    return _final_norm(x_all, g_final[None], n_lat)
```
